```python
import math
import jax
import jax.numpy as jnp
from jax import lax
import numpy as np

D_MODEL = 4096
BATCH = 2
SEQ = 8192
DEPTH = 2

GRID_W = 64
CTX_LEN = 256
EPS = 1e-6
NEG_BIG = -1e30
TINY = 1e-30

HG_HEADS = 12
HG_DK = 128
HG_DV = 128
HG_W = HG_HEADS * HG_DV

DA_HEADS = 12
DA_QK = 64
DA_V = 2 * DA_QK
DA_W = DA_HEADS * DA_V
ROPE_THETA = 10000.0
Q_BLOCK = 128

ML_HEADS = 12
ML_DK = 128
ML_DV = 128
ML_W = ML_HEADS * ML_DV

CHUNK = 64
N_BRANCH = 3
BRANCH_W = 1536

N_EXPERTS = 32
TOP_K = 4
D_EXPERT = 384
D_SHARED = 384
ROUTED_SCALE = 2.5

IN_LAYOUT = (
    ('hg_q', HG_HEADS * HG_DK),
    ('hg_f_fwd', HG_HEADS * HG_DK),
    ('hg_f_bwd', HG_HEADS * HG_DK),
    ('hg_i', HG_W),
    ('hg_g', HG_W),
    ('da_q', DA_HEADS * 2 * DA_QK),
    ('da_k', DA_HEADS * 2 * DA_QK),
    ('da_v', DA_W),
    ('ml_q', ML_HEADS * ML_DK),
    ('ml_k', ML_HEADS * ML_DK),
    ('ml_v', ML_W),
    ('ml_o', ML_W),
    ('ml_i_fwd', ML_HEADS),
    ('ml_i_bwd', ML_HEADS),
    ('ml_f_fwd', ML_HEADS),
    ('ml_f_bwd', ML_HEADS),
    ('gates', N_BRANCH * D_MODEL),
)
D_IN = sum(w for _, w in IN_LAYOUT)

kernel_name = 'hybrid_hgrn2_diffattn_mlstm_moe_dit'


def rmsnorm(x, g):
    xf = x.astype(jnp.float32)
    y = xf * lax.rsqrt(jnp.mean(xf * xf, axis=-1, keepdims=True) + EPS)
    return (y * g.astype(jnp.float32)).astype(x.dtype)


def split_cols(p):
    names = [n for n, _ in IN_LAYOUT]
    cuts = np.cumsum([w for _, w in IN_LAYOUT])[:-1].tolist()
    return dict(zip(names, jnp.split(p, cuts, axis=-1)))


def to_heads(a, h):
    b, t, _ = a.shape
    return a.reshape(b, t, h, -1).transpose(0, 2, 1, 3)


def from_heads(a):
    b, h, t, d = a.shape
    return a.transpose(0, 2, 1, 3).reshape(b, t, h * d)


def to_chunks(a):
    b, h, t = a.shape[:3]
    return jnp.moveaxis(a.reshape((b, h, t // CHUNK, CHUNK) + a.shape[3:]), 2, 0)


def from_chunks(a):
    nc, b, h, l = a.shape[:4]
    return jnp.moveaxis(a, 0, 2).reshape((b, h, nc * l) + a.shape[4:])


def axial_rope_tables(rows, dim):
    row = jnp.repeat(jnp.arange(rows, dtype=jnp.float32), GRID_W)
    col = jnp.tile(jnp.arange(GRID_W, dtype=jnp.float32), rows)
    n = dim // 4
    inv = ROPE_THETA ** (-jnp.arange(n, dtype=jnp.float32) / n)
    ang = jnp.concatenate([row[:, None] * inv, col[:, None] * inv], axis=-1)
    return jnp.cos(ang), jnp.sin(ang)


def apply_rope(a, cos, sin):
    a1, a2 = jnp.split(a.astype(jnp.float32), 2, axis=-1)
    cs, sn = cos[:, None, None, :], sin[:, None, None, :]
    return jnp.concatenate([a1 * cs - a2 * sn, a2 * cs + a1 * sn], axis=-1).astype(a.dtype)


def bidirectional(scan_fn, ctx_dirs, lat_dirs, init):
    o_ctx, o_lat = 0.0, 0.0
    for d in range(2):
        ca, la = ctx_dirs[d], lat_dirs[d]
        if d == 1:
            ca = [jnp.flip(a, axis=2) for a in ca]
            la = [jnp.flip(a, axis=2) for a in la]
        oc, state = scan_fn(*ca, init)
        ol, _ = scan_fn(*la, state)
        if d == 1:
            oc, ol = jnp.flip(oc, axis=2), jnp.flip(ol, axis=2)
        o_ctx, o_lat = o_ctx + oc, o_lat + ol
    return o_ctx, o_lat


def hgrn2_scan(q, k, v, log_f, s0):
    out_dtype = v.dtype
    mask = jnp.tril(jnp.ones((CHUNK, CHUNK), bool))[:, :, None]

    def step(s, inp):
        qc, kc, vc, lf = [a.astype(jnp.float32) for a in inp]
        bcum = jnp.cumsum(lf, axis=2)
        b_last = bcum[:, :, -1:]
        decay = jnp.exp(jnp.where(mask, bcum[:, :, :, None] - bcum[:, :, None], NEG_BIG))
        att = jnp.einsum('bhtd,bhsd,bhtsd->bhts', qc, kc, decay)
        o = (jnp.einsum('bhts,bhse->bhte', att, vc)
             + jnp.einsum('bhtd,bhde->bhte', qc * jnp.exp(bcum), s))
        s = (jnp.exp(b_last[:, :, 0])[..., None] * s
             + jnp.einsum('bhsd,bhse->bhde', kc * jnp.exp(b_last - bcum), vc))
        return s, o

    s, o = lax.scan(step, s0, tuple(to_chunks(a) for a in (q, k, v, log_f)))
    return from_chunks(o).astype(out_dtype), s


def mlstm_scan(q, k, v, ig, log_f, state):
    out_dtype = v.dtype
    mask = jnp.tril(jnp.ones((CHUNK, CHUNK), bool))

    def step(carry, inp):
        cmat, nvec, m = carry
        qc, kc, vc, ic, lf = [a.astype(jnp.float32) for a in inp]
        bcum = jnp.cumsum(lf, axis=-1)
        dlog = jnp.where(mask, bcum[..., :, None] - bcum[..., None, :] + ic[..., None, :], NEG_BIG)
        inter = bcum + m[..., None]
        m_t = jnp.maximum(jnp.max(dlog, axis=-1), inter)
        w = jnp.exp(dlog - m_t[..., None]) * jnp.einsum('bhtd,bhsd->bhts', qc, kc)
        w_inter = jnp.exp(inter - m_t)
        num = (jnp.einsum('bhts,bhse->bhte', w, vc)
               + w_inter[..., None] * jnp.einsum('bhtd,bhde->bhte', qc, cmat))
        den = jnp.sum(w, axis=-1) + w_inter * jnp.einsum('bhtd,bhd->bht', qc, nvec)
        h = num / jnp.maximum(jnp.abs(den), jnp.exp(-m_t))[..., None]
        m_new = m_t[..., -1]
        wk = jnp.exp(bcum[..., -1:] - bcum + ic - m_new[..., None])
        dec = jnp.exp(bcum[..., -1] + m - m_new)
        cmat = dec[..., None, None] * cmat + jnp.einsum('bhs,bhsd,bhse->bhde', wk, kc, vc)
        nvec = dec[..., None] * nvec + jnp.einsum('bhs,bhsd->bhd', wk, kc)
        return (cmat, nvec, m_new), h

    state, h = lax.scan(step, state, tuple(to_chunks(a) for a in (q, k, v, ig, log_f)))
    return from_chunks(h).astype(out_dtype), state


def hgrn2_mixer(px, pc, lb, g_norm, need_ctx):
    def prep(p):
        q = to_heads(jax.nn.silu(p['hg_q']) * HG_DK ** -0.5, HG_HEADS)
        v = to_heads(p['hg_i'], HG_HEADS)
        dirs = []
        for d, name in enumerate(('hg_f_fwd', 'hg_f_bwd')):
            fp = p[name].astype(jnp.float32)
            f = lb[d] + (1.0 - lb[d]) * jax.nn.sigmoid(fp)
            log_f = jnp.log(jnp.maximum(f, TINY))
            k = (1.0 - lb[d]) * jax.nn.sigmoid(-fp)
            dirs.append((q, to_heads(k, HG_HEADS), v, to_heads(log_f, HG_HEADS)))
        return dirs

    b = px['hg_q'].shape[0]
    s0 = jnp.zeros((b, HG_HEADS, HG_DK, HG_DV), jnp.float32)
    o_c, o_x = bidirectional(hgrn2_scan, prep(pc), prep(px), s0)

    def finish(o, p):
        return from_heads(rmsnorm(o, g_norm)) * jax.nn.silu(p['hg_g'])

    return finish(o_x, px), (finish(o_c, pc) if need_ctx else None)


def diff_attend(q, k, v, lam):
    s = jnp.einsum('bqhcd,bkhcd->bhcqk', q, k).astype(jnp.float32)
    p = jax.nn.softmax(s, axis=-1)
    a = p[:, :, 0] - lam * p[:, :, 1]
    return jnp.einsum('bhqk,bkhe->bqhe', a.astype(v.dtype), v)


def diff_attention(px, pc, cos, sin, q_norm, k_norm, lam_vec, sub_norm, lam_init, need_ctx):
    def qk(a, g, rope):
        a = rmsnorm(a.reshape(a.shape[0], a.shape[1], DA_HEADS, 2, DA_QK), g)
        return apply_rope(a, cos, sin) if rope else a

    def vals(a):
        return a.reshape(a.shape[0], a.shape[1], DA_HEADS, DA_V)

    lv = lam_vec.astype(jnp.float32)
    lam = jnp.exp(jnp.sum(lv[0] * lv[1])) - jnp.exp(jnp.sum(lv[2] * lv[3])) + lam_init
    k_c, v_c = qk(pc['da_k'], k_norm, False), vals(pc['da_v'])
    q_x = qk(px['da_q'], q_norm, True) * DA_QK ** -0.5
    k_all = jnp.concatenate([k_c, qk(px['da_k'], k_norm, True)], axis=1)
    v_all = jnp.concatenate([v_c, vals(px['da_v'])], axis=1)
    b, t = q_x.shape[:2]
    nb = t // Q_BLOCK
    q_blocks = jnp.moveaxis(q_x.reshape(b, nb, Q_BLOCK, DA_HEADS, 2, DA_QK), 1, 0)
    o_x = lax.map(lambda qb: diff_attend(qb, k_all, v_all, lam), q_blocks)
    o_x = jnp.moveaxis(o_x, 0, 1).reshape(b, t, DA_HEADS, DA_V)

    def finish(o):
        return (rmsnorm(o, sub_norm) * (1.0 - lam_init)).reshape(o.shape[0], o.shape[1], DA_W)

    out_c = None
    if need_ctx:
        q_c = qk(pc['da_q'], q_norm, False) * DA_QK ** -0.5
        out_c = finish(diff_attend(q_c, k_c, v_c, lam))
    return finish(o_x), out_c


def mlstm_mixer(px, pc, i_bias, f_bias, g_norm, need_ctx):
    def prep(p):
        q = to_heads(p['ml_q'], ML_HEADS)
        k = to_heads(p['ml_k'] * ML_DK ** -0.5, ML_HEADS)
        v = to_heads(p['ml_v'], ML_HEADS)
        dirs = []
        for d, (iname, fname) in enumerate((('ml_i_fwd', 'ml_f_fwd'), ('ml_i_bwd', 'ml_f_bwd'))):
            ig = (p[iname].astype(jnp.float32) + i_bias[d]).transpose(0, 2, 1)
            log_f = jax.nn.log_sigmoid(p[fname].astype(jnp.float32) + f_bias[d]).transpose(0, 2, 1)
            dirs.append((q, k, v, ig, log_f))
        return dirs

    b = px['ml_q'].shape[0]
    state0 = (jnp.zeros((b, ML_HEADS, ML_DK, ML_DV), jnp.float32),
              jnp.zeros((b, ML_HEADS, ML_DK), jnp.float32),
              jnp.zeros((b, ML_HEADS), jnp.float32))
    h_c, h_x = bidirectional(mlstm_scan, prep(pc), prep(px), state0)

    def finish(h, p):
        return from_heads(rmsnorm(h, g_norm)) * jax.nn.sigmoid(p['ml_o'])

    return finish(h_x, px), (finish(h_c, pc) if need_ctx else None)


def merge_branches(branches, gate_pre, w_branch, w_out):
    gates = jax.nn.sigmoid(gate_pre)
    y = 0.0
    for j, h in enumerate(branches):
        y = y + gates[..., j * D_MODEL:(j + 1) * D_MODEL] * (h @ w_branch[j])
    return y @ w_out


def swiglu(h, w1, w3, w2):
    return (jax.nn.silu(h @ w1) * (h @ w3)) @ w2


def moe(h, router_w, router_bias, w1, w3, w2, s1, s3, s2):
    scores = jax.nn.sigmoid((h @ router_w).astype(jnp.float32))
    _, idx = lax.top_k(scores + router_bias.astype(jnp.float32), TOP_K)
    sel = jnp.take_along_axis(scores, idx, axis=-1)
    wts = sel / jnp.sum(sel, axis=-1, keepdims=True) * ROUTED_SCALE
    gates = jnp.einsum('nk,nke->ne', wts,
                       jax.nn.one_hot(idx, N_EXPERTS, dtype=jnp.float32)).astype(h.dtype)
    y = swiglu(h, s1, s3, s2)
    for e in range(N_EXPERTS):
        y = y + gates[:, e:e + 1] * swiglu(h, w1[e], w3[e], w2[e])
    return y


def setup_inputs(seed: int = 0) -> dict:
    key = jax.random.key(seed)
    ks = jax.random.split(key, 28)
    L, D = DEPTH, D_MODEL

    def nrm(k, shape, scale):
        return jax.random.normal(k, shape, jnp.float32) * scale

    def gain(k, shape):
        return 1.0 + 0.02 * jax.random.normal(k, shape, jnp.float32)

    f_lin = jnp.linspace(3.0, 6.0, ML_HEADS, dtype=jnp.float32)
    return {
        'x': nrm(ks[0], (BATCH, SEQ, D), 1.0),
        'c': nrm(ks[1], (BATCH, D), 1.0),
        'ctx': nrm(ks[2], (BATCH, CTX_LEN, D), 1.0),
        'c_ctx': nrm(ks[3], (D,), 1.0),
        'norm1': gain(ks[4], (L, D)),
        'norm2': gain(ks[5], (L, D)),
        'w_ada': nrm(ks[6], (L, D, 6 * D), 0.5 * D ** -0.5),
        'b_ada': nrm(ks[7], (L, 6 * D), 0.02),
        'w_in': nrm(ks[8], (L, D, D_IN), D ** -0.5),
        'hg_lb_logits': nrm(ks[9], (2, L, HG_W), 0.5),
        'hg_norm': gain(ks[10], (L, HG_DV)),
        'da_q_norm': gain(ks[11], (L, DA_QK)),
        'da_k_norm': gain(ks[12], (L, DA_QK)),
        'da_lambda': nrm(ks[13], (L, 4, DA_QK), 0.1),
        'da_sub_norm': gain(ks[14], (L, DA_V)),
        'ml_igate_bias': nrm(ks[15], (L, 2, ML_HEADS), 0.1),
        'ml_fgate_bias': f_lin + nrm(ks[16], (L, 2, ML_HEADS), 0.1),
        'ml_norm': gain(ks[17], (L, ML_DV)),
        'w_branch': nrm(ks[18], (L, N_BRANCH, BRANCH_W, D), BRANCH_W ** -0.5),
        'w_out': nrm(ks[19], (L, D, D), D ** -0.5),
        'router_w': nrm(ks[20], (L, D, N_EXPERTS), D ** -0.5),
        'router_bias': nrm(ks[21], (L, N_EXPERTS), 0.01),
        'exp_w1': nrm(ks[22], (L, N_EXPERTS, D, D_EXPERT), D ** -0.5),
        'exp_w3': nrm(ks[23], (L, N_EXPERTS, D, D_EXPERT), D ** -0.5),
        'exp_w2': nrm(ks[24], (L, N_EXPERTS, D_EXPERT, D), D_EXPERT ** -0.5),
        'sh_w1': nrm(ks[25], (L, D, D_SHARED), D ** -0.5),
        'sh_w3': nrm(ks[26], (L, D, D_SHARED), D ** -0.5),
        'sh_w2': nrm(ks[27], (L, D_SHARED, D), D_SHARED ** -0.5),
    }


def reference(x, c, ctx, c_ctx, norm1, norm2, w_ada, b_ada, w_in, hg_lb_logits, hg_norm,
              da_q_norm, da_k_norm, da_lambda, da_sub_norm, ml_igate_bias, ml_fgate_bias,
              ml_norm, w_branch, w_out, router_w, router_bias, exp_w1, exp_w3, exp_w2,
              sh_w1, sh_w3, sh_w2):
    b, t, d = x.shape
    n_ctx = ctx.shape[1]
    rows = t // GRID_W
    cos, sin = axial_rope_tables(rows, DA_QK)
    p_lb = jax.nn.softmax(hg_lb_logits.astype(jnp.float32), axis=1)
    lower_bounds = jnp.cumsum(p_lb, axis=1) - p_lb[:, :1]
    silu_c = jax.nn.silu(c)[:, None, :]
    silu_cc = jax.nn.silu(c_ctx)[None, None, :]
    for l in range(DEPTH):
        need_ctx = l < DEPTH - 1
        lam_init = 0.8 - 0.6 * math.exp(-0.3 * l)
        mod_x = jnp.split(silu_c @ w_ada[l] + b_ada[l], 6, axis=-1)
        mod_c = jnp.split(silu_cc @ w_ada[l] + b_ada[l], 6, axis=-1)
        hx = rmsnorm(x, norm1[l]) * (1.0 + mod_x[1]) + mod_x[0]
        hc = rmsnorm(ctx, norm1[l]) * (1.0 + mod_c[1]) + mod_c[0]
        px = split_cols(hx @ w_in[l])
        pc = split_cols(hc @ w_in[l])
        a_x, a_c = hgrn2_mixer(px, pc, lower_bounds[:, l], hg_norm[l], need_ctx)
        d_x, d_c = diff_attention(px, pc, cos, sin, da_q_norm[l], da_k_norm[l], da_lambda[l],
                                  da_sub_norm[l], lam_init, need_ctx)
        m_x, m_c = mlstm_mixer(px, pc, ml_igate_bias[l], ml_fgate_bias[l], ml_norm[l], need_ctx)
        x = x + mod_x[2] * merge_branches((a_x, d_x, m_x), px['gates'], w_branch[l], w_out[l])
        tokens = (rmsnorm(x, norm2[l]) * (1.0 + mod_x[4]) + mod_x[3]).reshape(b * t, d)
        if need_ctx:
            ctx = ctx + mod_c[2] * merge_branches((a_c, d_c, m_c), pc['gates'], w_branch[l], w_out[l])
            tok_c = (rmsnorm(ctx, norm2[l]) * (1.0 + mod_c[4]) + mod_c[3]).reshape(b * n_ctx, d)
            tokens = jnp.concatenate([tokens, tok_c], axis=0)
        y = moe(tokens, router_w[l], router_bias[l], exp_w1[l], exp_w3[l], exp_w2[l],
                sh_w1[l], sh_w3[l], sh_w2[l])
        x = x + mod_x[5] * y[:b * t].reshape(b, t, d)
        if need_ctx:
            ctx = ctx + mod_c[5] * y[b * t:].reshape(b, n_ctx, d)
    return x
```

```python
import functools
import math

import jax
import jax.numpy as jnp
from jax import lax
from jax.experimental import pallas as pl
from jax.experimental.pallas import tpu as pltpu

F32 = jnp.float32
BF16 = jnp.bfloat16

EPS = 1e-6
NEG_BIG = -1e30
TINY = 1e-30
GRID_W = 64
ROPE_THETA = 10000.0
ROUTED_SCALE = 2.5
TOP_K = 4

HEADS = 12
HEAD_W = 128
SEG = HEADS * HEAD_W
DA_QK = 64
N_GATE_COLS = 4 * HEADS

S_HG_Q, S_HG_FF, S_HG_FB, S_HG_I, S_HG_G = 0, 1, 2, 3, 4
S_DA_Q, S_DA_K, S_DA_V = 5, 6, 7
S_ML_Q, S_ML_K, S_ML_V, S_ML_O = 8, 9, 10, 11
N_SEG = 12

LANES = 128
ROW_BLK = 256
HG_CHUNK = 64
HG_SUB = 16
VMEM_LIMIT = 56 * 1024 * 1024


def _cp(sem, vmem=VMEM_LIMIT):
    return pltpu.CompilerParams(dimension_semantics=sem, vmem_limit_bytes=vmem)


def _pick(n, cands):
    for c in cands:
        if n % c == 0:
            return c
    raise ValueError(f"no tile for {n} in {cands}")


def _dot(a, b):
    return jnp.dot(a, b, preferred_element_type=F32)


def _dot_nt(a, b):
    return lax.dot_general(a, b, (((1,), (1,)), ((), ())), preferred_element_type=F32)


def _dot_tn(a, b):
    return lax.dot_general(a, b, (((0,), (0,)), ((), ())), preferred_element_type=F32)


def _split(x):
    hi = x.astype(BF16)
    lo = (x - hi.astype(F32)).astype(BF16)
    return hi, lo


def _sigmoid(x):
    return 1.0 / (1.0 + jnp.exp(-x))


class _Rows:
    def __init__(self, batch, seq, ctx_len):
        self.batch, self.seq, self.ctx = batch, seq, ctx_len
        assert seq % ROW_BLK == 0 and ctx_len % ROW_BLK == 0
        self.nlt = seq // ROW_BLK
        self.nct = ctx_len // ROW_BLK
        self.n_lat = batch * seq
        self.n_all = batch * (seq + ctx_len)
        self.n_groups = batch + 1

    def group(self, i, tm):
        lat_tiles = self.n_lat // tm
        return jnp.where(i < lat_tiles, 1 + i // (self.seq // tm), 0)

    def mod_row(self, layer, i, tm, which):
        return (layer * self.n_groups + self.group(i, tm)) * 6 + which

    def scan_block(self, b, s, reverse):
        if reverse:
            cblk = self.batch * self.nlt + b * self.nct + (self.nct - 1 - s)
            lblk = b * self.nlt + (self.nlt - 1 - (s - self.nct))
        else:
            cblk = self.batch * self.nlt + b * self.nct + s
            lblk = b * self.nlt + (s - self.nct)
        return jnp.where(s < self.nct, cblk, lblk)


def _mods_kernel(c_ref, w_ref, b_ref, o_ref):
    c = c_ref[...]
    s_hi, s_lo = _split(c * _sigmoid(c))
    w_hi, w_lo = _split(w_ref[...])
    acc = _dot(s_hi, w_hi) + _dot(s_lo, w_hi) + _dot(s_hi, w_lo)
    o_ref[...] = acc + b_ref[...]


def _mods(cvec, w_ada, b_ada):
    depth, d, n = w_ada.shape
    tn = _pick(n, (512, 256, 128))
    rows = cvec.shape[0]
    return pl.pallas_call(
        _mods_kernel,
        grid=(depth, n // tn),
        in_specs=[pl.BlockSpec((rows, d), lambda l, j: (0, 0)),
                  pl.BlockSpec((None, d, tn), lambda l, j: (l, 0, j)),
                  pl.BlockSpec((None, 1, tn), lambda l, j: (l, 0, j))],
        out_specs=pl.BlockSpec((None, rows, tn), lambda l, j: (l, 0, j)),
        out_shape=jax.ShapeDtypeStruct((depth, rows, n), F32),
        compiler_params=_cp(("arbitrary", "arbitrary")),
        name="adaln_mods",
    )(cvec, w_ada, b_ada.reshape(depth, 1, n))


def _norm_mod(x_ref, g_ref, sc_ref, sh_ref):
    x = x_ref[...]
    ms = jnp.mean(x * x, axis=-1, keepdims=True)
    y = x * lax.rsqrt(ms + EPS) * g_ref[...]
    return y * (1.0 + sc_ref[0]) + sh_ref[0]


def _norm_kernel(x_ref, g_ref, sc_ref, sh_ref, o_ref):
    o_ref[...] = _norm_mod(x_ref, g_ref, sc_ref, sh_ref).astype(BF16)


def _norm_router_kernel(n_exp, x_ref, g_ref, sc_ref, sh_ref, rwh_ref, rwl_ref, rb_ref,
                        o_ref, gate_ref):
    y = _norm_mod(x_ref, g_ref, sc_ref, sh_ref)
    o_ref[...] = y.astype(BF16)
    y_hi, y_lo = _split(y)
    rwh = rwh_ref[...]
    logits = _dot(y_hi, rwh) + _dot(y_lo, rwh) + _dot(y_hi, rwl_ref[...])
    scores = _sigmoid(logits)
    lane = lax.broadcasted_iota(jnp.int32, scores.shape, 1).astype(F32)
    work = jnp.where(lane < n_exp, scores + rb_ref[...], -jnp.inf)
    sel = jnp.zeros_like(scores)
    for _ in range(TOP_K):
        mx = jnp.max(work, axis=-1, keepdims=True)
        first = jnp.min(jnp.where(work == mx, lane, float(LANES)), axis=-1, keepdims=True)
        hit = lane == first
        sel = jnp.where(hit, scores, sel)
        work = jnp.where(hit, -jnp.inf, work)
    gate_ref[...] = sel / jnp.sum(sel, axis=-1, keepdims=True) * ROUTED_SCALE


def _norm(rows, x_all, n_all, gain, modsflat, layer, which_scale, which_shift, router=None):
    d = x_all.shape[1]
    tm = ROW_BLK

    def mod_idx(which):
        return lambda i: (rows.mod_row(layer, i, tm, which), 0, 0)

    in_specs = [pl.BlockSpec((tm, d), lambda i: (i, 0)),
                pl.BlockSpec((1, d), lambda i: (0, 0)),
                pl.BlockSpec((1, 1, d), mod_idx(which_scale)),
                pl.BlockSpec((1, 1, d), mod_idx(which_shift))]
    args = [x_all, gain.reshape(1, d), modsflat, modsflat]
    out_spec = pl.BlockSpec((tm, d), lambda i: (i, 0))
    out_shape = jax.ShapeDtypeStruct((n_all, d), BF16)
    if router is None:
        return pl.pallas_call(
            _norm_kernel, grid=(n_all // tm,), in_specs=in_specs, out_specs=out_spec,
            out_shape=out_shape, compiler_params=_cp(("arbitrary",)), name="prenorm",
        )(*args)
    rw_hi, rw_lo, rbias, n_exp = router
    in_specs += [pl.BlockSpec((d, LANES), lambda i: (0, 0)),
                 pl.BlockSpec((d, LANES), lambda i: (0, 0)),
                 pl.BlockSpec((1, LANES), lambda i: (0, 0))]
    return pl.pallas_call(
        functools.partial(_norm_router_kernel, n_exp),
        grid=(n_all // tm,), in_specs=in_specs,
        out_specs=[out_spec, pl.BlockSpec((tm, LANES), lambda i: (i, 0))],
        out_shape=[out_shape, jax.ShapeDtypeStruct((n_all, LANES), F32)],
        compiler_params=_cp(("arbitrary",)), name="prenorm_router",
    )(*args, rw_hi, rw_lo, rbias)


def _mm_kernel(a_ref, w_ref, o_ref):
    o_ref[...] = _dot(a_ref[...], w_ref[...]).astype(o_ref.dtype)


def _matmul(a, w, out_dtype, tm=512):
    m, k = a.shape
    n = w.shape[1]
    tn = _pick(n, (1024, 512, 256, 128))
    return pl.pallas_call(
        _mm_kernel,
        grid=(n // tn, m // tm),
        in_specs=[pl.BlockSpec((tm, k), lambda j, i: (i, 0)),
                  pl.BlockSpec((k, tn), lambda j, i: (0, j))],
        out_specs=pl.BlockSpec((tm, tn), lambda j, i: (i, j)),
        out_shape=jax.ShapeDtypeStruct((m, n), out_dtype),
        compiler_params=_cp(("arbitrary", "arbitrary")),
        name="in_proj",
    )(a, w)


def _hgrn_chunk(q, k, lf, v, st, reverse):
    c = HG_CHUNK
    r_i = lax.broadcasted_iota(jnp.int32, (c, c), 0)
    c_i = lax.broadcasted_iota(jnp.int32, (c, c), 1)
    tri = (c_i >= r_i) if reverse else (c_i <= r_i)
    tri = jnp.where(tri, 1.0, 0.0).astype(BF16)
    lf_hi, lf_lo = _split(lf)
    bcum = _dot(tri, lf_hi) + _dot(tri, lf_lo)
    last = 0 if reverse else c - 1
    b_last = bcum[last:last + 1]
    o_inter = _dot_nt((q * jnp.exp(bcum)).astype(BF16), st.astype(BF16))
    kd = k * jnp.exp(b_last - bcum)
    st_new = st * jnp.exp(b_last) + _dot_tn(v.astype(BF16), kd.astype(BF16))

    ones = jnp.ones((LANES, LANES), BF16)
    sub_row = lax.broadcasted_iota(jnp.int32, (HG_SUB, LANES), 0)
    v16 = v.astype(BF16)
    outs = []
    for i in range(c // HG_SUB):
        r0 = i * HG_SUB
        bsub = bcum[r0:r0 + HG_SUB]
        qsub = q[r0:r0 + HG_SUB]
        zs = []
        for s in range(HG_SUB):
            keep = (sub_row <= s) if reverse else (sub_row >= s)
            dl = jnp.where(keep, bsub - bcum[r0 + s:r0 + s + 1], NEG_BIG)
            zs.append(qsub * (k[r0 + s:r0 + s + 1] * jnp.exp(dl)))
        red = _dot(jnp.concatenate(zs, axis=0).astype(BF16), ones)
        o_sub = red[0:HG_SUB] * v[r0:r0 + 1]
        for s in range(1, HG_SUB):
            o_sub = o_sub + red[s * HG_SUB:(s + 1) * HG_SUB] * v[r0 + s:r0 + s + 1]
        lo, hi = (r0 + HG_SUB, c) if reverse else (0, r0)
        if hi > lo:
            ref_row = bcum[lo:lo + 1] if reverse else bcum[hi - 1:hi]
            qi = (qsub * jnp.exp(bsub - ref_row)).astype(BF16)
            kt = (k[lo:hi] * jnp.exp(ref_row - bcum[lo:hi])).astype(BF16)
            o_sub = o_sub + _dot(_dot_nt(qi, kt).astype(BF16), v16[lo:hi])
        outs.append(o_sub)
    return o_inter + jnp.concatenate(outs, axis=0), st_new


def _hgrn_kernel(layer, reverse, q_ref, f_ref, v_ref, lb_ref, o_ref, st_ref):
    @pl.when(pl.program_id(2) == 0)
    def _():
        st_ref[...] = jnp.zeros_like(st_ref)

    lg = lb_ref[0]
    e = jnp.exp(lg - jnp.max(lg, axis=0, keepdims=True))
    p = e / jnp.sum(e, axis=0, keepdims=True)
    lb = jnp.zeros((1, LANES), F32)
    for i in range(1, layer + 1):
        lb = lb + p[i:i + 1]

    n_chunk = ROW_BLK // HG_CHUNK
    st = st_ref[...]
    order = range(n_chunk - 1, -1, -1) if reverse else range(n_chunk)
    for ci in order:
        rows = pl.ds(ci * HG_CHUNK, HG_CHUNK)
        qp = q_ref[rows, :].astype(F32)
        fp = f_ref[rows, :].astype(F32)
        v = v_ref[rows, :].astype(F32)
        q = qp * _sigmoid(qp) * HEAD_W ** -0.5
        f = lb + (1.0 - lb) * _sigmoid(fp)
        lf = jnp.log(jnp.maximum(f, TINY))
        k = (1.0 - lb) * _sigmoid(-fp)
        o, st = _hgrn_chunk(q, k, lf, v, st, reverse)
        o_ref[rows, :] = o.astype(o_ref.dtype)
    st_ref[...] = st


def _hgrn_scan(rows, p_main, lb_logits, layer, direction):
    reverse = direction == 1
    n_all = p_main.shape[0]
    steps = rows.nct + rows.nlt
    depth = lb_logits.shape[1]

    def spec(seg):
        return pl.BlockSpec((ROW_BLK, HEAD_W),
                            lambda b, h, s: (rows.scan_block(b, s, reverse), seg * HEADS + h))

    f_seg = S_HG_FB if reverse else S_HG_FF
    return pl.pallas_call(
        functools.partial(_hgrn_kernel, layer, reverse),
        grid=(rows.batch, HEADS, steps),
        in_specs=[spec(S_HG_Q), spec(f_seg), spec(S_HG_I),
                  pl.BlockSpec((1, depth, HEAD_W), lambda b, h, s: (direction, 0, h))],
        out_specs=pl.BlockSpec((ROW_BLK, HEAD_W),
                               lambda b, h, s: (rows.scan_block(b, s, reverse), h)),
        out_shape=jax.ShapeDtypeStruct((n_all, SEG), BF16),
        scratch_shapes=[pltpu.VMEM((HEAD_W, HEAD_W), F32)],
        compiler_params=_cp(("arbitrary", "arbitrary", "arbitrary")),
        name="hgrn2_scan",
    )(p_main, p_main, p_main, lb_logits)


def _mlstm_kernel(reverse, q_ref, k_ref, v_ref, ig_ref, fg_ref, ib_ref, fb_ref, o_ref,
                  s_ref, m_ref):
    c = ROW_BLK

    @pl.when(pl.program_id(2) == 0)
    def _():
        s_ref[...] = jnp.zeros_like(s_ref)
        m_ref[...] = jnp.zeros_like(m_ref)

    q = q_ref[...]
    kt = (k_ref[...].astype(F32) * HEAD_W ** -0.5).T
    v_cat = jnp.concatenate([v_ref[...], jnp.ones((c, HEAD_W), BF16)], axis=1)
    ig = ig_ref[0] + ib_ref[0][:, :1]
    fx = fg_ref[0] + fb_ref[0][:, :1]
    lf = jnp.minimum(fx, 0.0) - jnp.log1p(jnp.exp(-jnp.abs(fx)))

    r_i = lax.broadcasted_iota(jnp.int32, (c, c), 0)
    c_i = lax.broadcasted_iota(jnp.int32, (c, c), 1)
    cum = jnp.where((r_i >= c_i) if reverse else (r_i <= c_i), 1.0, 0.0).astype(BF16)
    lf8 = jnp.broadcast_to(lf, (8, c))
    lf_hi, lf_lo = _split(lf8)
    brow = (_dot(lf_hi, cum) + _dot(lf_lo, cum))[0:1]
    b_rows = jnp.broadcast_to(brow, (c, c))
    b_cols = b_rows.T
    keep = (c_i >= r_i) if reverse else (c_i <= r_i)
    dlog = jnp.where(keep, b_cols - b_rows + ig, NEG_BIG)
    m_prev = m_ref[:, :1]
    inter = b_cols[:, :1] + m_prev
    m_t = jnp.maximum(jnp.max(dlog, axis=-1, keepdims=True), inter)
    w = jnp.exp(dlog - m_t) * _dot(q, kt.astype(BF16))
    w_inter = jnp.exp(inter - m_t)
    state = s_ref[...]
    nd = _dot(w.astype(BF16), v_cat) + w_inter * _dot(q, state.astype(BF16))
    num, den = nd[:, :HEAD_W], nd[:, HEAD_W:]
    o_ref[...] = (num / jnp.maximum(jnp.abs(den), jnp.exp(-m_t))).astype(o_ref.dtype)

    last = 0 if reverse else c - 1
    m_new = m_t[last:last + 1]
    b_last = brow[:, last:last + 1]
    wk = jnp.exp(b_last - brow + ig - m_new)
    dec = jnp.exp(b_last + m_prev - m_new)
    s_ref[...] = dec * state + _dot((kt * wk).astype(BF16), v_cat)
    m_ref[...] = jnp.broadcast_to(m_new, m_ref.shape)


def _mlstm_scan(rows, p_main, gates_t, i_bias, f_bias, direction):
    reverse = direction == 1
    n_all = p_main.shape[0]
    steps = rows.nct + rows.nlt

    def spec(seg):
        return pl.BlockSpec((ROW_BLK, HEAD_W),
                            lambda b, h, s: (rows.scan_block(b, s, reverse), seg * HEADS + h))

    def gate_spec(which):
        return pl.BlockSpec((1, 1, ROW_BLK),
                            lambda b, h, s: (which * HEADS + h, 0, rows.scan_block(b, s, reverse)))

    def bias_spec():
        return pl.BlockSpec((1, 1, LANES), lambda b, h, s: (direction * HEADS + h, 0, 0))

    return pl.pallas_call(
        functools.partial(_mlstm_kernel, reverse),
        grid=(rows.batch, HEADS, steps),
        in_specs=[spec(S_ML_Q), spec(S_ML_K), spec(S_ML_V),
                  gate_spec(direction), gate_spec(2 + direction), bias_spec(), bias_spec()],
        out_specs=pl.BlockSpec((ROW_BLK, HEAD_W),
                               lambda b, h, s: (rows.scan_block(b, s, reverse), h)),
        out_shape=jax.ShapeDtypeStruct((n_all, SEG), BF16),
        scratch_shapes=[pltpu.VMEM((HEAD_W, 2 * HEAD_W), F32), pltpu.VMEM((1, LANES), F32)],
        compiler_params=_cp(("arbitrary", "arbitrary", "arbitrary")),
        name="mlstm_scan",
    )(p_main, p_main, p_main, gates_t, gates_t, i_bias, f_bias)


def _da_prep_kernel(q_ref, k_ref, cos_ref, sin_ref, qg_ref, kg_ref, qo_ref, ko_ref):
    r_i = lax.broadcasted_iota(jnp.int32, (LANES, LANES), 0)
    c_i = lax.broadcasted_iota(jnp.int32, (LANES, LANES), 1)
    blockdiag = jnp.where((r_i // DA_QK) == (c_i // DA_QK), 1.0 / DA_QK, 0.0).astype(BF16)
    cos, sin = cos_ref[...], sin_ref[...]
    lane = lax.broadcasted_iota(jnp.int32, cos.shape, 1)
    first_half = (lane % DA_QK) < (DA_QK // 2)

    def qk_norm_rope(x, gain):
        x_hi, x_lo = _split(x * x)
        ms = _dot(x_hi, blockdiag) + _dot(x_lo, blockdiag)
        y = x * lax.rsqrt(ms + EPS) * gain
        rot = jnp.where(first_half, -pltpu.roll(y, LANES - DA_QK // 2, 1),
                        pltpu.roll(y, DA_QK // 2, 1))
        return y * cos + rot * sin

    for h in range(HEADS):
        cols = slice(h * HEAD_W, (h + 1) * HEAD_W)
        qo_ref[:, cols] = (qk_norm_rope(q_ref[:, cols].astype(F32), qg_ref[...])
                           * DA_QK ** -0.5).astype(BF16)
        ko_ref[:, cols] = qk_norm_rope(k_ref[:, cols].astype(F32), kg_ref[...]).astype(BF16)


def _da_prep(rows, p_main, cos_tab, sin_tab, q_gain, k_gain):
    n_all = p_main.shape[0]
    tm = ROW_BLK

    def tab_idx(i):
        lat = i % rows.nlt
        ctx = rows.nlt + (i - rows.batch * rows.nlt) % rows.nct
        return (jnp.where(i < rows.batch * rows.nlt, lat, ctx), 0)

    out = jax.ShapeDtypeStruct((n_all, SEG), BF16)
    return pl.pallas_call(
        _da_prep_kernel,
        grid=(n_all // tm,),
        in_specs=[pl.BlockSpec((tm, SEG), lambda i: (i, S_DA_Q)),
                  pl.BlockSpec((tm, SEG), lambda i: (i, S_DA_K)),
                  pl.BlockSpec((tm, LANES), tab_idx),
                  pl.BlockSpec((tm, LANES), tab_idx),
                  pl.BlockSpec((1, LANES), lambda i: (0, 0)),
                  pl.BlockSpec((1, LANES), lambda i: (0, 0))],
        out_specs=[pl.BlockSpec((tm, SEG), lambda i: (i, 0))] * 2,
        out_shape=[out, out],
        compiler_params=_cp(("arbitrary",)),
        name="da_qk_prep",
    )(p_main, p_main, cos_tab, sin_tab, q_gain, k_gain)


def _attn_kernel(lam_init, n_lat_q_blocks, n_lat_keys, tk, q_ref, kc_ref, vc_ref, kl_ref, vl_ref, lam_ref, sn_ref,
                 o_ref, acc_ref, m_ref, l_ref):
    tq = q_ref.shape[0]
    q = q_ref[...]
    lane = lax.broadcasted_iota(jnp.int32, q.shape, 1)
    zero = jnp.zeros_like(q)
    q2 = jnp.concatenate([jnp.where(lane < DA_QK, q, zero), jnp.where(lane >= DA_QK, q, zero)],
                         axis=0)

    def update(k, v, first):
        s = _dot_nt(q2, k)
        m_cur = jnp.max(s, axis=-1, keepdims=True)
        if first:
            m_new = m_cur
        else:
            m_old = m_ref[...]
            m_new = jnp.maximum(m_old, m_cur)
        p = jnp.exp(s - m_new)
        l_cur = jnp.sum(p, axis=-1, keepdims=True)
        pv = _dot(p.astype(BF16), v)
        if first:
            l_ref[...] = l_cur
            acc_ref[...] = pv
        else:
            alpha = jnp.exp(m_old - m_new)
            l_ref[...] = alpha * l_ref[...] + l_cur
            acc_ref[...] = alpha * acc_ref[...] + pv
        m_ref[...] = m_new

    update(kc_ref[...], vc_ref[...], True)

    def body(j, carry):
        ks = pl.ds(pl.multiple_of(j * tk, tk), tk)
        update(kl_ref[ks, :], vl_ref[ks, :], False)
        return carry

    n_chunks = jnp.where(pl.program_id(2) < n_lat_q_blocks, n_lat_keys // tk, 0)
    lax.fori_loop(0, n_chunks, body, 0)

    lv = lam_ref[...]
    lam = (jnp.exp(jnp.sum(lv[0:1] * lv[1:2], axis=-1, keepdims=True))
           - jnp.exp(jnp.sum(lv[2:3] * lv[3:4], axis=-1, keepdims=True)) + lam_init)
    o = acc_ref[...] / l_ref[...]
    a = o[:tq] - lam * o[tq:]
    ms = jnp.mean(a * a, axis=-1, keepdims=True)
    o_ref[...] = (a * lax.rsqrt(ms + EPS) * sn_ref[...] * (1.0 - lam_init)).astype(o_ref.dtype)


def _attention(rows, qh, kh, p_main, lam_vec, sub_gain, lam_init, n_rows):
    tq = ROW_BLK
    assert rows.nct == 1
    ctx_blk = lambda b: rows.batch * rows.nlt + b
    v_col = lambda h: S_DA_V * HEADS + h
    ctx_q = n_rows > rows.n_lat
    q_idx = lambda b, h, i: (jnp.where(i < rows.nlt, b * rows.nlt + i, ctx_blk(b)), h)
    tk = _pick(rows.seq, (512, 256))
    return pl.pallas_call(
        functools.partial(_attn_kernel, lam_init, rows.nlt, rows.seq, tk),
        grid=(rows.batch, HEADS, rows.nlt + (1 if ctx_q else 0)),
        in_specs=[pl.BlockSpec((tq, HEAD_W), q_idx),
                  pl.BlockSpec((ROW_BLK, HEAD_W), lambda b, h, i: (ctx_blk(b), h)),
                  pl.BlockSpec((ROW_BLK, HEAD_W), lambda b, h, i: (ctx_blk(b), v_col(h))),
                  pl.BlockSpec((rows.seq, HEAD_W), lambda b, h, i: (b, h)),
                  pl.BlockSpec((rows.seq, HEAD_W), lambda b, h, i: (b, v_col(h))),
                  pl.BlockSpec(lam_vec.shape, lambda b, h, i: (0, 0)),
                  pl.BlockSpec((1, LANES), lambda b, h, i: (0, 0))],
        out_specs=pl.BlockSpec((tq, HEAD_W), q_idx),
        out_shape=jax.ShapeDtypeStruct((n_rows, SEG), BF16),
        scratch_shapes=[pltpu.VMEM((2 * tq, HEAD_W), F32), pltpu.VMEM((2 * tq, 1), F32),
                        pltpu.VMEM((2 * tq, 1), F32)],
        compiler_params=_cp(("arbitrary", "arbitrary", "arbitrary")),
        name="diff_attention",
    )(qh, kh, p_main, kh, p_main, lam_vec, sub_gain)


def _merge_kernel(ohf_ref, ohb_ref, hg_ref, da_ref, omf_ref, omb_ref, mo_ref, g0_ref, g1_ref,
                  g2_ref, hn_ref, mn_ref, wb_ref, o_ref, h_ref):
    @pl.when(pl.program_id(1) == 0)
    def _():
        for h in range(HEADS):
            cols = slice(h * HEAD_W, (h + 1) * HEAD_W)
            o = ohf_ref[:, cols].astype(F32) + ohb_ref[:, cols].astype(F32)
            y = o * lax.rsqrt(jnp.mean(o * o, axis=-1, keepdims=True) + EPS) * hn_ref[...]
            g = hg_ref[:, cols].astype(F32)
            h_ref[0, :, cols] = (y * (g * _sigmoid(g))).astype(BF16)
            o = omf_ref[:, cols].astype(F32) + omb_ref[:, cols].astype(F32)
            y = o * lax.rsqrt(jnp.mean(o * o, axis=-1, keepdims=True) + EPS) * mn_ref[...]
            h_ref[2, :, cols] = (y * _sigmoid(mo_ref[:, cols].astype(F32))).astype(BF16)
        h_ref[1] = da_ref[...]

    y = _sigmoid(g0_ref[...].astype(F32)) * _dot(h_ref[0], wb_ref[0])
    y = y + _sigmoid(g1_ref[...].astype(F32)) * _dot(h_ref[1], wb_ref[1])
    y = y + _sigmoid(g2_ref[...].astype(F32)) * _dot(h_ref[2], wb_ref[2])
    o_ref[...] = y.astype(o_ref.dtype)


def _merge(p_main, ohf, ohb, da, omf, omb, hg_gain, ml_gain, w_branch, d):
    n_all = da.shape[0]
    tm = 512
    tn = _pick(d, (512, 256, 128))
    gate_col0 = N_SEG * SEG // tn
    row = lambda i, j: (i, 0)

    def gate_spec(jj):
        return pl.BlockSpec((tm, tn), lambda i, j: (i, gate_col0 + jj * (d // tn) + j))

    return pl.pallas_call(
        _merge_kernel,
        grid=(n_all // tm, d // tn),
        in_specs=[pl.BlockSpec((tm, SEG), row), pl.BlockSpec((tm, SEG), row),
                  pl.BlockSpec((tm, SEG), lambda i, j: (i, S_HG_G)),
                  pl.BlockSpec((tm, SEG), row),
                  pl.BlockSpec((tm, SEG), row), pl.BlockSpec((tm, SEG), row),
                  pl.BlockSpec((tm, SEG), lambda i, j: (i, S_ML_O)),
                  gate_spec(0), gate_spec(1), gate_spec(2),
                  pl.BlockSpec((1, LANES), lambda i, j: (0, 0)),
                  pl.BlockSpec((1, LANES), lambda i, j: (0, 0)),
                  pl.BlockSpec((3, SEG, tn), lambda i, j: (0, 0, j))],
        out_specs=pl.BlockSpec((tm, tn), lambda i, j: (i, j)),
        out_shape=jax.ShapeDtypeStruct((n_all, d), BF16),
        scratch_shapes=[pltpu.VMEM((3, tm, SEG), BF16)],
        compiler_params=_cp(("arbitrary", "arbitrary")),
        name="branch_merge",
    )(ohf, ohb, p_main, da, omf, omb, p_main, p_main, p_main, p_main, hg_gain, ml_gain, w_branch)


def _proj_resid_kernel(a_ref, w_ref, x_ref, g_ref, o_ref):
    o_ref[...] = x_ref[...] + g_ref[0] * _dot(a_ref[...], w_ref[...])


def _proj_resid(rows, a, w, x_all, modsflat, layer, which_gate):
    d = x_all.shape[1]
    n_all, k = a.shape
    tm = 512
    tn = _pick(d, (1024, 512, 256, 128))
    return pl.pallas_call(
        _proj_resid_kernel,
        grid=(d // tn, n_all // tm),
        in_specs=[pl.BlockSpec((tm, k), lambda j, i: (i, 0)),
                  pl.BlockSpec((k, tn), lambda j, i: (0, j)),
                  pl.BlockSpec((tm, tn), lambda j, i: (i, j)),
                  pl.BlockSpec((1, 1, tn),
                               lambda j, i: (rows.mod_row(layer, i, tm, which_gate), 0, j))],
        out_specs=pl.BlockSpec((tm, tn), lambda j, i: (i, j)),
        out_shape=jax.ShapeDtypeStruct((n_all, d), F32),
        compiler_params=_cp(("arbitrary", "arbitrary")),
        name="out_proj_residual",
    )(a, w, x_all, modsflat)


def _moe_kernel(n_exp, tok_ref, gate_ref, w1_ref, w3_ref, w2_ref, o_ref, acc_ref):
    e = pl.program_id(1)

    @pl.when(e == 0)
    def _():
        acc_ref[...] = jnp.zeros_like(acc_ref)

    tok = tok_ref[...]
    gates = gate_ref[...]
    lane = lax.broadcasted_iota(jnp.int32, gates.shape, 1)
    g = jnp.sum(jnp.where(lane == e, gates, 0.0), axis=-1, keepdims=True)
    g = jnp.where(e == n_exp, 1.0, g)
    h1 = _dot(tok, w1_ref[...])
    h3 = _dot(tok, w3_ref[...])
    a = (h1 * _sigmoid(h1)) * h3 * g
    acc_ref[...] += _dot(a.astype(BF16), w2_ref[...])

    @pl.when(e == n_exp)
    def _():
        o_ref[...] = acc_ref[...].astype(o_ref.dtype)


def _moe(tok, gates, w1, w3, w2):
    n_all, d = tok.shape
    n_slots, _, de = w1.shape
    tm = 512
    return pl.pallas_call(
        functools.partial(_moe_kernel, n_slots - 1),
        grid=(n_all // tm, n_slots),
        in_specs=[pl.BlockSpec((tm, d), lambda i, e: (i, 0)),
                  pl.BlockSpec((tm, LANES), lambda i, e: (i, 0)),
                  pl.BlockSpec((None, d, de), lambda i, e: (e, 0, 0)),
                  pl.BlockSpec((None, d, de), lambda i, e: (e, 0, 0)),
                  pl.BlockSpec((None, de, d), lambda i, e: (e, 0, 0))],
        out_specs=pl.BlockSpec((tm, d), lambda i, e: (i, 0)),
        out_shape=jax.ShapeDtypeStruct((n_all, d), BF16),
        scratch_shapes=[pltpu.VMEM((tm, d), F32)],
        compiler_params=_cp(("arbitrary", "arbitrary")),
        name="moe_experts",
    )(tok, gates, w1, w3, w2)


def _resid_kernel(x_ref, y_ref, g_ref, o_ref):
    o_ref[...] = x_ref[...] + g_ref[0] * y_ref[...].astype(F32)


def _resid(rows, x_all, y, modsflat, layer, which_gate, n_rows):
    d = x_all.shape[1]
    tm = ROW_BLK
    blk = pl.BlockSpec((tm, d), lambda i: (i, 0))
    return pl.pallas_call(
        _resid_kernel,
        grid=(n_rows // tm,),
        in_specs=[blk, blk,
                  pl.BlockSpec((1, 1, d), lambda i: (rows.mod_row(layer, i, tm, which_gate), 0, 0))],
        out_specs=blk,
        out_shape=jax.ShapeDtypeStruct((n_rows, d), F32),
        compiler_params=_cp(("arbitrary",)),
        name="moe_residual",
    )(x_all, y, modsflat)


def _rope_tables(seq, ctx_len):
    n = DA_QK // 4
    t = jnp.arange(seq)
    inv = ROPE_THETA ** (-jnp.arange(n, dtype=F32) / n)
    row = (t // GRID_W).astype(F32)
    col = (t % GRID_W).astype(F32)
    ang = jnp.concatenate([row[:, None] * inv, col[:, None] * inv], axis=-1)
    ang = jnp.concatenate([ang, jnp.zeros((ctx_len, 2 * n), F32)], axis=0)
    return jnp.tile(jnp.cos(ang), (1, 4)), jnp.tile(jnp.sin(ang), (1, 4))


def kernel(x, c, ctx, c_ctx, norm1, norm2, w_ada, b_ada, w_in, hg_lb_logits, hg_norm, da_q_norm,
           da_k_norm, da_lambda, da_sub_norm, ml_igate_bias, ml_fgate_bias, ml_norm, w_branch,
           w_out, router_w, router_bias, exp_w1, exp_w3, exp_w2, sh_w1, sh_w3, sh_w2):
    batch, seq, d = x.shape
    ctx_len = ctx.shape[1]
    depth = w_ada.shape[0]
    n_exp = router_w.shape[-1]
    rows = _Rows(batch, seq, ctx_len)
    assert batch + 1 <= 16 and n_exp <= LANES

    x_all = jnp.concatenate([x.reshape(batch * seq, d), ctx.reshape(batch * ctx_len, d)], axis=0)
    cvec = jnp.zeros((16, d), F32).at[0].set(c_ctx).at[1:1 + batch].set(c)
    mods = _mods(cvec, w_ada, b_ada)
    modsflat = mods[:, :rows.n_groups].reshape(depth * rows.n_groups * 6, 1, d)
    cos_tab, sin_tab = _rope_tables(seq, ctx_len)
    gate_lo = N_SEG * SEG

    for l in range(depth):
        n_rows = rows.n_all if l < depth - 1 else rows.n_lat
        lam_init = 0.8 - 0.6 * math.exp(-0.3 * l)
        w_main = jnp.concatenate([w_in[l, :, :gate_lo], w_in[l, :, gate_lo + N_GATE_COLS:]],
                                 axis=1).astype(BF16)
        w_gate = jnp.pad(w_in[l, :, gate_lo:gate_lo + N_GATE_COLS],
                         ((0, 0), (0, LANES - N_GATE_COLS))).astype(BF16)

        hx = _norm(rows, x_all, rows.n_all, norm1[l], modsflat, l, 1, 0)
        p_main = _matmul(hx, w_main, BF16)
        p_gate = _matmul(hx, w_gate, F32)
        gates_t = p_gate[:, :N_GATE_COLS].T.reshape(N_GATE_COLS, 1, rows.n_all)

        ohf = _hgrn_scan(rows, p_main, hg_lb_logits, l, 0)
        ohb = _hgrn_scan(rows, p_main, hg_lb_logits, l, 1)

        tile2 = lambda g: jnp.tile(g.reshape(1, DA_QK), (1, 2))
        qh, kh = _da_prep(rows, p_main, cos_tab, sin_tab, tile2(da_q_norm[l]), tile2(da_k_norm[l]))
        da = _attention(rows, qh, kh, p_main, da_lambda[l], da_sub_norm[l].reshape(1, HEAD_W),
                        lam_init, n_rows)

        bias = lambda bv: jnp.broadcast_to(bv.reshape(2 * HEADS, 1, 1), (2 * HEADS, 1, LANES))
        i_b, f_b = bias(ml_igate_bias[l]), bias(ml_fgate_bias[l])
        omf = _mlstm_scan(rows, p_main, gates_t, i_b, f_b, 0)
        omb = _mlstm_scan(rows, p_main, gates_t, i_b, f_b, 1)

        ymid = _merge(p_main, ohf, ohb, da, omf, omb, hg_norm[l].reshape(1, HEAD_W),
                      ml_norm[l].reshape(1, HEAD_W), w_branch[l].astype(BF16), d)
        x_all = _proj_resid(rows, ymid, w_out[l].astype(BF16), x_all, modsflat, l, 2)

        rw = jnp.pad(router_w[l], ((0, 0), (0, LANES - n_exp)))
        rw_hi = rw.astype(BF16)
        rw_lo = (rw - rw_hi.astype(F32)).astype(BF16)
        rb = jnp.pad(router_bias[l], (0, LANES - n_exp)).reshape(1, LANES)
        tok, gates = _norm(rows, x_all, n_rows, norm2[l], modsflat, l, 4, 3,
                           router=(rw_hi, rw_lo, rb, n_exp))
        w1 = jnp.concatenate([exp_w1[l], sh_w1[l][None]], axis=0).astype(BF16)
        w3 = jnp.concatenate([exp_w3[l], sh_w3[l][None]], axis=0).astype(BF16)
        w2 = jnp.concatenate([exp_w2[l], sh_w2[l][None]], axis=0).astype(BF16)
        y = _moe(tok, gates, w1, w3, w2)
        x_all = _resid(rows, x_all, y, modsflat, l, 5, n_rows)

    return x_all.reshape(batch, seq, d)
```

```python
import functools
import math

import jax
import jax.numpy as jnp
from jax import lax
from jax.experimental import pallas as pl
from jax.experimental.pallas import tpu as pltpu

F32 = jnp.float32
BF16 = jnp.bfloat16

EPS = 1e-6
NEG_BIG = -1e30
TINY = 1e-30
GRID_W = 64
ROPE_THETA = 10000.0
ROUTED_SCALE = 2.5
TOP_K = 4
LOG2E = 1.4426950408889634

HEADS = 12
HEAD_W = 128
SEG = HEADS * HEAD_W
DA_QK = 64
N_GATE_COLS = 4 * HEADS

S_HG_Q, S_HG_FF, S_HG_FB, S_HG_I, S_HG_G = 0, 1, 2, 3, 4
S_DA_Q, S_DA_K, S_DA_V = 5, 6, 7
S_ML_Q, S_ML_K, S_ML_V, S_ML_O = 8, 9, 10, 11
N_SEG = 12

LANES = 128
ROW_BLK = 256
HG_CHUNK = 64
HG_SUB = 16
VMEM_LIMIT = 56 * 1024 * 1024


def _cp(sem, vmem=VMEM_LIMIT):
    return pltpu.CompilerParams(dimension_semantics=sem, vmem_limit_bytes=vmem)


def _pick(n, cands):
    for c in cands:
        if n % c == 0:
            return c
    raise ValueError(f"no tile for {n} in {cands}")


def _dot(a, b):
    return jnp.dot(a, b, preferred_element_type=F32)


def _dot_nt(a, b):
    return lax.dot_general(a, b, (((1,), (1,)), ((), ())), preferred_element_type=F32)


def _dot_tn(a, b):
    return lax.dot_general(a, b, (((0,), (0,)), ((), ())), preferred_element_type=F32)


def _split(x):
    hi = x.astype(BF16)
    lo = (x - hi.astype(F32)).astype(BF16)
    return hi, lo


def _sigmoid(x):
    return 1.0 / (1.0 + jnp.exp(-x))


class _Rows:
    def __init__(self, batch, seq, ctx_len):
        self.batch, self.seq, self.ctx = batch, seq, ctx_len
        assert seq % ROW_BLK == 0 and ctx_len % ROW_BLK == 0
        self.nlt = seq // ROW_BLK
        self.nct = ctx_len // ROW_BLK
        self.n_lat = batch * seq
        self.n_all = batch * (seq + ctx_len)
        self.n_groups = batch + 1

    def group(self, i, tm):
        lat_tiles = self.n_lat // tm
        return jnp.where(i < lat_tiles, 1 + i // (self.seq // tm), 0)

    def mod_row(self, layer, i, tm, which):
        return (layer * self.n_groups + self.group(i, tm)) * 6 + which

    def scan_block(self, b, s, reverse):
        if reverse:
            cblk = self.batch * self.nlt + b * self.nct + (self.nct - 1 - s)
            lblk = b * self.nlt + (self.nlt - 1 - (s - self.nct))
        else:
            cblk = self.batch * self.nlt + b * self.nct + s
            lblk = b * self.nlt + (s - self.nct)
        return jnp.where(s < self.nct, cblk, lblk)


def _mods_kernel(c_ref, w_ref, b_ref, o_ref):
    c = c_ref[...]
    s_hi, s_lo = _split(c * _sigmoid(c))
    w_hi, w_lo = _split(w_ref[...])
    acc = _dot(s_hi, w_hi) + _dot(s_lo, w_hi) + _dot(s_hi, w_lo)
    o_ref[...] = acc + b_ref[...]


def _mods(cvec, w_ada, b_ada):
    depth, d, n = w_ada.shape
    tn = _pick(n, (512, 256, 128))
    rows = cvec.shape[0]
    return pl.pallas_call(
        _mods_kernel,
        grid=(depth, n // tn),
        in_specs=[pl.BlockSpec((rows, d), lambda l, j: (0, 0)),
                  pl.BlockSpec((None, d, tn), lambda l, j: (l, 0, j)),
                  pl.BlockSpec((None, 1, tn), lambda l, j: (l, 0, j))],
        out_specs=pl.BlockSpec((None, rows, tn), lambda l, j: (l, 0, j)),
        out_shape=jax.ShapeDtypeStruct((depth, rows, n), F32),
        compiler_params=_cp(("arbitrary", "arbitrary")),
        name="adaln_mods",
    )(cvec, w_ada, b_ada.reshape(depth, 1, n))


def _norm_mod(x_ref, g_ref, sc_ref, sh_ref):
    x = x_ref[...]
    ms = jnp.mean(x * x, axis=-1, keepdims=True)
    y = x * lax.rsqrt(ms + EPS) * g_ref[...]
    return y * (1.0 + sc_ref[0]) + sh_ref[0]


def _norm_kernel(x_ref, g_ref, sc_ref, sh_ref, o_ref):
    o_ref[...] = _norm_mod(x_ref, g_ref, sc_ref, sh_ref).astype(BF16)


def _norm_router_kernel(n_exp, x_ref, g_ref, sc_ref, sh_ref, rwh_ref, rwl_ref, rb_ref,
                        o_ref, gate_ref):
    y = _norm_mod(x_ref, g_ref, sc_ref, sh_ref)
    o_ref[...] = y.astype(BF16)
    y_hi, y_lo = _split(y)
    rwh = rwh_ref[...]
    logits = _dot(y_hi, rwh) + _dot(y_lo, rwh) + _dot(y_hi, rwl_ref[...])
    scores = _sigmoid(logits)
    lane = lax.broadcasted_iota(jnp.int32, scores.shape, 1).astype(F32)
    work = jnp.where(lane < n_exp, scores + rb_ref[...], -jnp.inf)
    sel = jnp.zeros_like(scores)
    for _ in range(TOP_K):
        mx = jnp.max(work, axis=-1, keepdims=True)
        first = jnp.min(jnp.where(work == mx, lane, float(LANES)), axis=-1, keepdims=True)
        hit = lane == first
        sel = jnp.where(hit, scores, sel)
        work = jnp.where(hit, -jnp.inf, work)
    gate_ref[...] = sel / jnp.sum(sel, axis=-1, keepdims=True) * ROUTED_SCALE


def _norm(rows, x_all, n_all, gain, modsflat, layer, which_scale, which_shift, router=None):
    d = x_all.shape[1]
    tm = ROW_BLK

    def mod_idx(which):
        return lambda i: (rows.mod_row(layer, i, tm, which), 0, 0)

    in_specs = [pl.BlockSpec((tm, d), lambda i: (i, 0)),
                pl.BlockSpec((1, d), lambda i: (0, 0)),
                pl.BlockSpec((1, 1, d), mod_idx(which_scale)),
                pl.BlockSpec((1, 1, d), mod_idx(which_shift))]
    args = [x_all, gain.reshape(1, d), modsflat, modsflat]
    out_spec = pl.BlockSpec((tm, d), lambda i: (i, 0))
    out_shape = jax.ShapeDtypeStruct((n_all, d), BF16)
    if router is None:
        return pl.pallas_call(
            _norm_kernel, grid=(n_all // tm,), in_specs=in_specs, out_specs=out_spec,
            out_shape=out_shape, compiler_params=_cp(("arbitrary",)), name="prenorm",
        )(*args)
    rw_hi, rw_lo, rbias, n_exp = router
    in_specs += [pl.BlockSpec((d, LANES), lambda i: (0, 0)),
                 pl.BlockSpec((d, LANES), lambda i: (0, 0)),
                 pl.BlockSpec((1, LANES), lambda i: (0, 0))]
    return pl.pallas_call(
        functools.partial(_norm_router_kernel, n_exp),
        grid=(n_all // tm,), in_specs=in_specs,
        out_specs=[out_spec, pl.BlockSpec((tm, LANES), lambda i: (i, 0))],
        out_shape=[out_shape, jax.ShapeDtypeStruct((n_all, LANES), F32)],
        compiler_params=_cp(("arbitrary",)), name="prenorm_router",
    )(*args, rw_hi, rw_lo, rbias)


def _mm_kernel(a_ref, w_ref, o_ref):
    o_ref[...] = _dot(a_ref[...], w_ref[...]).astype(o_ref.dtype)


def _matmul(a, w, out_dtype, tm=512):
    m, k = a.shape
    n = w.shape[1]
    tn = _pick(n, (1024, 512, 256, 128))
    return pl.pallas_call(
        _mm_kernel,
        grid=(n // tn, m // tm),
        in_specs=[pl.BlockSpec((tm, k), lambda j, i: (i, 0)),
                  pl.BlockSpec((k, tn), lambda j, i: (0, j))],
        out_specs=pl.BlockSpec((tm, tn), lambda j, i: (i, j)),
        out_shape=jax.ShapeDtypeStruct((m, n), out_dtype),
        compiler_params=_cp(("arbitrary", "arbitrary")),
        name="in_proj",
    )(a, w)


def _hgrn_chunk(q, k, lf, v, st, reverse):
    c = HG_CHUNK
    r_i = lax.broadcasted_iota(jnp.int32, (c, c), 0)
    c_i = lax.broadcasted_iota(jnp.int32, (c, c), 1)
    tri = (c_i >= r_i) if reverse else (c_i <= r_i)
    tri = jnp.where(tri, 1.0, 0.0).astype(BF16)
    lf_hi, lf_lo = _split(lf)
    bcum = _dot(tri, lf_hi) + _dot(tri, lf_lo)
    last = 0 if reverse else c - 1
    b_last = bcum[last:last + 1]
    o_inter = _dot_nt((q * jnp.exp(bcum)).astype(BF16), st.astype(BF16))
    kd = k * jnp.exp(b_last - bcum)
    st_new = st * jnp.exp(b_last) + _dot_tn(v.astype(BF16), kd.astype(BF16))

    ones = jnp.ones((LANES, LANES), BF16)
    sub_row = lax.broadcasted_iota(jnp.int32, (HG_SUB, LANES), 0)
    v16 = v.astype(BF16)
    outs = []
    for i in range(c // HG_SUB):
        r0 = i * HG_SUB
        bsub = bcum[r0:r0 + HG_SUB]
        qsub = q[r0:r0 + HG_SUB]
        zs = []
        for s in range(HG_SUB):
            keep = (sub_row <= s) if reverse else (sub_row >= s)
            dl = jnp.where(keep, bsub - bcum[r0 + s:r0 + s + 1], NEG_BIG)
            zs.append(qsub * (k[r0 + s:r0 + s + 1] * jnp.exp(dl)))
        red = _dot(jnp.concatenate(zs, axis=0).astype(BF16), ones)
        o_sub = red[0:HG_SUB] * v[r0:r0 + 1]
        for s in range(1, HG_SUB):
            o_sub = o_sub + red[s * HG_SUB:(s + 1) * HG_SUB] * v[r0 + s:r0 + s + 1]
        lo, hi = (r0 + HG_SUB, c) if reverse else (0, r0)
        if hi > lo:
            ref_row = bcum[lo:lo + 1] if reverse else bcum[hi - 1:hi]
            qi = (qsub * jnp.exp(bsub - ref_row)).astype(BF16)
            kt = (k[lo:hi] * jnp.exp(ref_row - bcum[lo:hi])).astype(BF16)
            o_sub = o_sub + _dot(_dot_nt(qi, kt).astype(BF16), v16[lo:hi])
        outs.append(o_sub)
    return o_inter + jnp.concatenate(outs, axis=0), st_new


def _hgrn_kernel(layer, reverse, q_ref, f_ref, v_ref, lb_ref, o_ref, st_ref):
    @pl.when(pl.program_id(2) == 0)
    def _():
        st_ref[...] = jnp.zeros_like(st_ref)

    lg = lb_ref[0]
    e = jnp.exp(lg - jnp.max(lg, axis=0, keepdims=True))
    p = e / jnp.sum(e, axis=0, keepdims=True)
    lb = jnp.zeros((1, LANES), F32)
    for i in range(1, layer + 1):
        lb = lb + p[i:i + 1]

    n_chunk = ROW_BLK // HG_CHUNK
    st = st_ref[...]
    order = range(n_chunk - 1, -1, -1) if reverse else range(n_chunk)
    for ci in order:
        rows = pl.ds(ci * HG_CHUNK, HG_CHUNK)
        qp = q_ref[rows, :].astype(F32)
        fp = f_ref[rows, :].astype(F32)
        v = v_ref[rows, :].astype(F32)
        q = qp * _sigmoid(qp) * HEAD_W ** -0.5
        f = lb + (1.0 - lb) * _sigmoid(fp)
        lf = jnp.log(jnp.maximum(f, TINY))
        k = (1.0 - lb) * _sigmoid(-fp)
        o, st = _hgrn_chunk(q, k, lf, v, st, reverse)
        o_ref[rows, :] = o.astype(o_ref.dtype)
    st_ref[...] = st


def _hgrn_scan(rows, p_main, lb_logits, layer, direction):
    reverse = direction == 1
    n_all = p_main.shape[0]
    steps = rows.nct + rows.nlt
    depth = lb_logits.shape[1]

    def spec(seg):
        return pl.BlockSpec((ROW_BLK, HEAD_W),
                            lambda b, h, s: (rows.scan_block(b, s, reverse), seg * HEADS + h))

    f_seg = S_HG_FB if reverse else S_HG_FF
    return pl.pallas_call(
        functools.partial(_hgrn_kernel, layer, reverse),
        grid=(rows.batch, HEADS, steps),
        in_specs=[spec(S_HG_Q), spec(f_seg), spec(S_HG_I),
                  pl.BlockSpec((1, depth, HEAD_W), lambda b, h, s: (direction, 0, h))],
        out_specs=pl.BlockSpec((ROW_BLK, HEAD_W),
                               lambda b, h, s: (rows.scan_block(b, s, reverse), h)),
        out_shape=jax.ShapeDtypeStruct((n_all, SEG), BF16),
        scratch_shapes=[pltpu.VMEM((HEAD_W, HEAD_W), F32)],
        compiler_params=_cp(("arbitrary", "arbitrary", "arbitrary")),
        name="hgrn2_scan",
    )(p_main, p_main, p_main, lb_logits)


def _mlstm_kernel(reverse, q_ref, k_ref, v_ref, ig_ref, fg_ref, ib_ref, fb_ref, o_ref,
                  s_ref, m_ref):
    c = ROW_BLK

    @pl.when(pl.program_id(2) == 0)
    def _():
        s_ref[...] = jnp.zeros_like(s_ref)
        m_ref[...] = jnp.zeros_like(m_ref)

    q = q_ref[...]
    kt = (k_ref[...].astype(F32) * HEAD_W ** -0.5).T
    v_cat = jnp.concatenate([v_ref[...], jnp.ones((c, HEAD_W), BF16)], axis=1)
    ig = ig_ref[0] + ib_ref[0][:, :1]
    fx = fg_ref[0] + fb_ref[0][:, :1]
    lf = jnp.minimum(fx, 0.0) - jnp.log1p(jnp.exp(-jnp.abs(fx)))

    r_i = lax.broadcasted_iota(jnp.int32, (c, c), 0)
    c_i = lax.broadcasted_iota(jnp.int32, (c, c), 1)
    cum = jnp.where((r_i >= c_i) if reverse else (r_i <= c_i), 1.0, 0.0).astype(BF16)
    lf8 = jnp.broadcast_to(lf, (8, c))
    lf_hi, lf_lo = _split(lf8)
    brow = (_dot(lf_hi, cum) + _dot(lf_lo, cum))[0:1]
    b_rows = jnp.broadcast_to(brow, (c, c))
    b_cols = b_rows.T
    keep = (c_i >= r_i) if reverse else (c_i <= r_i)
    dlog = jnp.where(keep, b_cols - b_rows + ig, NEG_BIG)
    m_prev = m_ref[:, :1]
    inter = b_cols[:, :1] + m_prev
    m_t = jnp.maximum(jnp.max(dlog, axis=-1, keepdims=True), inter)
    w = jnp.exp(dlog - m_t) * _dot(q, kt.astype(BF16))
    w_inter = jnp.exp(inter - m_t)
    state = s_ref[...]
    nd = _dot(w.astype(BF16), v_cat) + w_inter * _dot(q, state.astype(BF16))
    num, den = nd[:, :HEAD_W], nd[:, HEAD_W:]
    o_ref[...] = (num / jnp.maximum(jnp.abs(den), jnp.exp(-m_t))).astype(o_ref.dtype)

    last = 0 if reverse else c - 1
    m_new = m_t[last:last + 1]
    b_last = brow[:, last:last + 1]
    wk = jnp.exp(b_last - brow + ig - m_new)
    dec = jnp.exp(b_last + m_prev - m_new)
    s_ref[...] = dec * state + _dot((kt * wk).astype(BF16), v_cat)
    m_ref[...] = jnp.broadcast_to(m_new, m_ref.shape)


def _mlstm_scan(rows, p_main, gates_t, i_bias, f_bias, direction):
    reverse = direction == 1
    n_all = p_main.shape[0]
    steps = rows.nct + rows.nlt

    def spec(seg):
        return pl.BlockSpec((ROW_BLK, HEAD_W),
                            lambda b, h, s: (rows.scan_block(b, s, reverse), seg * HEADS + h))

    def gate_spec(which):
        return pl.BlockSpec((1, 1, ROW_BLK),
                            lambda b, h, s: (which * HEADS + h, 0, rows.scan_block(b, s, reverse)))

    def bias_spec():
        return pl.BlockSpec((1, 1, LANES), lambda b, h, s: (direction * HEADS + h, 0, 0))

    return pl.pallas_call(
        functools.partial(_mlstm_kernel, reverse),
        grid=(rows.batch, HEADS, steps),
        in_specs=[spec(S_ML_Q), spec(S_ML_K), spec(S_ML_V),
                  gate_spec(direction), gate_spec(2 + direction), bias_spec(), bias_spec()],
        out_specs=pl.BlockSpec((ROW_BLK, HEAD_W),
                               lambda b, h, s: (rows.scan_block(b, s, reverse), h)),
        out_shape=jax.ShapeDtypeStruct((n_all, SEG), BF16),
        scratch_shapes=[pltpu.VMEM((HEAD_W, 2 * HEAD_W), F32), pltpu.VMEM((1, LANES), F32)],
        compiler_params=_cp(("arbitrary", "arbitrary", "arbitrary")),
        name="mlstm_scan",
    )(p_main, p_main, p_main, gates_t, gates_t, i_bias, f_bias)


def _da_prep_kernel(q_ref, k_ref, v_ref, cos_ref, sin_ref, qg_ref, kg_ref, qt_ref, ko_ref, vt_ref):
    r_i = lax.broadcasted_iota(jnp.int32, (LANES, LANES), 0)
    c_i = lax.broadcasted_iota(jnp.int32, (LANES, LANES), 1)
    blockdiag = jnp.where((r_i // DA_QK) == (c_i // DA_QK), 1.0 / DA_QK, 0.0).astype(BF16)
    cos, sin = cos_ref[...], sin_ref[...]
    lane = lax.broadcasted_iota(jnp.int32, cos.shape, 1)
    first_half = (lane % DA_QK) < (DA_QK // 2)

    def qk_norm_rope(x, gain):
        x_hi, x_lo = _split(x * x)
        ms = _dot(x_hi, blockdiag) + _dot(x_lo, blockdiag)
        y = x * lax.rsqrt(ms + EPS) * gain
        rot = jnp.where(first_half, -pltpu.roll(y, LANES - DA_QK // 2, 1),
                        pltpu.roll(y, DA_QK // 2, 1))
        return y * cos + rot * sin

    for h in range(HEADS):
        cols = slice(h * HEAD_W, (h + 1) * HEAD_W)
        q = qk_norm_rope(q_ref[:, cols].astype(F32), qg_ref[...]) * (DA_QK ** -0.5 * LOG2E)
        qt_ref[cols, :] = q.T.astype(BF16)
        ko_ref[:, cols] = qk_norm_rope(k_ref[:, cols].astype(F32), kg_ref[...]).astype(BF16)
        vt_ref[cols, :] = v_ref[:, cols].astype(F32).T.astype(BF16)


def _da_prep(rows, p_main, cos_tab, sin_tab, q_gain, k_gain):
    n_all = p_main.shape[0]
    tm = ROW_BLK
    out_t = jax.ShapeDtypeStruct((SEG, n_all), BF16)

    def tab_idx(i):
        lat = i % rows.nlt
        ctx = rows.nlt + (i - rows.batch * rows.nlt) % rows.nct
        return (jnp.where(i < rows.batch * rows.nlt, lat, ctx), 0)

    out = jax.ShapeDtypeStruct((n_all, SEG), BF16)
    return pl.pallas_call(
        _da_prep_kernel,
        grid=(n_all // tm,),
        in_specs=[pl.BlockSpec((tm, SEG), lambda i: (i, S_DA_Q)),
                  pl.BlockSpec((tm, SEG), lambda i: (i, S_DA_K)),
                  pl.BlockSpec((tm, SEG), lambda i: (i, S_DA_V)),
                  pl.BlockSpec((tm, LANES), tab_idx),
                  pl.BlockSpec((tm, LANES), tab_idx),
                  pl.BlockSpec((1, LANES), lambda i: (0, 0)),
                  pl.BlockSpec((1, LANES), lambda i: (0, 0))],
        out_specs=[pl.BlockSpec((SEG, tm), lambda i: (0, i)),
                   pl.BlockSpec((tm, SEG), lambda i: (i, 0)),
                   pl.BlockSpec((SEG, tm), lambda i: (0, i))],
        out_shape=[out_t, out, out_t],
        compiler_params=_cp(("arbitrary",)),
        name="da_qk_prep",
    )(p_main, p_main, p_main, cos_tab, sin_tab, q_gain, k_gain)


def _attn_kernel(lam_init, n_lat_q_blocks, n_lat_keys, tk, qt_ref, kc_ref, vct_ref, kl_ref, vlt_ref,
                 lam_ref, sn_ref, o_ref, acc_ref, m_ref, l_ref, sa_ref, sb_ref):
    tq = qt_ref.shape[1]
    qt = qt_ref[...]
    chan = lax.broadcasted_iota(jnp.int32, qt.shape, 0)
    zero = jnp.zeros_like(qt)
    q2t = jnp.concatenate([jnp.where(chan < DA_QK, qt, zero), jnp.where(chan >= DA_QK, qt, zero)],
                          axis=1)

    st = _dot(kc_ref[...], q2t)
    m0 = jnp.max(st, axis=0, keepdims=True)
    p = jnp.exp2(st - m0)
    m_ref[...] = m0
    l_ref[...] = jnp.sum(p, axis=0, keepdims=True)
    acc_ref[...] = _dot(vct_ref[...], p.astype(BF16))

    n_chunks = n_lat_keys // tk
    unroll = _pick(n_chunks, (4, 2, 1))
    bufs = (sa_ref, sb_ref) if unroll > 1 else (sa_ref, sa_ref)

    def chunk(c):
        return pl.ds(pl.multiple_of(c * tk, tk), tk)

    def scores(c, buf):
        st = _dot(kl_ref[chunk(c), :], q2t)
        buf[...] = st
        return jnp.max(st, axis=0, keepdims=True)

    def absorb(c, buf, m_cur):
        m_old = m_ref[...]
        m_new = jnp.maximum(m_old, m_cur)
        p = jnp.exp2(buf[...] - m_new)
        pv = _dot(vlt_ref[:, chunk(c)], p.astype(BF16))
        alpha = jnp.exp2(m_old - m_new)
        l_ref[...] = alpha * l_ref[...] + jnp.sum(p, axis=0, keepdims=True)
        acc_ref[...] = alpha * acc_ref[...] + pv
        m_ref[...] = m_new

    def body(j, m_pend):
        for u in range(unroll):
            c = j * unroll + u
            if unroll > 1:
                m_next = scores(jnp.minimum(c + 1, n_chunks - 1), bufs[(u + 1) % 2])
                absorb(c, bufs[u % 2], m_pend)
            else:
                absorb(c, bufs[0], m_pend)
                m_next = scores(jnp.minimum(c + 1, n_chunks - 1), bufs[0])
            m_pend = m_next
        return m_pend

    n_trips = jnp.where(pl.program_id(2) < n_lat_q_blocks, n_chunks // unroll, 0)
    lax.fori_loop(0, n_trips, body, scores(0, bufs[0]))

    lv = lam_ref[...]
    lam = (jnp.exp(jnp.sum(lv[0:1] * lv[1:2], axis=-1, keepdims=True))
           - jnp.exp(jnp.sum(lv[2:3] * lv[3:4], axis=-1, keepdims=True)) + lam_init)
    o = acc_ref[...] / l_ref[...]
    a = o[:, :tq] - lam * o[:, tq:]
    ms = jnp.mean(a * a, axis=0, keepdims=True)
    y = a * lax.rsqrt(ms + EPS) * sn_ref[...] * (1.0 - lam_init)
    o_ref[...] = y.T.astype(o_ref.dtype)


def _attention(rows, qt, kh, vt, lam_vec, sub_gain, lam_init, n_rows):
    tq = ROW_BLK
    assert rows.nct == 1
    ctx_blk = lambda b: rows.batch * rows.nlt + b
    ctx_q = n_rows > rows.n_lat
    q_blk = lambda b, i: jnp.where(i < rows.nlt, b * rows.nlt + i, ctx_blk(b))
    tk = _pick(rows.seq, (512, 256))
    return pl.pallas_call(
        functools.partial(_attn_kernel, lam_init, rows.nlt, rows.seq, tk),
        grid=(rows.batch, HEADS, rows.nlt + (1 if ctx_q else 0)),
        in_specs=[pl.BlockSpec((HEAD_W, tq), lambda b, h, i: (h, q_blk(b, i))),
                  pl.BlockSpec((ROW_BLK, HEAD_W), lambda b, h, i: (ctx_blk(b), h)),
                  pl.BlockSpec((HEAD_W, ROW_BLK), lambda b, h, i: (h, ctx_blk(b))),
                  pl.BlockSpec((rows.seq, HEAD_W), lambda b, h, i: (b, h)),
                  pl.BlockSpec((HEAD_W, rows.seq), lambda b, h, i: (h, b)),
                  pl.BlockSpec(lam_vec.shape, lambda b, h, i: (0, 0)),
                  pl.BlockSpec((HEAD_W, 1), lambda b, h, i: (0, 0))],
        out_specs=pl.BlockSpec((tq, HEAD_W), lambda b, h, i: (q_blk(b, i), h)),
        out_shape=jax.ShapeDtypeStruct((n_rows, SEG), BF16),
        scratch_shapes=[pltpu.VMEM((HEAD_W, 2 * tq), F32), pltpu.VMEM((1, 2 * tq), F32),
                        pltpu.VMEM((1, 2 * tq), F32), pltpu.VMEM((tk, 2 * tq), F32),
                        pltpu.VMEM((tk, 2 * tq), F32)],
        compiler_params=_cp(("arbitrary", "arbitrary", "arbitrary")),
        name="diff_attention",
    )(qt, kh, vt, kh, vt, lam_vec, sub_gain)


def _merge_kernel(ohf_ref, ohb_ref, hg_ref, da_ref, omf_ref, omb_ref, mo_ref, g0_ref, g1_ref,
                  g2_ref, hn_ref, mn_ref, wb_ref, o_ref, h_ref):
    @pl.when(pl.program_id(1) == 0)
    def _():
        for h in range(HEADS):
            cols = slice(h * HEAD_W, (h + 1) * HEAD_W)
            o = ohf_ref[:, cols].astype(F32) + ohb_ref[:, cols].astype(F32)
            y = o * lax.rsqrt(jnp.mean(o * o, axis=-1, keepdims=True) + EPS) * hn_ref[...]
            g = hg_ref[:, cols].astype(F32)
            h_ref[0, :, cols] = (y * (g * _sigmoid(g))).astype(BF16)
            o = omf_ref[:, cols].astype(F32) + omb_ref[:, cols].astype(F32)
            y = o * lax.rsqrt(jnp.mean(o * o, axis=-1, keepdims=True) + EPS) * mn_ref[...]
            h_ref[2, :, cols] = (y * _sigmoid(mo_ref[:, cols].astype(F32))).astype(BF16)
        h_ref[1] = da_ref[...]

    y = _sigmoid(g0_ref[...].astype(F32)) * _dot(h_ref[0], wb_ref[0])
    y = y + _sigmoid(g1_ref[...].astype(F32)) * _dot(h_ref[1], wb_ref[1])
    y = y + _sigmoid(g2_ref[...].astype(F32)) * _dot(h_ref[2], wb_ref[2])
    o_ref[...] = y.astype(o_ref.dtype)


def _merge(p_main, ohf, ohb, da, omf, omb, hg_gain, ml_gain, w_branch, d):
    n_all = da.shape[0]
    tm = 512
    tn = _pick(d, (512, 256, 128))
    gate_col0 = N_SEG * SEG // tn
    row = lambda i, j: (i, 0)

    def gate_spec(jj):
        return pl.BlockSpec((tm, tn), lambda i, j: (i, gate_col0 + jj * (d // tn) + j))

    return pl.pallas_call(
        _merge_kernel,
        grid=(n_all // tm, d // tn),
        in_specs=[pl.BlockSpec((tm, SEG), row), pl.BlockSpec((tm, SEG), row),
                  pl.BlockSpec((tm, SEG), lambda i, j: (i, S_HG_G)),
                  pl.BlockSpec((tm, SEG), row),
                  pl.BlockSpec((tm, SEG), row), pl.BlockSpec((tm, SEG), row),
                  pl.BlockSpec((tm, SEG), lambda i, j: (i, S_ML_O)),
                  gate_spec(0), gate_spec(1), gate_spec(2),
                  pl.BlockSpec((1, LANES), lambda i, j: (0, 0)),
                  pl.BlockSpec((1, LANES), lambda i, j: (0, 0)),
                  pl.BlockSpec((3, SEG, tn), lambda i, j: (0, 0, j))],
        out_specs=pl.BlockSpec((tm, tn), lambda i, j: (i, j)),
        out_shape=jax.ShapeDtypeStruct((n_all, d), BF16),
        scratch_shapes=[pltpu.VMEM((3, tm, SEG), BF16)],
        compiler_params=_cp(("arbitrary", "arbitrary")),
        name="branch_merge",
    )(ohf, ohb, p_main, da, omf, omb, p_main, p_main, p_main, p_main, hg_gain, ml_gain, w_branch)


def _proj_resid_kernel(a_ref, w_ref, x_ref, g_ref, o_ref):
    o_ref[...] = x_ref[...] + g_ref[0] * _dot(a_ref[...], w_ref[...])


def _proj_resid(rows, a, w, x_all, modsflat, layer, which_gate):
    d = x_all.shape[1]
    n_all, k = a.shape
    tm = 512
    tn = _pick(d, (1024, 512, 256, 128))
    return pl.pallas_call(
        _proj_resid_kernel,
        grid=(d // tn, n_all // tm),
        in_specs=[pl.BlockSpec((tm, k), lambda j, i: (i, 0)),
                  pl.BlockSpec((k, tn), lambda j, i: (0, j)),
                  pl.BlockSpec((tm, tn), lambda j, i: (i, j)),
                  pl.BlockSpec((1, 1, tn),
                               lambda j, i: (rows.mod_row(layer, i, tm, which_gate), 0, j))],
        out_specs=pl.BlockSpec((tm, tn), lambda j, i: (i, j)),
        out_shape=jax.ShapeDtypeStruct((n_all, d), F32),
        compiler_params=_cp(("arbitrary", "arbitrary")),
        name="out_proj_residual",
    )(a, w, x_all, modsflat)


def _moe_kernel(n_exp, tok_ref, gate_ref, w1_ref, w3_ref, w2_ref, o_ref, acc_ref):
    e = pl.program_id(1)

    @pl.when(e == 0)
    def _():
        acc_ref[...] = jnp.zeros_like(acc_ref)

    tok = tok_ref[...]
    gates = gate_ref[...]
    lane = lax.broadcasted_iota(jnp.int32, gates.shape, 1)
    g = jnp.sum(jnp.where(lane == e, gates, 0.0), axis=-1, keepdims=True)
    g = jnp.where(e == n_exp, 1.0, g)
    h1 = _dot(tok, w1_ref[...])
    h3 = _dot(tok, w3_ref[...])
    a = (h1 * _sigmoid(h1)) * h3 * g
    acc_ref[...] += _dot(a.astype(BF16), w2_ref[...])

    @pl.when(e == n_exp)
    def _():
        o_ref[...] = acc_ref[...].astype(o_ref.dtype)


def _moe(tok, gates, w1, w3, w2):
    n_all, d = tok.shape
    n_slots, _, de = w1.shape
    tm = 512
    return pl.pallas_call(
        functools.partial(_moe_kernel, n_slots - 1),
        grid=(n_all // tm, n_slots),
        in_specs=[pl.BlockSpec((tm, d), lambda i, e: (i, 0)),
                  pl.BlockSpec((tm, LANES), lambda i, e: (i, 0)),
                  pl.BlockSpec((None, d, de), lambda i, e: (e, 0, 0)),
                  pl.BlockSpec((None, d, de), lambda i, e: (e, 0, 0)),
                  pl.BlockSpec((None, de, d), lambda i, e: (e, 0, 0))],
        out_specs=pl.BlockSpec((tm, d), lambda i, e: (i, 0)),
        out_shape=jax.ShapeDtypeStruct((n_all, d), BF16),
        scratch_shapes=[pltpu.VMEM((tm, d), F32)],
        compiler_params=_cp(("arbitrary", "arbitrary")),
        name="moe_experts",
    )(tok, gates, w1, w3, w2)


def _resid_kernel(x_ref, y_ref, g_ref, o_ref):
    o_ref[...] = x_ref[...] + g_ref[0] * y_ref[...].astype(F32)


def _resid(rows, x_all, y, modsflat, layer, which_gate, n_rows):
    d = x_all.shape[1]
    tm = ROW_BLK
    blk = pl.BlockSpec((tm, d), lambda i: (i, 0))
    return pl.pallas_call(
        _resid_kernel,
        grid=(n_rows // tm,),
        in_specs=[blk, blk,
                  pl.BlockSpec((1, 1, d), lambda i: (rows.mod_row(layer, i, tm, which_gate), 0, 0))],
        out_specs=blk,
        out_shape=jax.ShapeDtypeStruct((n_rows, d), F32),
        compiler_params=_cp(("arbitrary",)),
        name="moe_residual",
    )(x_all, y, modsflat)


def _rope_tables(seq, ctx_len):
    n = DA_QK // 4
    t = jnp.arange(seq)
    inv = ROPE_THETA ** (-jnp.arange(n, dtype=F32) / n)
    row = (t // GRID_W).astype(F32)
    col = (t % GRID_W).astype(F32)
    ang = jnp.concatenate([row[:, None] * inv, col[:, None] * inv], axis=-1)
    ang = jnp.concatenate([ang, jnp.zeros((ctx_len, 2 * n), F32)], axis=0)
    return jnp.tile(jnp.cos(ang), (1, 4)), jnp.tile(jnp.sin(ang), (1, 4))


def kernel(x, c, ctx, c_ctx, norm1, norm2, w_ada, b_ada, w_in, hg_lb_logits, hg_norm, da_q_norm,
           da_k_norm, da_lambda, da_sub_norm, ml_igate_bias, ml_fgate_bias, ml_norm, w_branch,
           w_out, router_w, router_bias, exp_w1, exp_w3, exp_w2, sh_w1, sh_w3, sh_w2):
    batch, seq, d = x.shape
    ctx_len = ctx.shape[1]
    depth = w_ada.shape[0]
    n_exp = router_w.shape[-1]
    rows = _Rows(batch, seq, ctx_len)
    assert batch + 1 <= 16 and n_exp <= LANES

    x_all = jnp.concatenate([x.reshape(batch * seq, d), ctx.reshape(batch * ctx_len, d)], axis=0)
    cvec = jnp.zeros((16, d), F32).at[0].set(c_ctx).at[1:1 + batch].set(c)
    mods = _mods(cvec, w_ada, b_ada)
    modsflat = mods[:, :rows.n_groups].reshape(depth * rows.n_groups * 6, 1, d)
    cos_tab, sin_tab = _rope_tables(seq, ctx_len)
    gate_lo = N_SEG * SEG

    for l in range(depth):
        n_rows = rows.n_all if l < depth - 1 else rows.n_lat
        lam_init = 0.8 - 0.6 * math.exp(-0.3 * l)
        w_main = jnp.concatenate([w_in[l, :, :gate_lo], w_in[l, :, gate_lo + N_GATE_COLS:]],
                                 axis=1).astype(BF16)
        w_gate = jnp.pad(w_in[l, :, gate_lo:gate_lo + N_GATE_COLS],
                         ((0, 0), (0, LANES - N_GATE_COLS))).astype(BF16)

        hx = _norm(rows, x_all, rows.n_all, norm1[l], modsflat, l, 1, 0)
        p_main = _matmul(hx, w_main, BF16)
        p_gate = _matmul(hx, w_gate, F32)
        gates_t = p_gate[:, :N_GATE_COLS].T.reshape(N_GATE_COLS, 1, rows.n_all)

        ohf = _hgrn_scan(rows, p_main, hg_lb_logits, l, 0)
        ohb = _hgrn_scan(rows, p_main, hg_lb_logits, l, 1)

        tile2 = lambda g: jnp.tile(g.reshape(1, DA_QK), (1, 2))
        qt, kh, vt = _da_prep(rows, p_main, cos_tab, sin_tab, tile2(da_q_norm[l]),
                              tile2(da_k_norm[l]))
        da = _attention(rows, qt, kh, vt, da_lambda[l], da_sub_norm[l].reshape(HEAD_W, 1),
                        lam_init, n_rows)

        bias = lambda bv: jnp.broadcast_to(bv.reshape(2 * HEADS, 1, 1), (2 * HEADS, 1, LANES))
        i_b, f_b = bias(ml_igate_bias[l]), bias(ml_fgate_bias[l])
        omf = _mlstm_scan(rows, p_main, gates_t, i_b, f_b, 0)
        omb = _mlstm_scan(rows, p_main, gates_t, i_b, f_b, 1)

        ymid = _merge(p_main, ohf, ohb, da, omf, omb, hg_norm[l].reshape(1, HEAD_W),
                      ml_norm[l].reshape(1, HEAD_W), w_branch[l].astype(BF16), d)
        x_all = _proj_resid(rows, ymid, w_out[l].astype(BF16), x_all, modsflat, l, 2)

        rw = jnp.pad(router_w[l], ((0, 0), (0, LANES - n_exp)))
        rw_hi = rw.astype(BF16)
        rw_lo = (rw - rw_hi.astype(F32)).astype(BF16)
        rb = jnp.pad(router_bias[l], (0, LANES - n_exp)).reshape(1, LANES)
        tok, gates = _norm(rows, x_all, n_rows, norm2[l], modsflat, l, 4, 3,
                           router=(rw_hi, rw_lo, rb, n_exp))
        w1 = jnp.concatenate([exp_w1[l], sh_w1[l][None]], axis=0).astype(BF16)
        w3 = jnp.concatenate([exp_w3[l], sh_w3[l][None]], axis=0).astype(BF16)
        w2 = jnp.concatenate([exp_w2[l], sh_w2[l][None]], axis=0).astype(BF16)
        y = _moe(tok, gates, w1, w3, w2)
        x_all = _resid(rows, x_all, y, modsflat, l, 5, n_rows)

    return x_all.reshape(batch, seq, d)
```

```python
import functools
import math

import jax
import jax.numpy as jnp
from jax import lax
from jax.experimental import pallas as pl
from jax.experimental.pallas import tpu as pltpu

F32 = jnp.float32
BF16 = jnp.bfloat16

EPS = 1e-6
NEG_BIG = -1e30
TINY = 1e-30
GRID_W = 64
ROPE_THETA = 10000.0
ROUTED_SCALE = 2.5
TOP_K = 4
LOG2E = 1.4426950408889634

HEADS = 12
HEAD_W = 128
SEG = HEADS * HEAD_W
DA_QK = 64
N_GATE_COLS = 4 * HEADS

S_HG_Q, S_HG_FF, S_HG_FB, S_HG_I, S_HG_G = 0, 1, 2, 3, 4
S_DA_Q, S_DA_K, S_DA_V = 5, 6, 7
S_ML_Q, S_ML_K, S_ML_V, S_ML_O = 8, 9, 10, 11
N_SEG = 12

LANES = 128
ROW_BLK = 256
HG_CHUNK = 64
HG_SUB = 16
MOE_TILE = 256
VMEM_LIMIT = 56 * 1024 * 1024


def _cp(sem, vmem=VMEM_LIMIT):
    return pltpu.CompilerParams(dimension_semantics=sem, vmem_limit_bytes=vmem)


def _pick(n, cands):
    for c in cands:
        if n % c == 0:
            return c
    raise ValueError(f"no tile for {n} in {cands}")


def _dot(a, b):
    return jnp.dot(a, b, preferred_element_type=F32)


def _dot_nt(a, b):
    return lax.dot_general(a, b, (((1,), (1,)), ((), ())), preferred_element_type=F32)


def _dot_tn(a, b):
    return lax.dot_general(a, b, (((0,), (0,)), ((), ())), preferred_element_type=F32)


def _split(x):
    hi = x.astype(BF16)
    lo = (x - hi.astype(F32)).astype(BF16)
    return hi, lo


def _sigmoid(x):
    return 1.0 / (1.0 + jnp.exp(-x))


class _Rows:
    def __init__(self, batch, seq, ctx_len):
        self.batch, self.seq, self.ctx = batch, seq, ctx_len
        assert seq % ROW_BLK == 0 and ctx_len % ROW_BLK == 0
        self.nlt = seq // ROW_BLK
        self.nct = ctx_len // ROW_BLK
        self.n_lat = batch * seq
        self.n_all = batch * (seq + ctx_len)
        self.n_groups = batch + 1

    def group(self, i, tm):
        lat_tiles = self.n_lat // tm
        return jnp.where(i < lat_tiles, 1 + i // (self.seq // tm), 0)

    def mod_row(self, layer, i, tm, which):
        return (layer * self.n_groups + self.group(i, tm)) * 6 + which

    def scan_block(self, b, s, reverse):
        if reverse:
            cblk = self.batch * self.nlt + b * self.nct + (self.nct - 1 - s)
            lblk = b * self.nlt + (self.nlt - 1 - (s - self.nct))
        else:
            cblk = self.batch * self.nlt + b * self.nct + s
            lblk = b * self.nlt + (s - self.nct)
        return jnp.where(s < self.nct, cblk, lblk)


def _mods_kernel(c_ref, w_ref, b_ref, o_ref):
    c = c_ref[...]
    s_hi, s_lo = _split(c * _sigmoid(c))
    w_hi, w_lo = _split(w_ref[...])
    acc = _dot(s_hi, w_hi) + _dot(s_lo, w_hi) + _dot(s_hi, w_lo)
    o_ref[...] = acc + b_ref[...]


def _mods(cvec, w_ada, b_ada):
    depth, d, n = w_ada.shape
    tn = _pick(n, (512, 256, 128))
    rows = cvec.shape[0]
    return pl.pallas_call(
        _mods_kernel,
        grid=(depth, n // tn),
        in_specs=[pl.BlockSpec((rows, d), lambda l, j: (0, 0)),
                  pl.BlockSpec((None, d, tn), lambda l, j: (l, 0, j)),
                  pl.BlockSpec((None, 1, tn), lambda l, j: (l, 0, j))],
        out_specs=pl.BlockSpec((None, rows, tn), lambda l, j: (l, 0, j)),
        out_shape=jax.ShapeDtypeStruct((depth, rows, n), F32),
        compiler_params=_cp(("arbitrary", "arbitrary")),
        name="adaln_mods",
    )(cvec, w_ada, b_ada.reshape(depth, 1, n))


def _norm_mod(x_ref, g_ref, sc_ref, sh_ref):
    x = x_ref[...]
    ms = jnp.mean(x * x, axis=-1, keepdims=True)
    y = x * lax.rsqrt(ms + EPS) * g_ref[...]
    return y * (1.0 + sc_ref[0]) + sh_ref[0]


def _norm_kernel(x_ref, g_ref, sc_ref, sh_ref, o_ref):
    o_ref[...] = _norm_mod(x_ref, g_ref, sc_ref, sh_ref).astype(BF16)


def _pack_pair(lo, hi):
    lo_b = pltpu.bitcast(lo.astype(BF16).astype(F32), jnp.int32)
    hi_b = pltpu.bitcast(hi.astype(BF16).astype(F32), jnp.int32)
    return (hi_b & jnp.int32(-65536)) | lax.shift_right_logical(lo_b, 16)


def _unpack_pair(w):
    lo = pltpu.bitcast(lax.shift_left(w, 16), F32)
    hi = pltpu.bitcast(w & jnp.int32(-65536), F32)
    return lo, hi


def _store_slabs(y, o_ref):
    tm, d = y.shape
    ns = d // (2 * LANES)
    for s in range(ns):
        lo = y[:, s * LANES:(s + 1) * LANES]
        hi = y[:, d // 2 + s * LANES:d // 2 + (s + 1) * LANES]
        o_ref[pl.ds(s, tm, stride=ns), :] = _pack_pair(lo, hi)


def _load_slabs(src_ref, row0, tm, ns, emit):
    for s in range(ns):
        lo, hi = _unpack_pair(src_ref[pl.ds(row0 + s, tm, stride=ns), :])
        emit(s, lo, hi)


def _norm_router_kernel(n_exp, x_ref, g_ref, sc_ref, sh_ref, rwh_ref, rwl_ref, rb_ref,
                        o_ref, slab_ref, route_ref):
    y = _norm_mod(x_ref, g_ref, sc_ref, sh_ref)
    o_ref[...] = y.astype(BF16)
    _store_slabs(y, slab_ref)
    y_hi, y_lo = _split(y)
    rwh = rwh_ref[...]
    logits = _dot(y_hi, rwh) + _dot(y_lo, rwh) + _dot(y_hi, rwl_ref[...])
    scores = _sigmoid(logits)
    lane = lax.broadcasted_iota(jnp.int32, scores.shape, 1).astype(F32)
    work = jnp.where(lane < n_exp, scores + rb_ref[...], -jnp.inf)
    route = jnp.zeros_like(scores)
    total = jnp.zeros_like(scores[:, :1])
    for k in range(TOP_K):
        mx = jnp.max(work, axis=-1, keepdims=True)
        first = jnp.min(jnp.where(work == mx, lane, float(LANES)), axis=-1, keepdims=True)
        hit = lane == first
        sc = jnp.sum(jnp.where(hit, scores, 0.0), axis=-1, keepdims=True)
        total = total + sc
        route = jnp.where(lane == k, first, route)
        route = jnp.where(lane == TOP_K + k, sc, route)
        work = jnp.where(hit, -jnp.inf, work)
    is_w = (lane >= TOP_K) & (lane < 2 * TOP_K)
    route_ref[...] = jnp.where(is_w, route / total * ROUTED_SCALE, route)


def _norm(rows, x_all, n_all, gain, modsflat, layer, which_scale, which_shift, router=None):
    d = x_all.shape[1]
    tm = ROW_BLK

    def mod_idx(which):
        return lambda i: (rows.mod_row(layer, i, tm, which), 0, 0)

    in_specs = [pl.BlockSpec((tm, d), lambda i: (i, 0)),
                pl.BlockSpec((1, d), lambda i: (0, 0)),
                pl.BlockSpec((1, 1, d), mod_idx(which_scale)),
                pl.BlockSpec((1, 1, d), mod_idx(which_shift))]
    args = [x_all, gain.reshape(1, d), modsflat, modsflat]
    out_spec = pl.BlockSpec((tm, d), lambda i: (i, 0))
    out_shape = jax.ShapeDtypeStruct((n_all, d), BF16)
    if router is None:
        return pl.pallas_call(
            _norm_kernel, grid=(n_all // tm,), in_specs=in_specs, out_specs=out_spec,
            out_shape=out_shape, compiler_params=_cp(("arbitrary",)), name="prenorm",
        )(*args)
    rw_hi, rw_lo, rbias, n_exp = router
    ns = d // (2 * LANES)
    in_specs += [pl.BlockSpec((d, LANES), lambda i: (0, 0)),
                 pl.BlockSpec((d, LANES), lambda i: (0, 0)),
                 pl.BlockSpec((1, LANES), lambda i: (0, 0))]
    return pl.pallas_call(
        functools.partial(_norm_router_kernel, n_exp),
        grid=(n_all // tm,), in_specs=in_specs,
        out_specs=[out_spec, pl.BlockSpec((tm * ns, LANES), lambda i: (i, 0)),
                   pl.BlockSpec((tm, LANES), lambda i: (i, 0))],
        out_shape=[out_shape, jax.ShapeDtypeStruct((n_all * ns, LANES), jnp.int32),
                   jax.ShapeDtypeStruct((n_all, LANES), F32)],
        compiler_params=_cp(("arbitrary",)), name="prenorm_router",
    )(*args, rw_hi, rw_lo, rbias)


def _mm_kernel(a_ref, w_ref, o_ref):
    o_ref[...] = _dot(a_ref[...], w_ref[...]).astype(o_ref.dtype)


def _matmul(a, w, out_dtype, tm=512):
    m, k = a.shape
    n = w.shape[1]
    tn = _pick(n, (1024, 512, 256, 128))
    return pl.pallas_call(
        _mm_kernel,
        grid=(n // tn, m // tm),
        in_specs=[pl.BlockSpec((tm, k), lambda j, i: (i, 0)),
                  pl.BlockSpec((k, tn), lambda j, i: (0, j))],
        out_specs=pl.BlockSpec((tm, tn), lambda j, i: (i, j)),
        out_shape=jax.ShapeDtypeStruct((m, n), out_dtype),
        compiler_params=_cp(("arbitrary", "arbitrary")),
        name="in_proj",
    )(a, w)


def _hgrn_chunk(q, k, lf, v, st, reverse):
    c = HG_CHUNK
    r_i = lax.broadcasted_iota(jnp.int32, (c, c), 0)
    c_i = lax.broadcasted_iota(jnp.int32, (c, c), 1)
    tri = (c_i >= r_i) if reverse else (c_i <= r_i)
    tri = jnp.where(tri, 1.0, 0.0).astype(BF16)
    lf_hi, lf_lo = _split(lf)
    bcum = _dot(tri, lf_hi) + _dot(tri, lf_lo)
    last = 0 if reverse else c - 1
    b_last = bcum[last:last + 1]
    o_inter = _dot_nt((q * jnp.exp(bcum)).astype(BF16), st.astype(BF16))
    kd = k * jnp.exp(b_last - bcum)
    st_new = st * jnp.exp(b_last) + _dot_tn(v.astype(BF16), kd.astype(BF16))

    ones = jnp.ones((LANES, LANES), BF16)
    sub_row = lax.broadcasted_iota(jnp.int32, (HG_SUB, LANES), 0)
    v16 = v.astype(BF16)
    outs = []
    for i in range(c // HG_SUB):
        r0 = i * HG_SUB
        bsub = bcum[r0:r0 + HG_SUB]
        qsub = q[r0:r0 + HG_SUB]
        zs = []
        for s in range(HG_SUB):
            keep = (sub_row <= s) if reverse else (sub_row >= s)
            dl = jnp.where(keep, bsub - bcum[r0 + s:r0 + s + 1], NEG_BIG)
            zs.append(qsub * (k[r0 + s:r0 + s + 1] * jnp.exp(dl)))
        red = _dot(jnp.concatenate(zs, axis=0).astype(BF16), ones)
        o_sub = red[0:HG_SUB] * v[r0:r0 + 1]
        for s in range(1, HG_SUB):
            o_sub = o_sub + red[s * HG_SUB:(s + 1) * HG_SUB] * v[r0 + s:r0 + s + 1]
        lo, hi = (r0 + HG_SUB, c) if reverse else (0, r0)
        if hi > lo:
            ref_row = bcum[lo:lo + 1] if reverse else bcum[hi - 1:hi]
            qi = (qsub * jnp.exp(bsub - ref_row)).astype(BF16)
            kt = (k[lo:hi] * jnp.exp(ref_row - bcum[lo:hi])).astype(BF16)
            o_sub = o_sub + _dot(_dot_nt(qi, kt).astype(BF16), v16[lo:hi])
        outs.append(o_sub)
    return o_inter + jnp.concatenate(outs, axis=0), st_new


def _hgrn_kernel(layer, reverse, q_ref, f_ref, v_ref, lb_ref, o_ref, st_ref):
    @pl.when(pl.program_id(2) == 0)
    def _():
        st_ref[...] = jnp.zeros_like(st_ref)

    lg = lb_ref[0]
    e = jnp.exp(lg - jnp.max(lg, axis=0, keepdims=True))
    p = e / jnp.sum(e, axis=0, keepdims=True)
    lb = jnp.zeros((1, LANES), F32)
    for i in range(1, layer + 1):
        lb = lb + p[i:i + 1]

    n_chunk = ROW_BLK // HG_CHUNK
    st = st_ref[...]
    order = range(n_chunk - 1, -1, -1) if reverse else range(n_chunk)
    for ci in order:
        rows = pl.ds(ci * HG_CHUNK, HG_CHUNK)
        qp = q_ref[rows, :].astype(F32)
        fp = f_ref[rows, :].astype(F32)
        v = v_ref[rows, :].astype(F32)
        q = qp * _sigmoid(qp) * HEAD_W ** -0.5
        f = lb + (1.0 - lb) * _sigmoid(fp)
        lf = jnp.log(jnp.maximum(f, TINY))
        k = (1.0 - lb) * _sigmoid(-fp)
        o, st = _hgrn_chunk(q, k, lf, v, st, reverse)
        o_ref[rows, :] = o.astype(o_ref.dtype)
    st_ref[...] = st


def _hgrn_scan(rows, p_main, lb_logits, layer, direction):
    reverse = direction == 1
    n_all = p_main.shape[0]
    steps = rows.nct + rows.nlt
    depth = lb_logits.shape[1]

    def spec(seg):
        return pl.BlockSpec((ROW_BLK, HEAD_W),
                            lambda b, h, s: (rows.scan_block(b, s, reverse), seg * HEADS + h))

    f_seg = S_HG_FB if reverse else S_HG_FF
    return pl.pallas_call(
        functools.partial(_hgrn_kernel, layer, reverse),
        grid=(rows.batch, HEADS, steps),
        in_specs=[spec(S_HG_Q), spec(f_seg), spec(S_HG_I),
                  pl.BlockSpec((1, depth, HEAD_W), lambda b, h, s: (direction, 0, h))],
        out_specs=pl.BlockSpec((ROW_BLK, HEAD_W),
                               lambda b, h, s: (rows.scan_block(b, s, reverse), h)),
        out_shape=jax.ShapeDtypeStruct((n_all, SEG), BF16),
        scratch_shapes=[pltpu.VMEM((HEAD_W, HEAD_W), F32)],
        compiler_params=_cp(("arbitrary", "arbitrary", "arbitrary")),
        name="hgrn2_scan",
    )(p_main, p_main, p_main, lb_logits)


def _mlstm_kernel(reverse, q_ref, k_ref, v_ref, ig_ref, fg_ref, ib_ref, fb_ref, o_ref,
                  s_ref, m_ref):
    c = ROW_BLK

    @pl.when(pl.program_id(2) == 0)
    def _():
        s_ref[...] = jnp.zeros_like(s_ref)
        m_ref[...] = jnp.zeros_like(m_ref)

    q = q_ref[...]
    kt = (k_ref[...].astype(F32) * HEAD_W ** -0.5).T
    v_cat = jnp.concatenate([v_ref[...], jnp.ones((c, HEAD_W), BF16)], axis=1)
    ig = ig_ref[0] + ib_ref[0][:, :1]
    fx = fg_ref[0] + fb_ref[0][:, :1]
    lf = jnp.minimum(fx, 0.0) - jnp.log1p(jnp.exp(-jnp.abs(fx)))

    r_i = lax.broadcasted_iota(jnp.int32, (c, c), 0)
    c_i = lax.broadcasted_iota(jnp.int32, (c, c), 1)
    cum = jnp.where((r_i >= c_i) if reverse else (r_i <= c_i), 1.0, 0.0).astype(BF16)
    lf8 = jnp.broadcast_to(lf, (8, c))
    lf_hi, lf_lo = _split(lf8)
    brow = (_dot(lf_hi, cum) + _dot(lf_lo, cum))[0:1]
    b_rows = jnp.broadcast_to(brow, (c, c))
    b_cols = b_rows.T
    keep = (c_i >= r_i) if reverse else (c_i <= r_i)
    dlog = jnp.where(keep, b_cols - b_rows + ig, NEG_BIG)
    m_prev = m_ref[:, :1]
    inter = b_cols[:, :1] + m_prev
    m_t = jnp.maximum(jnp.max(dlog, axis=-1, keepdims=True), inter)
    w = jnp.exp(dlog - m_t) * _dot(q, kt.astype(BF16))
    w_inter = jnp.exp(inter - m_t)
    state = s_ref[...]
    nd = _dot(w.astype(BF16), v_cat) + w_inter * _dot(q, state.astype(BF16))
    num, den = nd[:, :HEAD_W], nd[:, HEAD_W:]
    o_ref[...] = (num / jnp.maximum(jnp.abs(den), jnp.exp(-m_t))).astype(o_ref.dtype)

    last = 0 if reverse else c - 1
    m_new = m_t[last:last + 1]
    b_last = brow[:, last:last + 1]
    wk = jnp.exp(b_last - brow + ig - m_new)
    dec = jnp.exp(b_last + m_prev - m_new)
    s_ref[...] = dec * state + _dot((kt * wk).astype(BF16), v_cat)
    m_ref[...] = jnp.broadcast_to(m_new, m_ref.shape)


def _mlstm_scan(rows, p_main, gates_t, i_bias, f_bias, direction):
    reverse = direction == 1
    n_all = p_main.shape[0]
    steps = rows.nct + rows.nlt

    def spec(seg):
        return pl.BlockSpec((ROW_BLK, HEAD_W),
                            lambda b, h, s: (rows.scan_block(b, s, reverse), seg * HEADS + h))

    def gate_spec(which):
        return pl.BlockSpec((1, 1, ROW_BLK),
                            lambda b, h, s: (which * HEADS + h, 0, rows.scan_block(b, s, reverse)))

    def bias_spec():
        return pl.BlockSpec((1, 1, LANES), lambda b, h, s: (direction * HEADS + h, 0, 0))

    return pl.pallas_call(
        functools.partial(_mlstm_kernel, reverse),
        grid=(rows.batch, HEADS, steps),
        in_specs=[spec(S_ML_Q), spec(S_ML_K), spec(S_ML_V),
                  gate_spec(direction), gate_spec(2 + direction), bias_spec(), bias_spec()],
        out_specs=pl.BlockSpec((ROW_BLK, HEAD_W),
                               lambda b, h, s: (rows.scan_block(b, s, reverse), h)),
        out_shape=jax.ShapeDtypeStruct((n_all, SEG), BF16),
        scratch_shapes=[pltpu.VMEM((HEAD_W, 2 * HEAD_W), F32), pltpu.VMEM((1, LANES), F32)],
        compiler_params=_cp(("arbitrary", "arbitrary", "arbitrary")),
        name="mlstm_scan",
    )(p_main, p_main, p_main, gates_t, gates_t, i_bias, f_bias)


def _da_prep_kernel(q_ref, k_ref, v_ref, cos_ref, sin_ref, qg_ref, kg_ref, qt_ref, ko_ref, vt_ref):
    r_i = lax.broadcasted_iota(jnp.int32, (LANES, LANES), 0)
    c_i = lax.broadcasted_iota(jnp.int32, (LANES, LANES), 1)
    blockdiag = jnp.where((r_i // DA_QK) == (c_i // DA_QK), 1.0 / DA_QK, 0.0).astype(BF16)
    cos, sin = cos_ref[...], sin_ref[...]
    lane = lax.broadcasted_iota(jnp.int32, cos.shape, 1)
    first_half = (lane % DA_QK) < (DA_QK // 2)

    def qk_norm_rope(x, gain):
        x_hi, x_lo = _split(x * x)
        ms = _dot(x_hi, blockdiag) + _dot(x_lo, blockdiag)
        y = x * lax.rsqrt(ms + EPS) * gain
        rot = jnp.where(first_half, -pltpu.roll(y, LANES - DA_QK // 2, 1),
                        pltpu.roll(y, DA_QK // 2, 1))
        return y * cos + rot * sin

    for h in range(HEADS):
        cols = slice(h * HEAD_W, (h + 1) * HEAD_W)
        q = qk_norm_rope(q_ref[:, cols].astype(F32), qg_ref[...]) * (DA_QK ** -0.5 * LOG2E)
        qt_ref[cols, :] = q.T.astype(BF16)
        ko_ref[:, cols] = qk_norm_rope(k_ref[:, cols].astype(F32), kg_ref[...]).astype(BF16)
        vt_ref[cols, :] = v_ref[:, cols].astype(F32).T.astype(BF16)


def _da_prep(rows, p_main, cos_tab, sin_tab, q_gain, k_gain):
    n_all = p_main.shape[0]
    tm = ROW_BLK
    out_t = jax.ShapeDtypeStruct((SEG, n_all), BF16)

    def tab_idx(i):
        lat = i % rows.nlt
        ctx = rows.nlt + (i - rows.batch * rows.nlt) % rows.nct
        return (jnp.where(i < rows.batch * rows.nlt, lat, ctx), 0)

    out = jax.ShapeDtypeStruct((n_all, SEG), BF16)
    return pl.pallas_call(
        _da_prep_kernel,
        grid=(n_all // tm,),
        in_specs=[pl.BlockSpec((tm, SEG), lambda i: (i, S_DA_Q)),
                  pl.BlockSpec((tm, SEG), lambda i: (i, S_DA_K)),
                  pl.BlockSpec((tm, SEG), lambda i: (i, S_DA_V)),
                  pl.BlockSpec((tm, LANES), tab_idx),
                  pl.BlockSpec((tm, LANES), tab_idx),
                  pl.BlockSpec((1, LANES), lambda i: (0, 0)),
                  pl.BlockSpec((1, LANES), lambda i: (0, 0))],
        out_specs=[pl.BlockSpec((SEG, tm), lambda i: (0, i)),
                   pl.BlockSpec((tm, SEG), lambda i: (i, 0)),
                   pl.BlockSpec((SEG, tm), lambda i: (0, i))],
        out_shape=[out_t, out, out_t],
        compiler_params=_cp(("arbitrary",)),
        name="da_qk_prep",
    )(p_main, p_main, p_main, cos_tab, sin_tab, q_gain, k_gain)


def _attn_kernel(lam_init, n_lat_q_blocks, n_lat_keys, tk, qt_ref, kc_ref, vct_ref, kl_ref, vlt_ref,
                 lam_ref, sn_ref, o_ref, acc_ref, m_ref, l_ref, sa_ref, sb_ref):
    tq = qt_ref.shape[1]
    qt = qt_ref[...]
    chan = lax.broadcasted_iota(jnp.int32, qt.shape, 0)
    zero = jnp.zeros_like(qt)
    q2t = jnp.concatenate([jnp.where(chan < DA_QK, qt, zero), jnp.where(chan >= DA_QK, qt, zero)],
                          axis=1)

    st = _dot(kc_ref[...], q2t)
    m0 = jnp.max(st, axis=0, keepdims=True)
    p = jnp.exp2(st - m0)
    m_ref[...] = m0
    l_ref[...] = jnp.sum(p, axis=0, keepdims=True)
    acc_ref[...] = _dot(vct_ref[...], p.astype(BF16))

    n_chunks = n_lat_keys // tk
    unroll = _pick(n_chunks, (4, 2, 1))
    bufs = (sa_ref, sb_ref) if unroll > 1 else (sa_ref, sa_ref)

    def chunk(c):
        return pl.ds(pl.multiple_of(c * tk, tk), tk)

    def scores(c, buf):
        st = _dot(kl_ref[chunk(c), :], q2t)
        buf[...] = st
        return jnp.max(st, axis=0, keepdims=True)

    def absorb(c, buf, m_cur):
        m_old = m_ref[...]
        m_new = jnp.maximum(m_old, m_cur)
        p = jnp.exp2(buf[...] - m_new)
        pv = _dot(vlt_ref[:, chunk(c)], p.astype(BF16))
        alpha = jnp.exp2(m_old - m_new)
        l_ref[...] = alpha * l_ref[...] + jnp.sum(p, axis=0, keepdims=True)
        acc_ref[...] = alpha * acc_ref[...] + pv
        m_ref[...] = m_new

    def body(j, m_pend):
        for u in range(unroll):
            c = j * unroll + u
            if unroll > 1:
                m_next = scores(jnp.minimum(c + 1, n_chunks - 1), bufs[(u + 1) % 2])
                absorb(c, bufs[u % 2], m_pend)
            else:
                absorb(c, bufs[0], m_pend)
                m_next = scores(jnp.minimum(c + 1, n_chunks - 1), bufs[0])
            m_pend = m_next
        return m_pend

    n_trips = jnp.where(pl.program_id(2) < n_lat_q_blocks, n_chunks // unroll, 0)
    lax.fori_loop(0, n_trips, body, scores(0, bufs[0]))

    lv = lam_ref[...]
    lam = (jnp.exp(jnp.sum(lv[0:1] * lv[1:2], axis=-1, keepdims=True))
           - jnp.exp(jnp.sum(lv[2:3] * lv[3:4], axis=-1, keepdims=True)) + lam_init)
    o = acc_ref[...] / l_ref[...]
    a = o[:, :tq] - lam * o[:, tq:]
    ms = jnp.mean(a * a, axis=0, keepdims=True)
    y = a * lax.rsqrt(ms + EPS) * sn_ref[...] * (1.0 - lam_init)
    o_ref[...] = y.T.astype(o_ref.dtype)


def _attention(rows, qt, kh, vt, lam_vec, sub_gain, lam_init, n_rows):
    tq = ROW_BLK
    assert rows.nct == 1
    ctx_blk = lambda b: rows.batch * rows.nlt + b
    ctx_q = n_rows > rows.n_lat
    q_blk = lambda b, i: jnp.where(i < rows.nlt, b * rows.nlt + i, ctx_blk(b))
    tk = _pick(rows.seq, (512, 256))
    return pl.pallas_call(
        functools.partial(_attn_kernel, lam_init, rows.nlt, rows.seq, tk),
        grid=(rows.batch, HEADS, rows.nlt + (1 if ctx_q else 0)),
        in_specs=[pl.BlockSpec((HEAD_W, tq), lambda b, h, i: (h, q_blk(b, i))),
                  pl.BlockSpec((ROW_BLK, HEAD_W), lambda b, h, i: (ctx_blk(b), h)),
                  pl.BlockSpec((HEAD_W, ROW_BLK), lambda b, h, i: (h, ctx_blk(b))),
                  pl.BlockSpec((rows.seq, HEAD_W), lambda b, h, i: (b, h)),
                  pl.BlockSpec((HEAD_W, rows.seq), lambda b, h, i: (h, b)),
                  pl.BlockSpec(lam_vec.shape, lambda b, h, i: (0, 0)),
                  pl.BlockSpec((HEAD_W, 1), lambda b, h, i: (0, 0))],
        out_specs=pl.BlockSpec((tq, HEAD_W), lambda b, h, i: (q_blk(b, i), h)),
        out_shape=jax.ShapeDtypeStruct((n_rows, SEG), BF16),
        scratch_shapes=[pltpu.VMEM((HEAD_W, 2 * tq), F32), pltpu.VMEM((1, 2 * tq), F32),
                        pltpu.VMEM((1, 2 * tq), F32), pltpu.VMEM((tk, 2 * tq), F32),
                        pltpu.VMEM((tk, 2 * tq), F32)],
        compiler_params=_cp(("arbitrary", "arbitrary", "arbitrary")),
        name="diff_attention",
    )(qt, kh, vt, kh, vt, lam_vec, sub_gain)


def _merge_kernel(ohf_ref, ohb_ref, hg_ref, da_ref, omf_ref, omb_ref, mo_ref, g0_ref, g1_ref,
                  g2_ref, hn_ref, mn_ref, wb_ref, o_ref, h_ref):
    @pl.when(pl.program_id(1) == 0)
    def _():
        for h in range(HEADS):
            cols = slice(h * HEAD_W, (h + 1) * HEAD_W)
            o = ohf_ref[:, cols].astype(F32) + ohb_ref[:, cols].astype(F32)
            y = o * lax.rsqrt(jnp.mean(o * o, axis=-1, keepdims=True) + EPS) * hn_ref[...]
            g = hg_ref[:, cols].astype(F32)
            h_ref[0, :, cols] = (y * (g * _sigmoid(g))).astype(BF16)
            o = omf_ref[:, cols].astype(F32) + omb_ref[:, cols].astype(F32)
            y = o * lax.rsqrt(jnp.mean(o * o, axis=-1, keepdims=True) + EPS) * mn_ref[...]
            h_ref[2, :, cols] = (y * _sigmoid(mo_ref[:, cols].astype(F32))).astype(BF16)
        h_ref[1] = da_ref[...]

    y = _sigmoid(g0_ref[...].astype(F32)) * _dot(h_ref[0], wb_ref[0])
    y = y + _sigmoid(g1_ref[...].astype(F32)) * _dot(h_ref[1], wb_ref[1])
    y = y + _sigmoid(g2_ref[...].astype(F32)) * _dot(h_ref[2], wb_ref[2])
    o_ref[...] = y.astype(o_ref.dtype)


def _merge(p_main, ohf, ohb, da, omf, omb, hg_gain, ml_gain, w_branch, d):
    n_all = da.shape[0]
    tm = 512
    tn = _pick(d, (512, 256, 128))
    gate_col0 = N_SEG * SEG // tn
    row = lambda i, j: (i, 0)

    def gate_spec(jj):
        return pl.BlockSpec((tm, tn), lambda i, j: (i, gate_col0 + jj * (d // tn) + j))

    return pl.pallas_call(
        _merge_kernel,
        grid=(n_all // tm, d // tn),
        in_specs=[pl.BlockSpec((tm, SEG), row), pl.BlockSpec((tm, SEG), row),
                  pl.BlockSpec((tm, SEG), lambda i, j: (i, S_HG_G)),
                  pl.BlockSpec((tm, SEG), row),
                  pl.BlockSpec((tm, SEG), row), pl.BlockSpec((tm, SEG), row),
                  pl.BlockSpec((tm, SEG), lambda i, j: (i, S_ML_O)),
                  gate_spec(0), gate_spec(1), gate_spec(2),
                  pl.BlockSpec((1, LANES), lambda i, j: (0, 0)),
                  pl.BlockSpec((1, LANES), lambda i, j: (0, 0)),
                  pl.BlockSpec((3, SEG, tn), lambda i, j: (0, 0, j))],
        out_specs=pl.BlockSpec((tm, tn), lambda i, j: (i, j)),
        out_shape=jax.ShapeDtypeStruct((n_all, d), BF16),
        scratch_shapes=[pltpu.VMEM((3, tm, SEG), BF16)],
        compiler_params=_cp(("arbitrary", "arbitrary")),
        name="branch_merge",
    )(ohf, ohb, p_main, da, omf, omb, p_main, p_main, p_main, p_main, hg_gain, ml_gain, w_branch)


def _proj_resid_kernel(a_ref, w_ref, x_ref, g_ref, o_ref):
    o_ref[...] = x_ref[...] + g_ref[0] * _dot(a_ref[...], w_ref[...])


def _proj_resid(rows, a, w, x_all, modsflat, layer, which_gate):
    d = x_all.shape[1]
    n_all, k = a.shape
    tm = 512
    tn = _pick(d, (1024, 512, 256, 128))
    return pl.pallas_call(
        _proj_resid_kernel,
        grid=(d // tn, n_all // tm),
        in_specs=[pl.BlockSpec((tm, k), lambda j, i: (i, 0)),
                  pl.BlockSpec((k, tn), lambda j, i: (0, j)),
                  pl.BlockSpec((tm, tn), lambda j, i: (i, j)),
                  pl.BlockSpec((1, 1, tn),
                               lambda j, i: (rows.mod_row(layer, i, tm, which_gate), 0, j))],
        out_specs=pl.BlockSpec((tm, tn), lambda j, i: (i, j)),
        out_shape=jax.ShapeDtypeStruct((n_all, d), F32),
        compiler_params=_cp(("arbitrary", "arbitrary")),
        name="out_proj_residual",
    )(a, w, x_all, modsflat)


def _shared_expert_kernel(tok_ref, w1_ref, w3_ref, w2_ref, o_ref):
    tok = tok_ref[...]
    h1 = _dot(tok, w1_ref[...])
    h3 = _dot(tok, w3_ref[...])
    a = (h1 * _sigmoid(h1)) * h3
    o_ref[...] = _dot(a.astype(BF16), w2_ref[...]).astype(o_ref.dtype)


def _shared_expert(tok, w1, w3, w2):
    n_all, d = tok.shape
    de = w1.shape[1]
    tm = 512
    full = lambda i: (0, 0)
    return pl.pallas_call(
        _shared_expert_kernel,
        grid=(n_all // tm,),
        in_specs=[pl.BlockSpec((tm, d), lambda i: (i, 0)),
                  pl.BlockSpec((d, de), full), pl.BlockSpec((d, de), full),
                  pl.BlockSpec((de, d), full)],
        out_specs=pl.BlockSpec((tm, d), lambda i: (i, 0)),
        out_shape=jax.ShapeDtypeStruct((n_all, d), BF16),
        compiler_params=_cp(("arbitrary",)),
        name="moe_shared_expert",
    )(tok, w1, w3, w2)


def _moe_plan_kernel(route_ref, pos_ref, cnt_ref, carry_ref):
    @pl.when(pl.program_id(0) == 0)
    def _():
        carry_ref[...] = jnp.zeros_like(carry_ref)

    r = route_ref[...]
    tm = r.shape[0]
    lane = lax.broadcasted_iota(jnp.int32, r.shape, 1).astype(F32)
    mask = jnp.zeros_like(r)
    for k in range(TOP_K):
        mask = mask + jnp.where(lane == r[:, k:k + 1], 1.0, 0.0)
    r_i = lax.broadcasted_iota(jnp.int32, (tm, tm), 0)
    c_i = lax.broadcasted_iota(jnp.int32, (tm, tm), 1)
    before = jnp.where(c_i < r_i, 1.0, 0.0).astype(BF16)
    rank = _dot(before, mask.astype(BF16)) + carry_ref[...]
    out = jnp.zeros_like(r)
    for k in range(TOP_K):
        pk = jnp.sum(jnp.where(lane == r[:, k:k + 1], rank, 0.0), axis=-1, keepdims=True)
        out = jnp.where(lane == k, pk, out)
    pos_ref[...] = out
    carry = carry_ref[...] + jnp.sum(mask, axis=0, keepdims=True)
    carry_ref[...] = carry
    cnt_ref[...] = jnp.broadcast_to(carry, cnt_ref.shape)


def _moe_plan(route):
    n = route.shape[0]
    tm = ROW_BLK
    return pl.pallas_call(
        _moe_plan_kernel,
        grid=(n // tm,),
        in_specs=[pl.BlockSpec((tm, LANES), lambda i: (i, 0))],
        out_specs=[pl.BlockSpec((tm, LANES), lambda i: (i, 0)),
                   pl.BlockSpec((8, LANES), lambda i: (0, 0))],
        out_shape=[jax.ShapeDtypeStruct((n, LANES), F32), jax.ShapeDtypeStruct((8, LANES), F32)],
        scratch_shapes=[pltpu.VMEM((1, LANES), F32)],
        compiler_params=_cp(("arbitrary",)),
        name="moe_plan",
    )(route)


def _moe_routed_kernel(ns, te_ref, na_ref, st_ref, tok_hbm, sw_ref, w1_ref, w3_ref, w2_ref, o_ref,
                       xbuf, x_ref, sem):
    tm = MOE_TILE
    t = pl.program_id(0)
    n_active = na_ref[0]
    par = t % 2

    def gather_copy(tile, buf, r):
        tok = st_ref[tile * tm + r]
        return pltpu.make_async_copy(
            tok_hbm.at[pl.ds(pl.multiple_of(tok * ns, ns), ns), :],
            xbuf.at[buf, pl.ds(pl.multiple_of(r * ns, ns), ns), :], sem.at[buf])

    def for_each_copy(tile, buf, act):
        def body(r4, carry):
            for u in range(4):
                act(gather_copy(tile, buf, r4 * 4 + u))
            return carry
        lax.fori_loop(0, tm // 4, body, 0)

    @pl.when(t == 0)
    def _():
        for_each_copy(0, 0, lambda cp: cp.start())

    @pl.when(t + 1 < n_active)
    def _():
        for_each_copy(t + 1, 1 - par, lambda cp: cp.start())

    @pl.when(t < n_active)
    def _():
        for_each_copy(t, par, lambda cp: cp.wait())
        d = x_ref.shape[1]

        def emit(s, lo, hi):
            x_ref[:, s * LANES:(s + 1) * LANES] = lo.astype(BF16)
            x_ref[:, d // 2 + s * LANES:d // 2 + (s + 1) * LANES] = hi.astype(BF16)
        _load_slabs(xbuf.at[par], 0, tm, ns, emit)
        x = x_ref[...]
        h1 = _dot(x, w1_ref[...])
        h3 = _dot(x, w3_ref[...])
        a = (h1 * _sigmoid(h1)) * h3 * sw_ref[...]
        _store_slabs(_dot(a.astype(BF16), w2_ref[...]), o_ref)

    @pl.when(t >= n_active)
    def _():
        o_ref[...] = jnp.zeros_like(o_ref)


def _moe_routed(tok_slabs, slot_token, slot_w, tile_expert, n_active, w1, w3, w2):
    n_exp, d, de = w1.shape
    ns = d // (2 * LANES)
    tm = MOE_TILE
    n_slots = slot_token.shape[0]
    nt = n_slots // tm
    grid_spec = pltpu.PrefetchScalarGridSpec(
        num_scalar_prefetch=3,
        grid=(nt,),
        in_specs=[pl.BlockSpec(memory_space=pl.ANY),
                  pl.BlockSpec((tm, 1), lambda t, te, na, st: (t, 0)),
                  pl.BlockSpec((None, d, de), lambda t, te, na, st: (te[t], 0, 0)),
                  pl.BlockSpec((None, d, de), lambda t, te, na, st: (te[t], 0, 0)),
                  pl.BlockSpec((None, de, d), lambda t, te, na, st: (te[t], 0, 0))],
        out_specs=pl.BlockSpec((tm * ns, LANES), lambda t, te, na, st: (t, 0)),
        scratch_shapes=[pltpu.VMEM((2, tm * ns, LANES), jnp.int32), pltpu.VMEM((tm, d), BF16),
                        pltpu.SemaphoreType.DMA((2,))],
    )
    return pl.pallas_call(
        functools.partial(_moe_routed_kernel, ns),
        grid_spec=grid_spec,
        out_shape=jax.ShapeDtypeStruct((n_slots * ns, LANES), jnp.int32),
        compiler_params=_cp(("arbitrary",)),
        name="moe_routed_experts",
    )(tile_expert, n_active, slot_token, tok_slabs, slot_w, w1, w3, w2)


def _moe_combine_kernel(ns, s4_ref, ys_hbm, ysh_ref, x_ref, g_ref, o_ref, ybuf, sem):
    tm = x_ref.shape[0]
    d = x_ref.shape[1]
    i = pl.program_id(0)
    par = i % 2

    def gather_copy(tile, buf, row, k):
        slot = s4_ref[(tile * tm + row) * TOP_K + k]
        return pltpu.make_async_copy(
            ys_hbm.at[pl.ds(pl.multiple_of(slot * ns, ns), ns), :],
            ybuf.at[buf, pl.ds(pl.multiple_of((k * tm + row) * ns, ns), ns), :], sem.at[buf])

    def for_each_copy(tile, buf, act):
        def body(row, carry):
            for k in range(TOP_K):
                act(gather_copy(tile, buf, row, k))
            return carry
        lax.fori_loop(0, tm, body, 0)

    @pl.when(i == 0)
    def _():
        for_each_copy(0, 0, lambda cp: cp.start())

    @pl.when(i + 1 < pl.num_programs(0))
    def _():
        for_each_copy(i + 1, 1 - par, lambda cp: cp.start())

    for_each_copy(i, par, lambda cp: cp.wait())

    gate = g_ref[0]
    for s in range(ns):
        lo_cols = slice(s * LANES, (s + 1) * LANES)
        hi_cols = slice(d // 2 + s * LANES, d // 2 + (s + 1) * LANES)
        y_lo = ysh_ref[:, lo_cols].astype(F32)
        y_hi = ysh_ref[:, hi_cols].astype(F32)
        for k in range(TOP_K):
            lo, hi = _unpack_pair(ybuf[par, pl.ds(k * tm * ns + s, tm, stride=ns), :])
            y_lo = y_lo + lo
            y_hi = y_hi + hi
        o_ref[:, lo_cols] = x_ref[:, lo_cols] + gate[:, lo_cols] * y_lo
        o_ref[:, hi_cols] = x_ref[:, hi_cols] + gate[:, hi_cols] * y_hi


def _moe_combine(rows, x_all, y_shared, ys_slabs, slot4, modsflat, layer, which_gate, n_rows):
    d = x_all.shape[1]
    ns = d // (2 * LANES)
    tm = 128
    grid_spec = pltpu.PrefetchScalarGridSpec(
        num_scalar_prefetch=1,
        grid=(n_rows // tm,),
        in_specs=[pl.BlockSpec(memory_space=pl.ANY),
                  pl.BlockSpec((tm, d), lambda i, s4: (i, 0)),
                  pl.BlockSpec((tm, d), lambda i, s4: (i, 0)),
                  pl.BlockSpec((1, 1, d),
                               lambda i, s4: (rows.mod_row(layer, i, tm, which_gate), 0, 0))],
        out_specs=pl.BlockSpec((tm, d), lambda i, s4: (i, 0)),
        scratch_shapes=[pltpu.VMEM((2, TOP_K * tm * ns, LANES), jnp.int32),
                        pltpu.SemaphoreType.DMA((2,))],
    )
    return pl.pallas_call(
        functools.partial(_moe_combine_kernel, ns),
        grid_spec=grid_spec,
        out_shape=jax.ShapeDtypeStruct((n_rows, d), F32),
        compiler_params=_cp(("arbitrary",)),
        name="moe_combine_residual",
    )(slot4, ys_slabs, y_shared, x_all, modsflat)


def _moe_slots(route, pos, counts, n_exp):
    n = route.shape[0]
    tm = MOE_TILE
    idx4 = route[:, :TOP_K].astype(jnp.int32)
    w4 = route[:, TOP_K:2 * TOP_K]
    pos4 = pos[:, :TOP_K].astype(jnp.int32)
    cnt = counts[0, :n_exp].astype(jnp.int32)
    padded = (cnt + tm - 1) // tm * tm
    ends = jnp.cumsum(padded)
    starts = ends - padded
    slot4 = (starts[idx4] + pos4).reshape(-1)
    n_slots = n * TOP_K + n_exp * tm
    nt = n_slots // tm
    tile_expert = jnp.minimum(
        jnp.searchsorted(ends, jnp.arange(nt, dtype=jnp.int32) * tm, side='right'),
        n_exp - 1).astype(jnp.int32)
    n_active = (ends[-1:] // tm).astype(jnp.int32)
    token_of = jnp.repeat(jnp.arange(n, dtype=jnp.int32), TOP_K)
    slot_token = jnp.zeros((n_slots,), jnp.int32).at[slot4].set(token_of)
    slot_w = jnp.zeros((n_slots,), F32).at[slot4].set(w4.reshape(-1)).reshape(n_slots, 1)
    return slot4, slot_token, slot_w, tile_expert, n_active


def _rope_tables(seq, ctx_len):
    n = DA_QK // 4
    t = jnp.arange(seq)
    inv = ROPE_THETA ** (-jnp.arange(n, dtype=F32) / n)
    row = (t // GRID_W).astype(F32)
    col = (t % GRID_W).astype(F32)
    ang = jnp.concatenate([row[:, None] * inv, col[:, None] * inv], axis=-1)
    ang = jnp.concatenate([ang, jnp.zeros((ctx_len, 2 * n), F32)], axis=0)
    return jnp.tile(jnp.cos(ang), (1, 4)), jnp.tile(jnp.sin(ang), (1, 4))


def kernel(x, c, ctx, c_ctx, norm1, norm2, w_ada, b_ada, w_in, hg_lb_logits, hg_norm, da_q_norm,
           da_k_norm, da_lambda, da_sub_norm, ml_igate_bias, ml_fgate_bias, ml_norm, w_branch,
           w_out, router_w, router_bias, exp_w1, exp_w3, exp_w2, sh_w1, sh_w3, sh_w2):
    batch, seq, d = x.shape
    ctx_len = ctx.shape[1]
    depth = w_ada.shape[0]
    n_exp = router_w.shape[-1]
    rows = _Rows(batch, seq, ctx_len)
    assert batch + 1 <= 16 and n_exp <= LANES

    x_all = jnp.concatenate([x.reshape(batch * seq, d), ctx.reshape(batch * ctx_len, d)], axis=0)
    cvec = jnp.zeros((16, d), F32).at[0].set(c_ctx).at[1:1 + batch].set(c)
    mods = _mods(cvec, w_ada, b_ada)
    modsflat = mods[:, :rows.n_groups].reshape(depth * rows.n_groups * 6, 1, d)
    cos_tab, sin_tab = _rope_tables(seq, ctx_len)
    gate_lo = N_SEG * SEG

    for l in range(depth):
        n_rows = rows.n_all if l < depth - 1 else rows.n_lat
        lam_init = 0.8 - 0.6 * math.exp(-0.3 * l)
        w_main = jnp.concatenate([w_in[l, :, :gate_lo], w_in[l, :, gate_lo + N_GATE_COLS:]],
                                 axis=1).astype(BF16)
        w_gate = jnp.pad(w_in[l, :, gate_lo:gate_lo + N_GATE_COLS],
                         ((0, 0), (0, LANES - N_GATE_COLS))).astype(BF16)

        hx = _norm(rows, x_all, rows.n_all, norm1[l], modsflat, l, 1, 0)
        p_main = _matmul(hx, w_main, BF16)
        p_gate = _matmul(hx, w_gate, F32)
        gates_t = p_gate[:, :N_GATE_COLS].T.reshape(N_GATE_COLS, 1, rows.n_all)

        ohf = _hgrn_scan(rows, p_main, hg_lb_logits, l, 0)
        ohb = _hgrn_scan(rows, p_main, hg_lb_logits, l, 1)

        tile2 = lambda g: jnp.tile(g.reshape(1, DA_QK), (1, 2))
        qt, kh, vt = _da_prep(rows, p_main, cos_tab, sin_tab, tile2(da_q_norm[l]),
                              tile2(da_k_norm[l]))
        da = _attention(rows, qt, kh, vt, da_lambda[l], da_sub_norm[l].reshape(HEAD_W, 1),
                        lam_init, n_rows)

        bias = lambda bv: jnp.broadcast_to(bv.reshape(2 * HEADS, 1, 1), (2 * HEADS, 1, LANES))
        i_b, f_b = bias(ml_igate_bias[l]), bias(ml_fgate_bias[l])
        omf = _mlstm_scan(rows, p_main, gates_t, i_b, f_b, 0)
        omb = _mlstm_scan(rows, p_main, gates_t, i_b, f_b, 1)

        ymid = _merge(p_main, ohf, ohb, da, omf, omb, hg_norm[l].reshape(1, HEAD_W),
                      ml_norm[l].reshape(1, HEAD_W), w_branch[l].astype(BF16), d)
        x_all = _proj_resid(rows, ymid, w_out[l].astype(BF16), x_all, modsflat, l, 2)

        rw = jnp.pad(router_w[l], ((0, 0), (0, LANES - n_exp)))
        rw_hi = rw.astype(BF16)
        rw_lo = (rw - rw_hi.astype(F32)).astype(BF16)
        rb = jnp.pad(router_bias[l], (0, LANES - n_exp)).reshape(1, LANES)
        tok, tok_slabs, route = _norm(rows, x_all, n_rows, norm2[l], modsflat, l, 4, 3,
                                      router=(rw_hi, rw_lo, rb, n_exp))
        pos, counts = _moe_plan(route)
        slot4, slot_token, slot_w, tile_expert, n_active = _moe_slots(route, pos, counts, n_exp)
        ys = _moe_routed(tok_slabs, slot_token, slot_w, tile_expert, n_active,
                         exp_w1[l].astype(BF16), exp_w3[l].astype(BF16), exp_w2[l].astype(BF16))
        y_sh = _shared_expert(tok, sh_w1[l].astype(BF16), sh_w3[l].astype(BF16),
                              sh_w2[l].astype(BF16))
        x_all = _moe_combine(rows, x_all, y_sh, ys, slot4, modsflat, l, 5, n_rows)

    return x_all.reshape(batch, seq, d)
```

```python
import functools
import math

import jax
import jax.numpy as jnp
from jax import lax
from jax.experimental import pallas as pl
from jax.experimental.pallas import tpu as pltpu

F32 = jnp.float32
BF16 = jnp.bfloat16

EPS = 1e-6
NEG_BIG = -1e30
TINY = 1e-30
GRID_W = 64
ROPE_THETA = 10000.0
ROUTED_SCALE = 2.5
TOP_K = 4
LOG2E = 1.4426950408889634

HEADS = 12
HEAD_W = 128
SEG = HEADS * HEAD_W
DA_QK = 64
N_GATE_COLS = 4 * HEADS

S_HG_Q, S_HG_FF, S_HG_FB, S_HG_I, S_HG_G = 0, 1, 2, 3, 4
S_DA_Q, S_DA_K, S_DA_V = 5, 6, 7
S_ML_Q, S_ML_K, S_ML_V, S_ML_O = 8, 9, 10, 11
N_SEG = 12

LANES = 128
ROW_BLK = 256
HG_CHUNK = 64
HG_SUB = 16
HG_FAST_SUB = 32
HG_SPREAD_MAX = 60.0
MOE_TILE = 256
VMEM_LIMIT = 56 * 1024 * 1024


def _cp(sem, vmem=VMEM_LIMIT):
    return pltpu.CompilerParams(dimension_semantics=sem, vmem_limit_bytes=vmem)


def _pick(n, cands):
    for c in cands:
        if n % c == 0:
            return c
    raise ValueError(f"no tile for {n} in {cands}")


def _dot(a, b):
    return jnp.dot(a, b, preferred_element_type=F32)


def _dot_nt(a, b):
    return lax.dot_general(a, b, (((1,), (1,)), ((), ())), preferred_element_type=F32)


def _dot_tn(a, b):
    return lax.dot_general(a, b, (((0,), (0,)), ((), ())), preferred_element_type=F32)


def _split(x):
    hi = x.astype(BF16)
    lo = (x - hi.astype(F32)).astype(BF16)
    return hi, lo


def _sigmoid(x):
    return 1.0 / (1.0 + jnp.exp(-x))


class _Rows:
    def __init__(self, batch, seq, ctx_len):
        self.batch, self.seq, self.ctx = batch, seq, ctx_len
        assert seq % ROW_BLK == 0 and ctx_len % ROW_BLK == 0
        self.nlt = seq // ROW_BLK
        self.nct = ctx_len // ROW_BLK
        self.n_lat = batch * seq
        self.n_all = batch * (seq + ctx_len)
        self.n_groups = batch + 1

    def group(self, i, tm):
        lat_tiles = self.n_lat // tm
        return jnp.where(i < lat_tiles, 1 + i // (self.seq // tm), 0)

    def mod_row(self, layer, i, tm, which):
        return (layer * self.n_groups + self.group(i, tm)) * 6 + which

    def scan_block(self, b, s, reverse):
        if reverse:
            cblk = self.batch * self.nlt + b * self.nct + (self.nct - 1 - s)
            lblk = b * self.nlt + (self.nlt - 1 - (s - self.nct))
        else:
            cblk = self.batch * self.nlt + b * self.nct + s
            lblk = b * self.nlt + (s - self.nct)
        return jnp.where(s < self.nct, cblk, lblk)


def _mods_kernel(c_ref, w_ref, b_ref, o_ref):
    c = c_ref[...]
    s_hi, s_lo = _split(c * _sigmoid(c))
    w_hi, w_lo = _split(w_ref[...])
    acc = _dot(s_hi, w_hi) + _dot(s_lo, w_hi) + _dot(s_hi, w_lo)
    o_ref[...] = acc + b_ref[...]


def _mods(cvec, w_ada, b_ada):
    depth, d, n = w_ada.shape
    tn = _pick(n, (512, 256, 128))
    rows = cvec.shape[0]
    return pl.pallas_call(
        _mods_kernel,
        grid=(depth, n // tn),
        in_specs=[pl.BlockSpec((rows, d), lambda l, j: (0, 0)),
                  pl.BlockSpec((None, d, tn), lambda l, j: (l, 0, j)),
                  pl.BlockSpec((None, 1, tn), lambda l, j: (l, 0, j))],
        out_specs=pl.BlockSpec((None, rows, tn), lambda l, j: (l, 0, j)),
        out_shape=jax.ShapeDtypeStruct((depth, rows, n), F32),
        compiler_params=_cp(("arbitrary", "arbitrary")),
        name="adaln_mods",
    )(cvec, w_ada, b_ada.reshape(depth, 1, n))


def _norm_mod(x_ref, g_ref, sc_ref, sh_ref):
    x = x_ref[...]
    ms = jnp.mean(x * x, axis=-1, keepdims=True)
    y = x * lax.rsqrt(ms + EPS) * g_ref[...]
    return y * (1.0 + sc_ref[0]) + sh_ref[0]


def _norm_kernel(x_ref, g_ref, sc_ref, sh_ref, o_ref):
    o_ref[...] = _norm_mod(x_ref, g_ref, sc_ref, sh_ref).astype(BF16)


def _pack_pair(lo, hi):
    lo_b = pltpu.bitcast(lo.astype(BF16).astype(F32), jnp.int32)
    hi_b = pltpu.bitcast(hi.astype(BF16).astype(F32), jnp.int32)
    return (hi_b & jnp.int32(-65536)) | lax.shift_right_logical(lo_b, 16)


def _unpack_pair(w):
    lo = pltpu.bitcast(lax.shift_left(w, 16), F32)
    hi = pltpu.bitcast(w & jnp.int32(-65536), F32)
    return lo, hi


def _store_slabs(y, o_ref):
    tm, d = y.shape
    ns = d // (2 * LANES)
    for s in range(ns):
        lo = y[:, s * LANES:(s + 1) * LANES]
        hi = y[:, d // 2 + s * LANES:d // 2 + (s + 1) * LANES]
        o_ref[pl.ds(s, tm, stride=ns), :] = _pack_pair(lo, hi)


def _load_slabs(src_ref, row0, tm, ns, emit):
    for s in range(ns):
        lo, hi = _unpack_pair(src_ref[pl.ds(row0 + s, tm, stride=ns), :])
        emit(s, lo, hi)


def _norm_router_kernel(n_exp, x_ref, g_ref, sc_ref, sh_ref, rwh_ref, rwl_ref, rb_ref,
                        o_ref, slab_ref, route_ref):
    y = _norm_mod(x_ref, g_ref, sc_ref, sh_ref)
    o_ref[...] = y.astype(BF16)
    _store_slabs(y, slab_ref)
    y_hi, y_lo = _split(y)
    rwh = rwh_ref[...]
    logits = _dot(y_hi, rwh) + _dot(y_lo, rwh) + _dot(y_hi, rwl_ref[...])
    scores = _sigmoid(logits)
    lane = lax.broadcasted_iota(jnp.int32, scores.shape, 1).astype(F32)
    work = jnp.where(lane < n_exp, scores + rb_ref[...], -jnp.inf)
    route = jnp.zeros_like(scores)
    total = jnp.zeros_like(scores[:, :1])
    for k in range(TOP_K):
        mx = jnp.max(work, axis=-1, keepdims=True)
        first = jnp.min(jnp.where(work == mx, lane, float(LANES)), axis=-1, keepdims=True)
        hit = lane == first
        sc = jnp.sum(jnp.where(hit, scores, 0.0), axis=-1, keepdims=True)
        total = total + sc
        route = jnp.where(lane == k, first, route)
        route = jnp.where(lane == TOP_K + k, sc, route)
        work = jnp.where(hit, -jnp.inf, work)
    is_w = (lane >= TOP_K) & (lane < 2 * TOP_K)
    route_ref[...] = jnp.where(is_w, route / total * ROUTED_SCALE, route)


def _norm(rows, x_all, n_all, gain, modsflat, layer, which_scale, which_shift, router=None):
    d = x_all.shape[1]
    tm = ROW_BLK

    def mod_idx(which):
        return lambda i: (rows.mod_row(layer, i, tm, which), 0, 0)

    in_specs = [pl.BlockSpec((tm, d), lambda i: (i, 0)),
                pl.BlockSpec((1, d), lambda i: (0, 0)),
                pl.BlockSpec((1, 1, d), mod_idx(which_scale)),
                pl.BlockSpec((1, 1, d), mod_idx(which_shift))]
    args = [x_all, gain.reshape(1, d), modsflat, modsflat]
    out_spec = pl.BlockSpec((tm, d), lambda i: (i, 0))
    out_shape = jax.ShapeDtypeStruct((n_all, d), BF16)
    if router is None:
        return pl.pallas_call(
            _norm_kernel, grid=(n_all // tm,), in_specs=in_specs, out_specs=out_spec,
            out_shape=out_shape, compiler_params=_cp(("arbitrary",)), name="prenorm",
        )(*args)
    rw_hi, rw_lo, rbias, n_exp = router
    ns = d // (2 * LANES)
    in_specs += [pl.BlockSpec((d, LANES), lambda i: (0, 0)),
                 pl.BlockSpec((d, LANES), lambda i: (0, 0)),
                 pl.BlockSpec((1, LANES), lambda i: (0, 0))]
    return pl.pallas_call(
        functools.partial(_norm_router_kernel, n_exp),
        grid=(n_all // tm,), in_specs=in_specs,
        out_specs=[out_spec, pl.BlockSpec((tm * ns, LANES), lambda i: (i, 0)),
                   pl.BlockSpec((tm, LANES), lambda i: (i, 0))],
        out_shape=[out_shape, jax.ShapeDtypeStruct((n_all * ns, LANES), jnp.int32),
                   jax.ShapeDtypeStruct((n_all, LANES), F32)],
        compiler_params=_cp(("arbitrary",)), name="prenorm_router",
    )(*args, rw_hi, rw_lo, rbias)


def _mm_kernel(a_ref, w_ref, o_ref):
    o_ref[...] = _dot(a_ref[...], w_ref[...]).astype(o_ref.dtype)


def _matmul(a, w, out_dtype, tm=512):
    m, k = a.shape
    n = w.shape[1]
    tn = _pick(n, (1024, 512, 256, 128))
    return pl.pallas_call(
        _mm_kernel,
        grid=(n // tn, m // tm),
        in_specs=[pl.BlockSpec((tm, k), lambda j, i: (i, 0)),
                  pl.BlockSpec((k, tn), lambda j, i: (0, j))],
        out_specs=pl.BlockSpec((tm, tn), lambda j, i: (i, j)),
        out_shape=jax.ShapeDtypeStruct((m, n), out_dtype),
        compiler_params=_cp(("arbitrary", "arbitrary")),
        name="in_proj",
    )(a, w)


def _hgrn_exact_chunk(q, k, bcum, v, st, reverse):
    c = HG_CHUNK
    last = 0 if reverse else c - 1
    b_last = bcum[last:last + 1]
    o_inter = _dot_nt((q * jnp.exp(bcum)).astype(BF16), st.astype(BF16))
    kd = k * jnp.exp(b_last - bcum)
    st_new = st * jnp.exp(b_last) + _dot_tn(v.astype(BF16), kd.astype(BF16))

    ones = jnp.ones((LANES, LANES), BF16)
    sub_row = lax.broadcasted_iota(jnp.int32, (HG_SUB, LANES), 0)
    v16 = v.astype(BF16)
    outs = []
    for i in range(c // HG_SUB):
        r0 = i * HG_SUB
        bsub = bcum[r0:r0 + HG_SUB]
        qsub = q[r0:r0 + HG_SUB]
        zs = []
        for s in range(HG_SUB):
            keep = (sub_row <= s) if reverse else (sub_row >= s)
            dl = jnp.where(keep, bsub - bcum[r0 + s:r0 + s + 1], NEG_BIG)
            zs.append(qsub * (k[r0 + s:r0 + s + 1] * jnp.exp(dl)))
        red = _dot(jnp.concatenate(zs, axis=0).astype(BF16), ones)
        o_sub = red[0:HG_SUB] * v[r0:r0 + 1]
        for s in range(1, HG_SUB):
            o_sub = o_sub + red[s * HG_SUB:(s + 1) * HG_SUB] * v[r0 + s:r0 + s + 1]
        lo, hi = (r0 + HG_SUB, c) if reverse else (0, r0)
        if hi > lo:
            ref_row = bcum[lo:lo + 1] if reverse else bcum[hi - 1:hi]
            qi = (qsub * jnp.exp(bsub - ref_row)).astype(BF16)
            kt = (k[lo:hi] * jnp.exp(ref_row - bcum[lo:hi])).astype(BF16)
            o_sub = o_sub + _dot(_dot_nt(qi, kt).astype(BF16), v16[lo:hi])
        outs.append(o_sub)
    return o_inter + jnp.concatenate(outs, axis=0), st_new


def _hgrn_fast_block(q_s, k_s, b_s, v_s, o_ref, st_ref, reverse):
    c, sub = HG_CHUNK, HG_FAST_SUB
    n_chunk = ROW_BLK // c
    order = list(range(n_chunk - 1, -1, -1) if reverse else range(n_chunk))
    last = 0 if reverse else c - 1
    qs, xs, eb, scores, vals = {}, {}, {}, {}, {}
    for ci in order:
        rows = pl.ds(ci * c, c)
        q, k, bcum, v = q_s[rows, :], k_s[rows, :], b_s[rows, :], v_s[rows, :]
        v16 = v.astype(BF16)
        b_last = bcum[last:last + 1]
        qs[ci] = (q * jnp.exp(bcum)).astype(BF16)
        eb[ci] = jnp.exp(b_last)
        xs[ci] = _dot_tn(v16, (k * jnp.exp(b_last - bcum)).astype(BF16))
        for i in range(c // sub):
            r0 = i * sub
            lo, hi = (r0, c) if reverse else (0, r0 + sub)
            ref_row = bcum[r0 + sub - 1:r0 + sub] if reverse else bcum[r0:r0 + 1]
            qi = (q[r0:r0 + sub] * jnp.exp(bcum[r0:r0 + sub] - ref_row)).astype(BF16)
            kt = (k[lo:hi] * jnp.exp(ref_row - bcum[lo:hi])).astype(BF16)
            scores[ci, i] = _dot_nt(qi, kt)
            vals[ci, i] = v16[lo:hi]
    outs = {}
    for ci in order:
        parts = []
        for i in range(c // sub):
            r0 = i * sub
            lo, hi = (r0, c) if reverse else (0, r0 + sub)
            key = lax.broadcasted_iota(jnp.int32, (sub, hi - lo), 1) + lo
            qry = lax.broadcasted_iota(jnp.int32, (sub, hi - lo), 0) + r0
            a = jnp.where((key >= qry) if reverse else (key <= qry), scores[ci, i], 0.0)
            parts.append(_dot(a.astype(BF16), vals[ci, i]))
        outs[ci] = jnp.concatenate(parts, axis=0)
    st = st_ref[...]
    for ci in order:
        o = outs[ci] + _dot_nt(qs[ci], st.astype(BF16))
        o_ref[pl.ds(ci * c, c), :] = o.astype(o_ref.dtype)
        st = st * eb[ci] + xs[ci]
    st_ref[...] = st


def _hgrn_kernel(layer, reverse, q_ref, f_ref, v_ref, lb_ref, o_ref, st_ref, q_s, k_s, b_s, v_s):
    @pl.when(pl.program_id(2) == 0)
    def _():
        st_ref[...] = jnp.zeros_like(st_ref)

    lg = lb_ref[0]
    e = jnp.exp(lg - jnp.max(lg, axis=0, keepdims=True))
    p = e / jnp.sum(e, axis=0, keepdims=True)
    lb = jnp.zeros((1, LANES), F32)
    for i in range(1, layer + 1):
        lb = lb + p[i:i + 1]

    qp = q_ref[...].astype(F32)
    fp = f_ref[...].astype(F32)
    q_s[...] = qp * _sigmoid(qp) * HEAD_W ** -0.5
    k_s[...] = (1.0 - lb) * _sigmoid(-fp)
    v_s[...] = v_ref[...].astype(F32)
    lf = jnp.log(jnp.maximum(lb + (1.0 - lb) * _sigmoid(fp), TINY))
    r_i = lax.broadcasted_iota(jnp.int32, (ROW_BLK, ROW_BLK), 0)
    c_i = lax.broadcasted_iota(jnp.int32, (ROW_BLK, ROW_BLK), 1)
    tri = (r_i // HG_CHUNK == c_i // HG_CHUNK) & ((c_i >= r_i) if reverse else (c_i <= r_i))
    tri = jnp.where(tri, 1.0, 0.0).astype(BF16)
    lf_hi, lf_lo = _split(lf)
    bcum = _dot(tri, lf_hi) + _dot(tri, lf_lo)
    b_s[...] = bcum

    spread = jnp.zeros((1, LANES), F32)
    for r0 in range(0, ROW_BLK, HG_FAST_SUB):
        spread = jnp.maximum(
            spread, jnp.abs(bcum[r0:r0 + 1] - bcum[r0 + HG_FAST_SUB - 1:r0 + HG_FAST_SUB]))
    small = jnp.max(spread) < HG_SPREAD_MAX

    @pl.when(small)
    def _():
        _hgrn_fast_block(q_s, k_s, b_s, v_s, o_ref, st_ref, reverse)

    @pl.when(jnp.logical_not(small))
    def _():
        n_chunk = ROW_BLK // HG_CHUNK
        st = st_ref[...]
        for ci in (range(n_chunk - 1, -1, -1) if reverse else range(n_chunk)):
            rows = pl.ds(ci * HG_CHUNK, HG_CHUNK)
            o, st = _hgrn_exact_chunk(q_s[rows, :], k_s[rows, :], b_s[rows, :], v_s[rows, :], st,
                                      reverse)
            o_ref[rows, :] = o.astype(o_ref.dtype)
        st_ref[...] = st


def _hgrn_scan(rows, p_main, lb_logits, layer, direction):
    reverse = direction == 1
    n_all = p_main.shape[0]
    steps = rows.nct + rows.nlt
    depth = lb_logits.shape[1]

    def spec(seg):
        return pl.BlockSpec((ROW_BLK, HEAD_W),
                            lambda b, h, s: (rows.scan_block(b, s, reverse), seg * HEADS + h))

    f_seg = S_HG_FB if reverse else S_HG_FF
    return pl.pallas_call(
        functools.partial(_hgrn_kernel, layer, reverse),
        grid=(rows.batch, HEADS, steps),
        in_specs=[spec(S_HG_Q), spec(f_seg), spec(S_HG_I),
                  pl.BlockSpec((1, depth, HEAD_W), lambda b, h, s: (direction, 0, h))],
        out_specs=pl.BlockSpec((ROW_BLK, HEAD_W),
                               lambda b, h, s: (rows.scan_block(b, s, reverse), h)),
        out_shape=jax.ShapeDtypeStruct((n_all, SEG), BF16),
        scratch_shapes=[pltpu.VMEM((HEAD_W, HEAD_W), F32)]
        + [pltpu.VMEM((ROW_BLK, HEAD_W), F32)] * 4,
        compiler_params=_cp(("arbitrary", "arbitrary", "arbitrary")),
        name="hgrn2_scan",
    )(p_main, p_main, p_main, lb_logits)


def _mlstm_kernel(*refs):
    c = ROW_BLK
    n_in = 7
    dirs = [dict(reverse=False, ins=refs[0:n_in], o=refs[2 * n_in], s=refs[2 * n_in + 2],
                 m=refs[2 * n_in + 3]),
            dict(reverse=True, ins=refs[n_in:2 * n_in], o=refs[2 * n_in + 1], s=refs[2 * n_in + 4],
                 m=refs[2 * n_in + 5])]

    @pl.when(pl.program_id(2) == 0)
    def _():
        for d in dirs:
            d['s'][...] = jnp.zeros_like(d['s'])
            d['m'][...] = jnp.zeros_like(d['m'])

    r_i = lax.broadcasted_iota(jnp.int32, (c, c), 0)
    c_i = lax.broadcasted_iota(jnp.int32, (c, c), 1)
    for d in dirs:
        q_ref, k_ref, v_ref, ig_ref, fg_ref, ib_ref, fb_ref = d['ins']
        d['q'] = q_ref[...]
        d['kt'] = (k_ref[...].astype(F32) * HEAD_W ** -0.5).T
        d['v_cat'] = jnp.concatenate([v_ref[...], jnp.ones((c, HEAD_W), BF16)], axis=1)
        d['ig'] = ig_ref[0] + ib_ref[0][:, :1]
        fx = fg_ref[0] + fb_ref[0][:, :1]
        lf = jnp.minimum(fx, 0.0) - jnp.log1p(jnp.exp(-jnp.abs(fx)))
        cum = jnp.where((r_i >= c_i) if d['reverse'] else (r_i <= c_i), 1.0, 0.0).astype(BF16)
        lf_hi, lf_lo = _split(jnp.broadcast_to(lf, (8, c)))
        d['brow'] = (_dot(lf_hi, cum) + _dot(lf_lo, cum))[0:1]
    for d in dirs:
        d['qk'] = _dot(d['q'], d['kt'].astype(BF16))
        d['state'] = d['s'][...]
        d['q_state'] = _dot(d['q'], d['state'].astype(BF16))
    for d in dirs:
        b_rows = jnp.broadcast_to(d['brow'], (c, c))
        b_cols = b_rows.T
        keep = (c_i >= r_i) if d['reverse'] else (c_i <= r_i)
        dlog = jnp.where(keep, b_cols - b_rows + d['ig'], NEG_BIG)
        d['m_prev'] = d['m'][:, :1]
        inter = b_cols[:, :1] + d['m_prev']
        d['m_t'] = jnp.maximum(jnp.max(dlog, axis=-1, keepdims=True), inter)
        d['w'] = (jnp.exp(dlog - d['m_t']) * d['qk']).astype(BF16)
        d['w_inter'] = jnp.exp(inter - d['m_t'])
    for d in dirs:
        nd = _dot(d['w'], d['v_cat']) + d['w_inter'] * d['q_state']
        num, den = nd[:, :HEAD_W], nd[:, HEAD_W:]
        d['o'][...] = (num / jnp.maximum(jnp.abs(den), jnp.exp(-d['m_t']))).astype(d['o'].dtype)
    for d in dirs:
        last = 0 if d['reverse'] else c - 1
        m_new = d['m_t'][last:last + 1]
        b_last = d['brow'][:, last:last + 1]
        wk = jnp.exp(b_last - d['brow'] + d['ig'] - m_new)
        dec = jnp.exp(b_last + d['m_prev'] - m_new)
        d['s'][...] = dec * d['state'] + _dot((d['kt'] * wk).astype(BF16), d['v_cat'])
        d['m'][...] = jnp.broadcast_to(m_new, d['m'].shape)


def _mlstm_scan(rows, p_main, gates_t, i_bias, f_bias):
    n_all = p_main.shape[0]
    steps = rows.nct + rows.nlt

    def dir_specs(direction):
        reverse = direction == 1
        blk = lambda b, s: rows.scan_block(b, s, reverse)
        spec = lambda seg: pl.BlockSpec((ROW_BLK, HEAD_W),
                                        lambda b, h, s: (blk(b, s), seg * HEADS + h))
        gate = lambda which: pl.BlockSpec((1, 1, ROW_BLK),
                                          lambda b, h, s: (which * HEADS + h, 0, blk(b, s)))
        bias = pl.BlockSpec((1, 1, LANES), lambda b, h, s: (direction * HEADS + h, 0, 0))
        ins = [spec(S_ML_Q), spec(S_ML_K), spec(S_ML_V), gate(direction), gate(2 + direction),
               bias, bias]
        return ins, pl.BlockSpec((ROW_BLK, HEAD_W), lambda b, h, s: (blk(b, s), h))

    ins_f, out_f = dir_specs(0)
    ins_b, out_b = dir_specs(1)
    out = jax.ShapeDtypeStruct((n_all, SEG), BF16)
    args = (p_main, p_main, p_main, gates_t, gates_t, i_bias, f_bias)
    return pl.pallas_call(
        _mlstm_kernel,
        grid=(rows.batch, HEADS, steps),
        in_specs=ins_f + ins_b,
        out_specs=[out_f, out_b],
        out_shape=[out, out],
        scratch_shapes=[pltpu.VMEM((HEAD_W, 2 * HEAD_W), F32), pltpu.VMEM((1, LANES), F32)] * 2,
        compiler_params=_cp(("arbitrary", "arbitrary", "arbitrary")),
        name="mlstm_scan",
    )(*args, *args)


def _da_prep_kernel(q_ref, k_ref, v_ref, cos_ref, sin_ref, qg_ref, kg_ref, qt_ref, ko_ref, vt_ref):
    r_i = lax.broadcasted_iota(jnp.int32, (LANES, LANES), 0)
    c_i = lax.broadcasted_iota(jnp.int32, (LANES, LANES), 1)
    blockdiag = jnp.where((r_i // DA_QK) == (c_i // DA_QK), 1.0 / DA_QK, 0.0).astype(BF16)
    cos, sin = cos_ref[...], sin_ref[...]
    lane = lax.broadcasted_iota(jnp.int32, cos.shape, 1)
    first_half = (lane % DA_QK) < (DA_QK // 2)

    def qk_norm_rope(x, gain):
        x_hi, x_lo = _split(x * x)
        ms = _dot(x_hi, blockdiag) + _dot(x_lo, blockdiag)
        y = x * lax.rsqrt(ms + EPS) * gain
        rot = jnp.where(first_half, -pltpu.roll(y, LANES - DA_QK // 2, 1),
                        pltpu.roll(y, DA_QK // 2, 1))
        return y * cos + rot * sin

    for h in range(HEADS):
        cols = slice(h * HEAD_W, (h + 1) * HEAD_W)
        q = qk_norm_rope(q_ref[:, cols].astype(F32), qg_ref[...]) * (DA_QK ** -0.5 * LOG2E)
        qt_ref[cols, :] = q.T.astype(BF16)
        ko_ref[:, cols] = qk_norm_rope(k_ref[:, cols].astype(F32), kg_ref[...]).astype(BF16)
        vt_ref[cols, :] = v_ref[:, cols].astype(F32).T.astype(BF16)


def _da_prep(rows, p_main, cos_tab, sin_tab, q_gain, k_gain):
    n_all = p_main.shape[0]
    tm = ROW_BLK
    out_t = jax.ShapeDtypeStruct((SEG, n_all), BF16)

    def tab_idx(i):
        lat = i % rows.nlt
        ctx = rows.nlt + (i - rows.batch * rows.nlt) % rows.nct
        return (jnp.where(i < rows.batch * rows.nlt, lat, ctx), 0)

    out = jax.ShapeDtypeStruct((n_all, SEG), BF16)
    return pl.pallas_call(
        _da_prep_kernel,
        grid=(n_all // tm,),
        in_specs=[pl.BlockSpec((tm, SEG), lambda i: (i, S_DA_Q)),
                  pl.BlockSpec((tm, SEG), lambda i: (i, S_DA_K)),
                  pl.BlockSpec((tm, SEG), lambda i: (i, S_DA_V)),
                  pl.BlockSpec((tm, LANES), tab_idx),
                  pl.BlockSpec((tm, LANES), tab_idx),
                  pl.BlockSpec((1, LANES), lambda i: (0, 0)),
                  pl.BlockSpec((1, LANES), lambda i: (0, 0))],
        out_specs=[pl.BlockSpec((SEG, tm), lambda i: (0, i)),
                   pl.BlockSpec((tm, SEG), lambda i: (i, 0)),
                   pl.BlockSpec((SEG, tm), lambda i: (0, i))],
        out_shape=[out_t, out, out_t],
        compiler_params=_cp(("arbitrary",)),
        name="da_qk_prep",
    )(p_main, p_main, p_main, cos_tab, sin_tab, q_gain, k_gain)


def _attn_kernel(lam_init, n_lat_q_blocks, n_lat_keys, tk, qt_ref, kc_ref, vct_ref, kl_ref, vlt_ref,
                 lam_ref, sn_ref, o_ref, acc_ref, m_ref, l_ref, sa_ref, sb_ref):
    tq = qt_ref.shape[1]
    qt = qt_ref[...]
    chan = lax.broadcasted_iota(jnp.int32, qt.shape, 0)
    zero = jnp.zeros_like(qt)
    q2t = jnp.concatenate([jnp.where(chan < DA_QK, qt, zero), jnp.where(chan >= DA_QK, qt, zero)],
                          axis=1)

    st = _dot(kc_ref[...], q2t)
    m0 = jnp.max(st, axis=0, keepdims=True)
    p = jnp.exp2(st - m0)
    m_ref[...] = m0
    l_ref[...] = jnp.sum(p, axis=0, keepdims=True)
    acc_ref[...] = _dot(vct_ref[...], p.astype(BF16))

    n_chunks = n_lat_keys // tk
    unroll = _pick(n_chunks, (4, 2, 1))
    bufs = (sa_ref, sb_ref) if unroll > 1 else (sa_ref, sa_ref)

    def chunk(c):
        return pl.ds(pl.multiple_of(c * tk, tk), tk)

    def scores(c, buf):
        st = _dot(kl_ref[chunk(c), :], q2t)
        buf[...] = st
        return jnp.max(st, axis=0, keepdims=True)

    def absorb(c, buf, m_cur):
        m_old = m_ref[...]
        m_new = jnp.maximum(m_old, m_cur)
        p = jnp.exp2(buf[...] - m_new)
        pv = _dot(vlt_ref[:, chunk(c)], p.astype(BF16))
        alpha = jnp.exp2(m_old - m_new)
        l_ref[...] = alpha * l_ref[...] + jnp.sum(p, axis=0, keepdims=True)
        acc_ref[...] = alpha * acc_ref[...] + pv
        m_ref[...] = m_new

    def body(j, m_pend):
        for u in range(unroll):
            c = j * unroll + u
            if unroll > 1:
                m_next = scores(jnp.minimum(c + 1, n_chunks - 1), bufs[(u + 1) % 2])
                absorb(c, bufs[u % 2], m_pend)
            else:
                absorb(c, bufs[0], m_pend)
                m_next = scores(jnp.minimum(c + 1, n_chunks - 1), bufs[0])
            m_pend = m_next
        return m_pend

    n_trips = jnp.where(pl.program_id(2) < n_lat_q_blocks, n_chunks // unroll, 0)
    lax.fori_loop(0, n_trips, body, scores(0, bufs[0]))

    lv = lam_ref[...]
    lam = (jnp.exp(jnp.sum(lv[0:1] * lv[1:2], axis=-1, keepdims=True))
           - jnp.exp(jnp.sum(lv[2:3] * lv[3:4], axis=-1, keepdims=True)) + lam_init)
    o = acc_ref[...] / l_ref[...]
    a = o[:, :tq] - lam * o[:, tq:]
    ms = jnp.mean(a * a, axis=0, keepdims=True)
    y = a * lax.rsqrt(ms + EPS) * sn_ref[...] * (1.0 - lam_init)
    o_ref[...] = y.T.astype(o_ref.dtype)


def _attention_call(rows, qt, kh, vt, lam_vec, sub_gain, lam_init, n_rows, prev):
    assert rows.nct == 1
    ctx_blk = lambda b: rows.batch * rows.nlt + b
    tk = _pick(rows.seq, (512, 256))
    if prev is None:
        tq = _pick(rows.seq, (512, 256))
        n_q = rows.seq // tq
        q_blk = lambda b, i: b * n_q + i
        kern = functools.partial(_attn_kernel, lam_init, n_q, rows.seq, tk)
        extra_specs, extra_args, aliases = [], [], {}
    else:
        tq, n_q = ROW_BLK, 0
        q_blk = lambda b, i: ctx_blk(b)
        attn = functools.partial(_attn_kernel, lam_init, n_q, rows.seq, tk)
        kern = lambda *refs: attn(*refs[:7], *refs[8:])
        extra_specs, extra_args, aliases = [pl.BlockSpec(memory_space=pl.ANY)], [prev], {7: 0}
    return pl.pallas_call(
        kern,
        grid=(rows.batch, HEADS, max(n_q, 1)),
        in_specs=[pl.BlockSpec((HEAD_W, tq), lambda b, h, i: (h, q_blk(b, i))),
                  pl.BlockSpec((ROW_BLK, HEAD_W), lambda b, h, i: (ctx_blk(b), h)),
                  pl.BlockSpec((HEAD_W, ROW_BLK), lambda b, h, i: (h, ctx_blk(b))),
                  pl.BlockSpec((rows.seq, HEAD_W), lambda b, h, i: (b, h)),
                  pl.BlockSpec((HEAD_W, rows.seq), lambda b, h, i: (h, b)),
                  pl.BlockSpec(lam_vec.shape, lambda b, h, i: (0, 0)),
                  pl.BlockSpec((HEAD_W, 1), lambda b, h, i: (0, 0))] + extra_specs,
        out_specs=pl.BlockSpec((tq, HEAD_W), lambda b, h, i: (q_blk(b, i), h)),
        out_shape=jax.ShapeDtypeStruct((n_rows, SEG), BF16),
        scratch_shapes=[pltpu.VMEM((HEAD_W, 2 * tq), F32), pltpu.VMEM((1, 2 * tq), F32),
                        pltpu.VMEM((1, 2 * tq), F32), pltpu.VMEM((tk, 2 * tq), F32),
                        pltpu.VMEM((tk, 2 * tq), F32)],
        input_output_aliases=aliases,
        compiler_params=_cp(("arbitrary", "arbitrary", "arbitrary")),
        name="diff_attention" if prev is None else "diff_attention_ctx",
    )(qt, kh, vt, kh, vt, lam_vec, sub_gain, *extra_args)


def _attention(rows, qt, kh, vt, lam_vec, sub_gain, lam_init, n_rows):
    da = _attention_call(rows, qt, kh, vt, lam_vec, sub_gain, lam_init, n_rows, None)
    if n_rows > rows.n_lat:
        da = _attention_call(rows, qt, kh, vt, lam_vec, sub_gain, lam_init, n_rows, da)
    return da


def _merge_kernel(ohf_ref, ohb_ref, hg_ref, da_ref, omf_ref, omb_ref, mo_ref, g0_ref, g1_ref,
                  g2_ref, hn_ref, mn_ref, wb_ref, o_ref, h_ref):
    @pl.when(pl.program_id(1) == 0)
    def _():
        for h in range(HEADS):
            cols = slice(h * HEAD_W, (h + 1) * HEAD_W)
            o = ohf_ref[:, cols].astype(F32) + ohb_ref[:, cols].astype(F32)
            y = o * lax.rsqrt(jnp.mean(o * o, axis=-1, keepdims=True) + EPS) * hn_ref[...]
            g = hg_ref[:, cols].astype(F32)
            h_ref[0, :, cols] = (y * (g * _sigmoid(g))).astype(BF16)
            o = omf_ref[:, cols].astype(F32) + omb_ref[:, cols].astype(F32)
            y = o * lax.rsqrt(jnp.mean(o * o, axis=-1, keepdims=True) + EPS) * mn_ref[...]
            h_ref[2, :, cols] = (y * _sigmoid(mo_ref[:, cols].astype(F32))).astype(BF16)
        h_ref[1] = da_ref[...]

    y = _sigmoid(g0_ref[...].astype(F32)) * _dot(h_ref[0], wb_ref[0])
    y = y + _sigmoid(g1_ref[...].astype(F32)) * _dot(h_ref[1], wb_ref[1])
    y = y + _sigmoid(g2_ref[...].astype(F32)) * _dot(h_ref[2], wb_ref[2])
    o_ref[...] = y.astype(o_ref.dtype)


def _merge(p_main, ohf, ohb, da, omf, omb, hg_gain, ml_gain, w_branch, d):
    n_all = da.shape[0]
    tm = 512
    tn = _pick(d, (512, 256, 128))
    gate_col0 = N_SEG * SEG // tn
    row = lambda i, j: (i, 0)

    def gate_spec(jj):
        return pl.BlockSpec((tm, tn), lambda i, j: (i, gate_col0 + jj * (d // tn) + j))

    return pl.pallas_call(
        _merge_kernel,
        grid=(n_all // tm, d // tn),
        in_specs=[pl.BlockSpec((tm, SEG), row), pl.BlockSpec((tm, SEG), row),
                  pl.BlockSpec((tm, SEG), lambda i, j: (i, S_HG_G)),
                  pl.BlockSpec((tm, SEG), row),
                  pl.BlockSpec((tm, SEG), row), pl.BlockSpec((tm, SEG), row),
                  pl.BlockSpec((tm, SEG), lambda i, j: (i, S_ML_O)),
                  gate_spec(0), gate_spec(1), gate_spec(2),
                  pl.BlockSpec((1, LANES), lambda i, j: (0, 0)),
                  pl.BlockSpec((1, LANES), lambda i, j: (0, 0)),
                  pl.BlockSpec((3, SEG, tn), lambda i, j: (0, 0, j))],
        out_specs=pl.BlockSpec((tm, tn), lambda i, j: (i, j)),
        out_shape=jax.ShapeDtypeStruct((n_all, d), BF16),
        scratch_shapes=[pltpu.VMEM((3, tm, SEG), BF16)],
        compiler_params=_cp(("arbitrary", "arbitrary")),
        name="branch_merge",
    )(ohf, ohb, p_main, da, omf, omb, p_main, p_main, p_main, p_main, hg_gain, ml_gain, w_branch)


def _proj_resid_kernel(a_ref, w_ref, x_ref, g_ref, o_ref):
    o_ref[...] = x_ref[...] + g_ref[0] * _dot(a_ref[...], w_ref[...])


def _proj_resid(rows, a, w, x_all, modsflat, layer, which_gate):
    d = x_all.shape[1]
    n_all, k = a.shape
    tm = 512
    tn = _pick(d, (1024, 512, 256, 128))
    return pl.pallas_call(
        _proj_resid_kernel,
        grid=(d // tn, n_all // tm),
        in_specs=[pl.BlockSpec((tm, k), lambda j, i: (i, 0)),
                  pl.BlockSpec((k, tn), lambda j, i: (0, j)),
                  pl.BlockSpec((tm, tn), lambda j, i: (i, j)),
                  pl.BlockSpec((1, 1, tn),
                               lambda j, i: (rows.mod_row(layer, i, tm, which_gate), 0, j))],
        out_specs=pl.BlockSpec((tm, tn), lambda j, i: (i, j)),
        out_shape=jax.ShapeDtypeStruct((n_all, d), F32),
        compiler_params=_cp(("arbitrary", "arbitrary")),
        name="out_proj_residual",
    )(a, w, x_all, modsflat)


def _shared_expert_kernel(tok_ref, w1_ref, w3_ref, w2_ref, o_ref):
    tok = tok_ref[...]
    h1 = _dot(tok, w1_ref[...])
    h3 = _dot(tok, w3_ref[...])
    a = (h1 * _sigmoid(h1)) * h3
    o_ref[...] = _dot(a.astype(BF16), w2_ref[...]).astype(o_ref.dtype)


def _shared_expert(tok, w1, w3, w2):
    n_all, d = tok.shape
    de = w1.shape[1]
    tm = 512
    full = lambda i: (0, 0)
    return pl.pallas_call(
        _shared_expert_kernel,
        grid=(n_all // tm,),
        in_specs=[pl.BlockSpec((tm, d), lambda i: (i, 0)),
                  pl.BlockSpec((d, de), full), pl.BlockSpec((d, de), full),
                  pl.BlockSpec((de, d), full)],
        out_specs=pl.BlockSpec((tm, d), lambda i: (i, 0)),
        out_shape=jax.ShapeDtypeStruct((n_all, d), BF16),
        compiler_params=_cp(("arbitrary",)),
        name="moe_shared_expert",
    )(tok, w1, w3, w2)


def _moe_plan_kernel(route_ref, pos_ref, cnt_ref, carry_ref):
    @pl.when(pl.program_id(0) == 0)
    def _():
        carry_ref[...] = jnp.zeros_like(carry_ref)

    r = route_ref[...]
    tm = r.shape[0]
    lane = lax.broadcasted_iota(jnp.int32, r.shape, 1).astype(F32)
    mask = jnp.zeros_like(r)
    for k in range(TOP_K):
        mask = mask + jnp.where(lane == r[:, k:k + 1], 1.0, 0.0)
    r_i = lax.broadcasted_iota(jnp.int32, (tm, tm), 0)
    c_i = lax.broadcasted_iota(jnp.int32, (tm, tm), 1)
    before = jnp.where(c_i < r_i, 1.0, 0.0).astype(BF16)
    rank = _dot(before, mask.astype(BF16)) + carry_ref[...]
    out = jnp.zeros_like(r)
    for k in range(TOP_K):
        pk = jnp.sum(jnp.where(lane == r[:, k:k + 1], rank, 0.0), axis=-1, keepdims=True)
        out = jnp.where(lane == k, pk, out)
    pos_ref[...] = out
    carry = carry_ref[...] + jnp.sum(mask, axis=0, keepdims=True)
    carry_ref[...] = carry
    cnt_ref[...] = jnp.broadcast_to(carry, cnt_ref.shape)


def _moe_plan(route):
    n = route.shape[0]
    tm = ROW_BLK
    return pl.pallas_call(
        _moe_plan_kernel,
        grid=(n // tm,),
        in_specs=[pl.BlockSpec((tm, LANES), lambda i: (i, 0))],
        out_specs=[pl.BlockSpec((tm, LANES), lambda i: (i, 0)),
                   pl.BlockSpec((8, LANES), lambda i: (0, 0))],
        out_shape=[jax.ShapeDtypeStruct((n, LANES), F32), jax.ShapeDtypeStruct((8, LANES), F32)],
        scratch_shapes=[pltpu.VMEM((1, LANES), F32)],
        compiler_params=_cp(("arbitrary",)),
        name="moe_plan",
    )(route)


def _moe_routed_kernel(ns, te_ref, na_ref, st_ref, tok_hbm, sw_ref, w1_ref, w3_ref, w2_ref, o_ref,
                       xbuf, x_ref, sem):
    tm = MOE_TILE
    t = pl.program_id(0)
    n_active = na_ref[0]
    par = t % 2

    def gather_copy(tile, buf, r):
        tok = st_ref[tile * tm + r]
        return pltpu.make_async_copy(
            tok_hbm.at[pl.ds(pl.multiple_of(tok * ns, ns), ns), :],
            xbuf.at[buf, pl.ds(pl.multiple_of(r * ns, ns), ns), :], sem.at[buf])

    def for_each_copy(tile, buf, act):
        def body(r4, carry):
            for u in range(4):
                act(gather_copy(tile, buf, r4 * 4 + u))
            return carry
        lax.fori_loop(0, tm // 4, body, 0)

    @pl.when(t == 0)
    def _():
        for_each_copy(0, 0, lambda cp: cp.start())

    @pl.when(t + 1 < n_active)
    def _():
        for_each_copy(t + 1, 1 - par, lambda cp: cp.start())

    @pl.when(t < n_active)
    def _():
        for_each_copy(t, par, lambda cp: cp.wait())
        d = x_ref.shape[1]

        def emit(s, lo, hi):
            x_ref[:, s * LANES:(s + 1) * LANES] = lo.astype(BF16)
            x_ref[:, d // 2 + s * LANES:d // 2 + (s + 1) * LANES] = hi.astype(BF16)
        _load_slabs(xbuf.at[par], 0, tm, ns, emit)
        x = x_ref[...]
        h1 = _dot(x, w1_ref[...])
        h3 = _dot(x, w3_ref[...])
        a = (h1 * _sigmoid(h1)) * h3 * sw_ref[...]
        _store_slabs(_dot(a.astype(BF16), w2_ref[...]), o_ref)

    @pl.when(t >= n_active)
    def _():
        o_ref[...] = jnp.zeros_like(o_ref)


def _moe_routed(tok_slabs, slot_token, slot_w, tile_expert, n_active, w1, w3, w2):
    n_exp, d, de = w1.shape
    ns = d // (2 * LANES)
    tm = MOE_TILE
    n_slots = slot_token.shape[0]
    nt = n_slots // tm
    grid_spec = pltpu.PrefetchScalarGridSpec(
        num_scalar_prefetch=3,
        grid=(nt,),
        in_specs=[pl.BlockSpec(memory_space=pl.ANY),
                  pl.BlockSpec((tm, 1), lambda t, te, na, st: (t, 0)),
                  pl.BlockSpec((None, d, de), lambda t, te, na, st: (te[t], 0, 0)),
                  pl.BlockSpec((None, d, de), lambda t, te, na, st: (te[t], 0, 0)),
                  pl.BlockSpec((None, de, d), lambda t, te, na, st: (te[t], 0, 0))],
        out_specs=pl.BlockSpec((tm * ns, LANES), lambda t, te, na, st: (t, 0)),
        scratch_shapes=[pltpu.VMEM((2, tm * ns, LANES), jnp.int32), pltpu.VMEM((tm, d), BF16),
                        pltpu.SemaphoreType.DMA((2,))],
    )
    return pl.pallas_call(
        functools.partial(_moe_routed_kernel, ns),
        grid_spec=grid_spec,
        out_shape=jax.ShapeDtypeStruct((n_slots * ns, LANES), jnp.int32),
        compiler_params=_cp(("arbitrary",)),
        name="moe_routed_experts",
    )(tile_expert, n_active, slot_token, tok_slabs, slot_w, w1, w3, w2)


def _moe_combine_kernel(ns, s4_ref, ys_hbm, ysh_ref, x_ref, g_ref, o_ref, ybuf, sem):
    tm = x_ref.shape[0]
    d = x_ref.shape[1]
    i = pl.program_id(0)
    par = i % 2

    def gather_copy(tile, buf, row, k):
        slot = s4_ref[(tile * tm + row) * TOP_K + k]
        return pltpu.make_async_copy(
            ys_hbm.at[pl.ds(pl.multiple_of(slot * ns, ns), ns), :],
            ybuf.at[buf, pl.ds(pl.multiple_of((k * tm + row) * ns, ns), ns), :], sem.at[buf])

    def for_each_copy(tile, buf, act):
        def body(row, carry):
            for k in range(TOP_K):
                act(gather_copy(tile, buf, row, k))
            return carry
        lax.fori_loop(0, tm, body, 0)

    @pl.when(i == 0)
    def _():
        for_each_copy(0, 0, lambda cp: cp.start())

    @pl.when(i + 1 < pl.num_programs(0))
    def _():
        for_each_copy(i + 1, 1 - par, lambda cp: cp.start())

    for_each_copy(i, par, lambda cp: cp.wait())

    gate = g_ref[0]
    for s in range(ns):
        lo_cols = slice(s * LANES, (s + 1) * LANES)
        hi_cols = slice(d // 2 + s * LANES, d // 2 + (s + 1) * LANES)
        y_lo = ysh_ref[:, lo_cols].astype(F32)
        y_hi = ysh_ref[:, hi_cols].astype(F32)
        for k in range(TOP_K):
            lo, hi = _unpack_pair(ybuf[par, pl.ds(k * tm * ns + s, tm, stride=ns), :])
            y_lo = y_lo + lo
            y_hi = y_hi + hi
        o_ref[:, lo_cols] = x_ref[:, lo_cols] + gate[:, lo_cols] * y_lo
        o_ref[:, hi_cols] = x_ref[:, hi_cols] + gate[:, hi_cols] * y_hi


def _moe_combine(rows, x_all, y_shared, ys_slabs, slot4, modsflat, layer, which_gate, n_rows):
    d = x_all.shape[1]
    ns = d // (2 * LANES)
    tm = 128
    grid_spec = pltpu.PrefetchScalarGridSpec(
        num_scalar_prefetch=1,
        grid=(n_rows // tm,),
        in_specs=[pl.BlockSpec(memory_space=pl.ANY),
                  pl.BlockSpec((tm, d), lambda i, s4: (i, 0)),
                  pl.BlockSpec((tm, d), lambda i, s4: (i, 0)),
                  pl.BlockSpec((1, 1, d),
                               lambda i, s4: (rows.mod_row(layer, i, tm, which_gate), 0, 0))],
        out_specs=pl.BlockSpec((tm, d), lambda i, s4: (i, 0)),
        scratch_shapes=[pltpu.VMEM((2, TOP_K * tm * ns, LANES), jnp.int32),
                        pltpu.SemaphoreType.DMA((2,))],
    )
    return pl.pallas_call(
        functools.partial(_moe_combine_kernel, ns),
        grid_spec=grid_spec,
        out_shape=jax.ShapeDtypeStruct((n_rows, d), F32),
        compiler_params=_cp(("arbitrary",)),
        name="moe_combine_residual",
    )(slot4, ys_slabs, y_shared, x_all, modsflat)


def _moe_slots(route, pos, counts, n_exp):
    n = route.shape[0]
    tm = MOE_TILE
    idx4 = route[:, :TOP_K].astype(jnp.int32)
    w4 = route[:, TOP_K:2 * TOP_K]
    pos4 = pos[:, :TOP_K].astype(jnp.int32)
    cnt = counts[0, :n_exp].astype(jnp.int32)
    padded = (cnt + tm - 1) // tm * tm
    ends = jnp.cumsum(padded)
    starts = ends - padded
    slot4 = (starts[idx4] + pos4).reshape(-1)
    n_slots = n * TOP_K + n_exp * tm
    nt = n_slots // tm
    tile_expert = jnp.minimum(
        jnp.searchsorted(ends, jnp.arange(nt, dtype=jnp.int32) * tm, side='right'),
        n_exp - 1).astype(jnp.int32)
    n_active = (ends[-1:] // tm).astype(jnp.int32)
    token_of = jnp.repeat(jnp.arange(n, dtype=jnp.int32), TOP_K)
    pairs = jnp.stack([token_of, lax.bitcast_convert_type(w4.reshape(-1), jnp.int32)], axis=1)
    slot_meta = jnp.zeros((n_slots, 2), jnp.int32).at[slot4].set(pairs)
    slot_token = slot_meta[:, 0]
    slot_w = lax.bitcast_convert_type(slot_meta[:, 1:2], F32)
    return slot4, slot_token, slot_w, tile_expert, n_active


def _rope_tables(seq, ctx_len):
    n = DA_QK // 4
    t = jnp.arange(seq)
    inv = ROPE_THETA ** (-jnp.arange(n, dtype=F32) / n)
    row = (t // GRID_W).astype(F32)
    col = (t % GRID_W).astype(F32)
    ang = jnp.concatenate([row[:, None] * inv, col[:, None] * inv], axis=-1)
    ang = jnp.concatenate([ang, jnp.zeros((ctx_len, 2 * n), F32)], axis=0)
    return jnp.tile(jnp.cos(ang), (1, 4)), jnp.tile(jnp.sin(ang), (1, 4))


def kernel(x, c, ctx, c_ctx, norm1, norm2, w_ada, b_ada, w_in, hg_lb_logits, hg_norm, da_q_norm,
           da_k_norm, da_lambda, da_sub_norm, ml_igate_bias, ml_fgate_bias, ml_norm, w_branch,
           w_out, router_w, router_bias, exp_w1, exp_w3, exp_w2, sh_w1, sh_w3, sh_w2):
    batch, seq, d = x.shape
    ctx_len = ctx.shape[1]
    depth = w_ada.shape[0]
    n_exp = router_w.shape[-1]
    rows = _Rows(batch, seq, ctx_len)
    assert batch + 1 <= 16 and n_exp <= LANES

    x_all = jnp.concatenate([x.reshape(batch * seq, d), ctx.reshape(batch * ctx_len, d)], axis=0)
    cvec = jnp.zeros((16, d), F32).at[0].set(c_ctx).at[1:1 + batch].set(c)
    mods = _mods(cvec, w_ada, b_ada)
    modsflat = mods[:, :rows.n_groups].reshape(depth * rows.n_groups * 6, 1, d)
    cos_tab, sin_tab = _rope_tables(seq, ctx_len)
    gate_lo = N_SEG * SEG

    for l in range(depth):
        n_rows = rows.n_all if l < depth - 1 else rows.n_lat
        lam_init = 0.8 - 0.6 * math.exp(-0.3 * l)
        w_main = jnp.concatenate([w_in[l, :, :gate_lo], w_in[l, :, gate_lo + N_GATE_COLS:]],
                                 axis=1).astype(BF16)
        w_gate = jnp.pad(w_in[l, :, gate_lo:gate_lo + N_GATE_COLS],
                         ((0, 0), (0, LANES - N_GATE_COLS))).astype(BF16)

        hx = _norm(rows, x_all, rows.n_all, norm1[l], modsflat, l, 1, 0)
        p_main = _matmul(hx, w_main, BF16)
        p_gate = _matmul(hx, w_gate, F32)
        gates_t = p_gate[:, :N_GATE_COLS].T.reshape(N_GATE_COLS, 1, rows.n_all)

        ohf = _hgrn_scan(rows, p_main, hg_lb_logits, l, 0)
        ohb = _hgrn_scan(rows, p_main, hg_lb_logits, l, 1)

        tile2 = lambda g: jnp.tile(g.reshape(1, DA_QK), (1, 2))
        qt, kh, vt = _da_prep(rows, p_main, cos_tab, sin_tab, tile2(da_q_norm[l]),
                              tile2(da_k_norm[l]))
        da = _attention(rows, qt, kh, vt, da_lambda[l], da_sub_norm[l].reshape(HEAD_W, 1),
                        lam_init, n_rows)

        bias = lambda bv: jnp.broadcast_to(bv.reshape(2 * HEADS, 1, 1), (2 * HEADS, 1, LANES))
        i_b, f_b = bias(ml_igate_bias[l]), bias(ml_fgate_bias[l])
        omf, omb = _mlstm_scan(rows, p_main, gates_t, i_b, f_b)

        ymid = _merge(p_main, ohf, ohb, da, omf, omb, hg_norm[l].reshape(1, HEAD_W),
                      ml_norm[l].reshape(1, HEAD_W), w_branch[l].astype(BF16), d)
        x_all = _proj_resid(rows, ymid, w_out[l].astype(BF16), x_all, modsflat, l, 2)

        rw = jnp.pad(router_w[l], ((0, 0), (0, LANES - n_exp)))
        rw_hi = rw.astype(BF16)
        rw_lo = (rw - rw_hi.astype(F32)).astype(BF16)
        rb = jnp.pad(router_bias[l], (0, LANES - n_exp)).reshape(1, LANES)
        tok, tok_slabs, route = _norm(rows, x_all, n_rows, norm2[l], modsflat, l, 4, 3,
                                      router=(rw_hi, rw_lo, rb, n_exp))
        pos, counts = _moe_plan(route)
        slot4, slot_token, slot_w, tile_expert, n_active = _moe_slots(route, pos, counts, n_exp)
        ys = _moe_routed(tok_slabs, slot_token, slot_w, tile_expert, n_active,
                         exp_w1[l].astype(BF16), exp_w3[l].astype(BF16), exp_w2[l].astype(BF16))
        y_sh = _shared_expert(tok, sh_w1[l].astype(BF16), sh_w3[l].astype(BF16),
                              sh_w2[l].astype(BF16))
        x_all = _moe_combine(rows, x_all, y_sh, ys, slot4, modsflat, l, 5, n_rows)

    return x_all.reshape(batch, seq, d)
```

```python
import functools
import math

import jax
import jax.numpy as jnp
from jax import lax
from jax.experimental import pallas as pl
from jax.experimental.pallas import tpu as pltpu

F32 = jnp.float32
BF16 = jnp.bfloat16

EPS = 1e-6
NEG_BIG = -1e30
TINY = 1e-30
GRID_W = 64
ROPE_THETA = 10000.0
ROUTED_SCALE = 2.5
TOP_K = 4
LOG2E = 1.4426950408889634

HEADS = 12
HEAD_W = 128
SEG = HEADS * HEAD_W
DA_QK = 64
N_GATE_COLS = 4 * HEADS

S_HG_Q, S_HG_FF, S_HG_FB, S_HG_I, S_HG_G = 0, 1, 2, 3, 4
S_DA_Q, S_DA_K, S_DA_V = 5, 6, 7
S_ML_Q, S_ML_K, S_ML_V, S_ML_O = 8, 9, 10, 11
N_SEG = 12

LANES = 128
ROW_BLK = 256
HG_CHUNK = 64
HG_SUB = 16
HG_FAST_SUB = 32
HG_SPREAD_MAX = 60.0
MOE_TILE = 256
ATTN_PAD_ROWS = 16
ATTN_RANGE_MAX = 60.0
VMEM_LIMIT = 56 * 1024 * 1024


def _cp(sem, vmem=VMEM_LIMIT):
    return pltpu.CompilerParams(dimension_semantics=sem, vmem_limit_bytes=vmem)


def _pick(n, cands):
    for c in cands:
        if n % c == 0:
            return c
    raise ValueError(f"no tile for {n} in {cands}")


def _dot(a, b):
    return jnp.dot(a, b, preferred_element_type=F32)


def _dot_nt(a, b):
    return lax.dot_general(a, b, (((1,), (1,)), ((), ())), preferred_element_type=F32)


def _dot_tn(a, b):
    return lax.dot_general(a, b, (((0,), (0,)), ((), ())), preferred_element_type=F32)


def _split(x):
    hi = x.astype(BF16)
    lo = (x - hi.astype(F32)).astype(BF16)
    return hi, lo


def _sigmoid(x):
    return 1.0 / (1.0 + jnp.exp(-x))


class _Rows:
    def __init__(self, batch, seq, ctx_len):
        self.batch, self.seq, self.ctx = batch, seq, ctx_len
        assert seq % ROW_BLK == 0 and ctx_len % ROW_BLK == 0
        self.nlt = seq // ROW_BLK
        self.nct = ctx_len // ROW_BLK
        self.n_lat = batch * seq
        self.n_all = batch * (seq + ctx_len)
        self.n_groups = batch + 1

    def group(self, i, tm):
        lat_tiles = self.n_lat // tm
        return jnp.where(i < lat_tiles, 1 + i // (self.seq // tm), 0)

    def mod_row(self, layer, i, tm, which):
        return (layer * self.n_groups + self.group(i, tm)) * 6 + which

    def scan_block(self, b, s, reverse):
        if reverse:
            cblk = self.batch * self.nlt + b * self.nct + (self.nct - 1 - s)
            lblk = b * self.nlt + (self.nlt - 1 - (s - self.nct))
        else:
            cblk = self.batch * self.nlt + b * self.nct + s
            lblk = b * self.nlt + (s - self.nct)
        return jnp.where(s < self.nct, cblk, lblk)


def _mods_kernel(c_ref, w_ref, b_ref, o_ref):
    c = c_ref[...]
    s_hi, s_lo = _split(c * _sigmoid(c))
    w_hi, w_lo = _split(w_ref[...])
    acc = _dot(s_hi, w_hi) + _dot(s_lo, w_hi) + _dot(s_hi, w_lo)
    o_ref[...] = acc + b_ref[...]


def _mods(cvec, w_ada, b_ada):
    depth, d, n = w_ada.shape
    tn = _pick(n, (512, 256, 128))
    rows = cvec.shape[0]
    return pl.pallas_call(
        _mods_kernel,
        grid=(depth, n // tn),
        in_specs=[pl.BlockSpec((rows, d), lambda l, j: (0, 0)),
                  pl.BlockSpec((None, d, tn), lambda l, j: (l, 0, j)),
                  pl.BlockSpec((None, 1, tn), lambda l, j: (l, 0, j))],
        out_specs=pl.BlockSpec((None, rows, tn), lambda l, j: (l, 0, j)),
        out_shape=jax.ShapeDtypeStruct((depth, rows, n), F32),
        compiler_params=_cp(("arbitrary", "arbitrary")),
        name="adaln_mods",
    )(cvec, w_ada, b_ada.reshape(depth, 1, n))


def _split_rows(x, block, tile_axis):
    if not isinstance(x, tuple):
        return [pl.BlockSpec(block, lambda *g: (g[tile_axis], _other(g, tile_axis)))], [x], None
    n_first = x[0].shape[0] // block[0]
    first = pl.BlockSpec(block, lambda *g: (jnp.minimum(g[tile_axis], n_first - 1),
                                            _other(g, tile_axis)))
    second = pl.BlockSpec(block, lambda *g: (jnp.maximum(g[tile_axis] - n_first, 0),
                                             _other(g, tile_axis)))
    return [first, second], list(x), n_first


def _other(g, tile_axis):
    return g[1 - tile_axis] if len(g) == 2 else 0


def _row_tile(parts, i, n_first):
    if len(parts) == 1:
        return parts[0][...]
    return jnp.where(i < n_first, parts[0][...], parts[1][...])


def _norm_mod(x, g_ref, sc_ref, sh_ref):
    ms = jnp.mean(x * x, axis=-1, keepdims=True)
    y = x * lax.rsqrt(ms + EPS) * g_ref[...]
    return y * (1.0 + sc_ref[0]) + sh_ref[0]


def _norm_kernel(n_first, *refs):
    x_parts, (g_ref, sc_ref, sh_ref, o_ref) = refs[:-4], refs[-4:]
    x = _row_tile(x_parts, pl.program_id(0), n_first)
    o_ref[...] = _norm_mod(x, g_ref, sc_ref, sh_ref).astype(BF16)


def _pack_pair(lo, hi):
    lo_b = pltpu.bitcast(lo.astype(BF16).astype(F32), jnp.int32)
    hi_b = pltpu.bitcast(hi.astype(BF16).astype(F32), jnp.int32)
    return (hi_b & jnp.int32(-65536)) | lax.shift_right_logical(lo_b, 16)


def _unpack_pair(w):
    lo = pltpu.bitcast(lax.shift_left(w, 16), F32)
    hi = pltpu.bitcast(w & jnp.int32(-65536), F32)
    return lo, hi


def _store_slabs(y, o_ref):
    tm, d = y.shape
    ns = d // (2 * LANES)
    for s in range(ns):
        lo = y[:, s * LANES:(s + 1) * LANES]
        hi = y[:, d // 2 + s * LANES:d // 2 + (s + 1) * LANES]
        o_ref[pl.ds(s, tm, stride=ns), :] = _pack_pair(lo, hi)


def _load_slabs(src_ref, row0, tm, ns, emit):
    for s in range(ns):
        lo, hi = _unpack_pair(src_ref[pl.ds(row0 + s, tm, stride=ns), :])
        emit(s, lo, hi)


def _norm_router_kernel(n_exp, x_ref, g_ref, sc_ref, sh_ref, rwh_ref, rwl_ref, rb_ref,
                        o_ref, slab_ref, route_ref):
    y = _norm_mod(x_ref[...], g_ref, sc_ref, sh_ref)
    o_ref[...] = y.astype(BF16)
    _store_slabs(y, slab_ref)
    y_hi, y_lo = _split(y)
    rwh = rwh_ref[...]
    logits = _dot(y_hi, rwh) + _dot(y_lo, rwh) + _dot(y_hi, rwl_ref[...])
    scores = _sigmoid(logits)
    lane = lax.broadcasted_iota(jnp.int32, scores.shape, 1).astype(F32)
    work = jnp.where(lane < n_exp, scores + rb_ref[...], -jnp.inf)
    route = jnp.zeros_like(scores)
    total = jnp.zeros_like(scores[:, :1])
    for k in range(TOP_K):
        mx = jnp.max(work, axis=-1, keepdims=True)
        first = jnp.min(jnp.where(work == mx, lane, float(LANES)), axis=-1, keepdims=True)
        hit = lane == first
        sc = jnp.sum(jnp.where(hit, scores, 0.0), axis=-1, keepdims=True)
        total = total + sc
        route = jnp.where(lane == k, first, route)
        route = jnp.where(lane == TOP_K + k, sc, route)
        work = jnp.where(hit, -jnp.inf, work)
    is_w = (lane >= TOP_K) & (lane < 2 * TOP_K)
    route_ref[...] = jnp.where(is_w, route / total * ROUTED_SCALE, route)


def _norm(rows, x_all, n_all, gain, modsflat, layer, which_scale, which_shift, router=None):
    d = gain.shape[0]
    tm = ROW_BLK

    def mod_idx(which):
        return lambda i: (rows.mod_row(layer, i, tm, which), 0, 0)

    x_specs, x_args, n_first = _split_rows(x_all, (tm, d), 0)
    in_specs = x_specs + [pl.BlockSpec((1, d), lambda i: (0, 0)),
                          pl.BlockSpec((1, 1, d), mod_idx(which_scale)),
                          pl.BlockSpec((1, 1, d), mod_idx(which_shift))]
    args = x_args + [gain.reshape(1, d), modsflat, modsflat]
    out_spec = pl.BlockSpec((tm, d), lambda i: (i, 0))
    out_shape = jax.ShapeDtypeStruct((n_all, d), BF16)
    if router is None:
        return pl.pallas_call(
            functools.partial(_norm_kernel, n_first), grid=(n_all // tm,), in_specs=in_specs,
            out_specs=out_spec, out_shape=out_shape, compiler_params=_cp(("arbitrary",)),
            name="prenorm",
        )(*args)
    rw_hi, rw_lo, rbias, n_exp = router
    ns = d // (2 * LANES)
    in_specs += [pl.BlockSpec((d, LANES), lambda i: (0, 0)),
                 pl.BlockSpec((d, LANES), lambda i: (0, 0)),
                 pl.BlockSpec((1, LANES), lambda i: (0, 0))]
    return pl.pallas_call(
        functools.partial(_norm_router_kernel, n_exp),
        grid=(n_all // tm,), in_specs=in_specs,
        out_specs=[out_spec, pl.BlockSpec((tm * ns, LANES), lambda i: (i, 0)),
                   pl.BlockSpec((tm, LANES), lambda i: (i, 0))],
        out_shape=[out_shape, jax.ShapeDtypeStruct((n_all * ns, LANES), jnp.int32),
                   jax.ShapeDtypeStruct((n_all, LANES), F32)],
        compiler_params=_cp(("arbitrary",)), name="prenorm_router",
    )(*args, rw_hi, rw_lo, rbias)


def _mm_kernel(a_ref, w_ref, o_ref):
    o_ref[...] = _dot(a_ref[...], w_ref[...]).astype(o_ref.dtype)


def _matmul(a, w, out_dtype, m=None, tm=512):
    k = a.shape[1]
    m = a.shape[0] if m is None else m
    n = w.shape[1]
    tn = _pick(n, (1024, 512, 256, 128))
    return pl.pallas_call(
        _mm_kernel,
        grid=(n // tn, m // tm),
        in_specs=[pl.BlockSpec((tm, k), lambda j, i: (i, 0)),
                  pl.BlockSpec((k, tn), lambda j, i: (0, j))],
        out_specs=pl.BlockSpec((tm, tn), lambda j, i: (i, j)),
        out_shape=jax.ShapeDtypeStruct((m, n), out_dtype),
        compiler_params=_cp(("arbitrary", "arbitrary")),
        name="in_proj",
    )(a, w)


def _mm_w32_kernel(a_ref, w_ref, o_ref, wb_ref):
    @pl.when(pl.program_id(1) == 0)
    def _():
        wb_ref[...] = w_ref[...].astype(BF16)

    o_ref[...] = _dot(a_ref[...], wb_ref[...]).astype(o_ref.dtype)


def _matmul_w32(a, w_stack, layer, n, out_dtype, tm=512):
    m, k = a.shape
    tn = _pick(n, (1024, 512, 256, 128))
    return pl.pallas_call(
        _mm_w32_kernel,
        grid=(n // tn, m // tm),
        in_specs=[pl.BlockSpec((tm, k), lambda j, i: (i, 0)),
                  pl.BlockSpec((None, k, tn), lambda j, i: (layer, 0, j))],
        out_specs=pl.BlockSpec((tm, tn), lambda j, i: (i, j)),
        out_shape=jax.ShapeDtypeStruct((m, n), out_dtype),
        scratch_shapes=[pltpu.VMEM((k, tn), BF16)],
        compiler_params=_cp(("arbitrary", "arbitrary")),
        name="in_proj_mixers",
    )(a, w_stack)


def _hgrn_exact_chunk(q, k, bcum, v, st, reverse):
    c = HG_CHUNK
    last = 0 if reverse else c - 1
    b_last = bcum[last:last + 1]
    o_inter = _dot_nt((q * jnp.exp(bcum)).astype(BF16), st.astype(BF16))
    kd = k * jnp.exp(b_last - bcum)
    st_new = st * jnp.exp(b_last) + _dot_tn(v.astype(BF16), kd.astype(BF16))

    ones = jnp.ones((LANES, LANES), BF16)
    sub_row = lax.broadcasted_iota(jnp.int32, (HG_SUB, LANES), 0)
    v16 = v.astype(BF16)
    outs = []
    for i in range(c // HG_SUB):
        r0 = i * HG_SUB
        bsub = bcum[r0:r0 + HG_SUB]
        qsub = q[r0:r0 + HG_SUB]
        zs = []
        for s in range(HG_SUB):
            keep = (sub_row <= s) if reverse else (sub_row >= s)
            dl = jnp.where(keep, bsub - bcum[r0 + s:r0 + s + 1], NEG_BIG)
            zs.append(qsub * (k[r0 + s:r0 + s + 1] * jnp.exp(dl)))
        red = _dot(jnp.concatenate(zs, axis=0).astype(BF16), ones)
        o_sub = red[0:HG_SUB] * v[r0:r0 + 1]
        for s in range(1, HG_SUB):
            o_sub = o_sub + red[s * HG_SUB:(s + 1) * HG_SUB] * v[r0 + s:r0 + s + 1]
        lo, hi = (r0 + HG_SUB, c) if reverse else (0, r0)
        if hi > lo:
            ref_row = bcum[lo:lo + 1] if reverse else bcum[hi - 1:hi]
            qi = (qsub * jnp.exp(bsub - ref_row)).astype(BF16)
            kt = (k[lo:hi] * jnp.exp(ref_row - bcum[lo:hi])).astype(BF16)
            o_sub = o_sub + _dot(_dot_nt(qi, kt).astype(BF16), v16[lo:hi])
        outs.append(o_sub)
    return o_inter + jnp.concatenate(outs, axis=0), st_new


def _hgrn_fast_block(q_s, k_s, b_s, v_s, o_ref, st_ref, reverse):
    c, sub = HG_CHUNK, HG_FAST_SUB
    n_chunk = ROW_BLK // c
    order = list(range(n_chunk - 1, -1, -1) if reverse else range(n_chunk))
    last = 0 if reverse else c - 1
    qs, xs, eb, scores, vals = {}, {}, {}, {}, {}
    for ci in order:
        rows = pl.ds(ci * c, c)
        q, k, bcum, v = q_s[rows, :], k_s[rows, :], b_s[rows, :], v_s[rows, :]
        v16 = v.astype(BF16)
        b_last = bcum[last:last + 1]
        qs[ci] = (q * jnp.exp(bcum)).astype(BF16)
        eb[ci] = jnp.exp(b_last)
        xs[ci] = _dot_tn(v16, (k * jnp.exp(b_last - bcum)).astype(BF16))
        for i in range(c // sub):
            r0 = i * sub
            lo, hi = (r0, c) if reverse else (0, r0 + sub)
            ref_row = bcum[r0 + sub - 1:r0 + sub] if reverse else bcum[r0:r0 + 1]
            qi = (q[r0:r0 + sub] * jnp.exp(bcum[r0:r0 + sub] - ref_row)).astype(BF16)
            kt = (k[lo:hi] * jnp.exp(ref_row - bcum[lo:hi])).astype(BF16)
            scores[ci, i] = _dot_nt(qi, kt)
            vals[ci, i] = v16[lo:hi]
    outs = {}
    for ci in order:
        parts = []
        for i in range(c // sub):
            r0 = i * sub
            lo, hi = (r0, c) if reverse else (0, r0 + sub)
            key = lax.broadcasted_iota(jnp.int32, (sub, hi - lo), 1) + lo
            qry = lax.broadcasted_iota(jnp.int32, (sub, hi - lo), 0) + r0
            a = jnp.where((key >= qry) if reverse else (key <= qry), scores[ci, i], 0.0)
            parts.append(_dot(a.astype(BF16), vals[ci, i]))
        outs[ci] = jnp.concatenate(parts, axis=0)
    st = st_ref[...]
    for ci in order:
        o = outs[ci] + _dot_nt(qs[ci], st.astype(BF16))
        o_ref[pl.ds(ci * c, c), :] = o.astype(o_ref.dtype)
        st = st * eb[ci] + xs[ci]
    st_ref[...] = st


def _hgrn_kernel(layer, reverse, q_ref, f_ref, v_ref, lb_ref, o_ref, st_ref, q_s, k_s, b_s, v_s):
    @pl.when(pl.program_id(2) == 0)
    def _():
        st_ref[...] = jnp.zeros_like(st_ref)

    lg = lb_ref[0]
    e = jnp.exp(lg - jnp.max(lg, axis=0, keepdims=True))
    p = e / jnp.sum(e, axis=0, keepdims=True)
    lb = jnp.zeros((1, LANES), F32)
    for i in range(1, layer + 1):
        lb = lb + p[i:i + 1]

    qp = q_ref[...].astype(F32)
    fp = f_ref[...].astype(F32)
    q_s[...] = qp * _sigmoid(qp) * HEAD_W ** -0.5
    k_s[...] = (1.0 - lb) * _sigmoid(-fp)
    v_s[...] = v_ref[...].astype(F32)
    lf = jnp.log(jnp.maximum(lb + (1.0 - lb) * _sigmoid(fp), TINY))
    r_i = lax.broadcasted_iota(jnp.int32, (ROW_BLK, ROW_BLK), 0)
    c_i = lax.broadcasted_iota(jnp.int32, (ROW_BLK, ROW_BLK), 1)
    tri = (r_i // HG_CHUNK == c_i // HG_CHUNK) & ((c_i >= r_i) if reverse else (c_i <= r_i))
    tri = jnp.where(tri, 1.0, 0.0).astype(BF16)
    lf_hi, lf_lo = _split(lf)
    bcum = _dot(tri, lf_hi) + _dot(tri, lf_lo)
    b_s[...] = bcum

    spread = jnp.zeros((1, LANES), F32)
    for r0 in range(0, ROW_BLK, HG_FAST_SUB):
        spread = jnp.maximum(
            spread, jnp.abs(bcum[r0:r0 + 1] - bcum[r0 + HG_FAST_SUB - 1:r0 + HG_FAST_SUB]))
    small = jnp.max(spread) < HG_SPREAD_MAX

    @pl.when(small)
    def _():
        _hgrn_fast_block(q_s, k_s, b_s, v_s, o_ref, st_ref, reverse)

    @pl.when(jnp.logical_not(small))
    def _():
        n_chunk = ROW_BLK // HG_CHUNK
        st = st_ref[...]
        for ci in (range(n_chunk - 1, -1, -1) if reverse else range(n_chunk)):
            rows = pl.ds(ci * HG_CHUNK, HG_CHUNK)
            o, st = _hgrn_exact_chunk(q_s[rows, :], k_s[rows, :], b_s[rows, :], v_s[rows, :], st,
                                      reverse)
            o_ref[rows, :] = o.astype(o_ref.dtype)
        st_ref[...] = st


def _hgrn_scan(rows, p_main, lb_logits, layer, direction):
    reverse = direction == 1
    n_all = p_main.shape[0]
    steps = rows.nct + rows.nlt
    depth = lb_logits.shape[1]

    def spec(seg):
        return pl.BlockSpec((ROW_BLK, HEAD_W),
                            lambda b, h, s: (rows.scan_block(b, s, reverse), seg * HEADS + h))

    f_seg = S_HG_FB if reverse else S_HG_FF
    return pl.pallas_call(
        functools.partial(_hgrn_kernel, layer, reverse),
        grid=(rows.batch, HEADS, steps),
        in_specs=[spec(S_HG_Q), spec(f_seg), spec(S_HG_I),
                  pl.BlockSpec((1, depth, HEAD_W), lambda b, h, s: (direction, 0, h))],
        out_specs=pl.BlockSpec((ROW_BLK, HEAD_W),
                               lambda b, h, s: (rows.scan_block(b, s, reverse), h)),
        out_shape=jax.ShapeDtypeStruct((n_all, SEG), BF16),
        scratch_shapes=[pltpu.VMEM((HEAD_W, HEAD_W), F32)]
        + [pltpu.VMEM((ROW_BLK, HEAD_W), F32)] * 4,
        compiler_params=_cp(("arbitrary", "arbitrary", "arbitrary")),
        name="hgrn2_scan",
    )(p_main, p_main, p_main, lb_logits)


def _mlstm_kernel(*refs):
    c = ROW_BLK
    n_in = 7
    dirs = [dict(reverse=False, ins=refs[0:n_in], o=refs[2 * n_in], s=refs[2 * n_in + 2],
                 m=refs[2 * n_in + 3]),
            dict(reverse=True, ins=refs[n_in:2 * n_in], o=refs[2 * n_in + 1], s=refs[2 * n_in + 4],
                 m=refs[2 * n_in + 5])]

    @pl.when(pl.program_id(2) == 0)
    def _():
        for d in dirs:
            d['s'][...] = jnp.zeros_like(d['s'])
            d['m'][...] = jnp.zeros_like(d['m'])

    r_i = lax.broadcasted_iota(jnp.int32, (c, c), 0)
    c_i = lax.broadcasted_iota(jnp.int32, (c, c), 1)
    for d in dirs:
        q_ref, k_ref, v_ref, ig_ref, fg_ref, ib_ref, fb_ref = d['ins']
        d['q'] = q_ref[...]
        d['kt'] = (k_ref[...].astype(F32) * HEAD_W ** -0.5).T
        d['v_cat'] = jnp.concatenate([v_ref[...], jnp.ones((c, HEAD_W), BF16)], axis=1)
        d['ig'] = ig_ref[0] + ib_ref[0][:, :1]
        fx = fg_ref[0] + fb_ref[0][:, :1]
        lf = jnp.minimum(fx, 0.0) - jnp.log1p(jnp.exp(-jnp.abs(fx)))
        cum = jnp.where((r_i >= c_i) if d['reverse'] else (r_i <= c_i), 1.0, 0.0).astype(BF16)
        lf_hi, lf_lo = _split(jnp.broadcast_to(lf, (8, c)))
        d['brow'] = (_dot(lf_hi, cum) + _dot(lf_lo, cum))[0:1]
    for d in dirs:
        d['qk'] = _dot(d['q'], d['kt'].astype(BF16))
        d['state'] = d['s'][...]
        d['q_state'] = _dot(d['q'], d['state'].astype(BF16))
    for d in dirs:
        b_rows = jnp.broadcast_to(d['brow'], (c, c))
        b_cols = b_rows.T
        keep = (c_i >= r_i) if d['reverse'] else (c_i <= r_i)
        dlog = jnp.where(keep, b_cols - b_rows + d['ig'], NEG_BIG)
        d['m_prev'] = d['m'][:, :1]
        inter = b_cols[:, :1] + d['m_prev']
        d['m_t'] = jnp.maximum(jnp.max(dlog, axis=-1, keepdims=True), inter)
        d['w'] = (jnp.exp(dlog - d['m_t']) * d['qk']).astype(BF16)
        d['w_inter'] = jnp.exp(inter - d['m_t'])
    for d in dirs:
        nd = _dot(d['w'], d['v_cat']) + d['w_inter'] * d['q_state']
        num, den = nd[:, :HEAD_W], nd[:, HEAD_W:]
        d['o'][...] = (num / jnp.maximum(jnp.abs(den), jnp.exp(-d['m_t']))).astype(d['o'].dtype)
    for d in dirs:
        last = 0 if d['reverse'] else c - 1
        m_new = d['m_t'][last:last + 1]
        b_last = d['brow'][:, last:last + 1]
        wk = jnp.exp(b_last - d['brow'] + d['ig'] - m_new)
        dec = jnp.exp(b_last + d['m_prev'] - m_new)
        d['s'][...] = dec * d['state'] + _dot((d['kt'] * wk).astype(BF16), d['v_cat'])
        d['m'][...] = jnp.broadcast_to(m_new, d['m'].shape)


def _mlstm_scan(rows, p_main, gates_t, i_bias, f_bias):
    n_all = p_main.shape[0]
    steps = rows.nct + rows.nlt

    def dir_specs(direction):
        reverse = direction == 1
        blk = lambda b, s: rows.scan_block(b, s, reverse)
        spec = lambda seg: pl.BlockSpec((ROW_BLK, HEAD_W),
                                        lambda b, h, s: (blk(b, s), seg * HEADS + h))
        gate = lambda which: pl.BlockSpec((1, 1, ROW_BLK),
                                          lambda b, h, s: (which * HEADS + h, 0, blk(b, s)))
        bias = pl.BlockSpec((1, 1, LANES), lambda b, h, s: (direction * HEADS + h, 0, 0))
        ins = [spec(S_ML_Q), spec(S_ML_K), spec(S_ML_V), gate(direction), gate(2 + direction),
               bias, bias]
        return ins, pl.BlockSpec((ROW_BLK, HEAD_W), lambda b, h, s: (blk(b, s), h))

    ins_f, out_f = dir_specs(0)
    ins_b, out_b = dir_specs(1)
    out = jax.ShapeDtypeStruct((n_all, SEG), BF16)
    args = (p_main, p_main, p_main, gates_t, gates_t, i_bias, f_bias)
    return pl.pallas_call(
        _mlstm_kernel,
        grid=(rows.batch, HEADS, steps),
        in_specs=ins_f + ins_b,
        out_specs=[out_f, out_b],
        out_shape=[out, out],
        scratch_shapes=[pltpu.VMEM((HEAD_W, 2 * HEAD_W), F32), pltpu.VMEM((1, LANES), F32)] * 2,
        compiler_params=_cp(("arbitrary", "arbitrary", "arbitrary")),
        name="mlstm_scan",
    )(*args, *args)


def _da_prep_kernel(q_ref, k_ref, v_ref, cos_ref, sin_ref, qg_ref, kg_ref, qt_ref, ko_ref, vt_ref):
    r_i = lax.broadcasted_iota(jnp.int32, (LANES, LANES), 0)
    c_i = lax.broadcasted_iota(jnp.int32, (LANES, LANES), 1)
    blockdiag = jnp.where((r_i // DA_QK) == (c_i // DA_QK), 1.0 / DA_QK, 0.0).astype(BF16)
    cos, sin = cos_ref[...], sin_ref[...]
    lane = lax.broadcasted_iota(jnp.int32, cos.shape, 1)
    first_half = (lane % DA_QK) < (DA_QK // 2)

    def qk_norm_rope(x, gain):
        x_hi, x_lo = _split(x * x)
        ms = _dot(x_hi, blockdiag) + _dot(x_lo, blockdiag)
        y = x * lax.rsqrt(ms + EPS) * gain
        rot = jnp.where(first_half, -pltpu.roll(y, LANES - DA_QK // 2, 1),
                        pltpu.roll(y, DA_QK // 2, 1))
        return y * cos + rot * sin

    for h in range(HEADS):
        cols = slice(h * HEAD_W, (h + 1) * HEAD_W)
        q = qk_norm_rope(q_ref[:, cols].astype(F32), qg_ref[...]) * (DA_QK ** -0.5 * LOG2E)
        qt_ref[cols, :] = q.T.astype(BF16)
        ko_ref[:, cols] = qk_norm_rope(k_ref[:, cols].astype(F32), kg_ref[...]).astype(BF16)
        vt_ref[cols, :] = v_ref[:, cols].astype(F32).T.astype(BF16)


def _da_prep(rows, p_main, cos_tab, sin_tab, q_gain, k_gain):
    n_all = p_main.shape[0]
    tm = ROW_BLK
    out_t = jax.ShapeDtypeStruct((SEG, n_all), BF16)

    def tab_idx(i):
        lat = i % rows.nlt
        ctx = rows.nlt + (i - rows.batch * rows.nlt) % rows.nct
        return (jnp.where(i < rows.batch * rows.nlt, lat, ctx), 0)

    out = jax.ShapeDtypeStruct((n_all, SEG), BF16)
    return pl.pallas_call(
        _da_prep_kernel,
        grid=(n_all // tm,),
        in_specs=[pl.BlockSpec((tm, SEG), lambda i: (i, S_DA_Q)),
                  pl.BlockSpec((tm, SEG), lambda i: (i, S_DA_K)),
                  pl.BlockSpec((tm, SEG), lambda i: (i, S_DA_V)),
                  pl.BlockSpec((tm, LANES), tab_idx),
                  pl.BlockSpec((tm, LANES), tab_idx),
                  pl.BlockSpec((1, LANES), lambda i: (0, 0)),
                  pl.BlockSpec((1, LANES), lambda i: (0, 0))],
        out_specs=[pl.BlockSpec((SEG, tm), lambda i: (0, i)),
                   pl.BlockSpec((tm, SEG), lambda i: (i, 0)),
                   pl.BlockSpec((SEG, tm), lambda i: (0, i))],
        out_shape=[out_t, out, out_t],
        compiler_params=_cp(("arbitrary",)),
        name="da_qk_prep",
    )(p_main, p_main, p_main, cos_tab, sin_tab, q_gain, k_gain)


def _attn_kernel(lam_init, n_lat_q_blocks, n_lat_keys, tk, qt_ref, kc_ref, vct_ref, kl_ref, vlt_ref,
                 lam_ref, sn_ref, o_ref, acc_ref, m_ref, sa_ref, sb_ref, kn_ref):
    tq = qt_ref.shape[1]
    qt = qt_ref[...]
    chan = lax.broadcasted_iota(jnp.int32, qt.shape, 0)
    zero = jnp.zeros_like(qt)
    q2t = jnp.concatenate([jnp.where(chan < DA_QK, qt, zero), jnp.where(chan >= DA_QK, qt, zero)],
                          axis=1)

    def weighted_values(vt, st, m):
        p = jnp.exp2((st - m).astype(BF16))
        ones = jnp.ones((ATTN_PAD_ROWS, vt.shape[1]), BF16)
        return _dot(jnp.concatenate([vt, ones], axis=0), p)

    st = _dot(kc_ref[...], q2t)
    m0 = jnp.max(st, axis=0, keepdims=True)
    m_ref[...] = m0
    acc_ref[...] = weighted_values(vct_ref[...], st, m0)

    n_chunks = n_lat_keys // tk
    unroll = _pick(n_chunks, (4, 2, 1))
    bufs = (sa_ref, sb_ref) if unroll > 1 else (sa_ref, sa_ref)

    def chunk(c):
        return pl.ds(pl.multiple_of(c * tk, tk), tk)

    def scores(c, buf):
        st = _dot(kl_ref[chunk(c), :], q2t)
        buf[...] = st
        return jnp.max(st, axis=0, keepdims=True)

    def absorb(c, buf, m_cur):
        m_old = m_ref[...]
        m_new = jnp.maximum(m_old, m_cur)
        pv = weighted_values(vlt_ref[:, chunk(c)], buf[...], m_new)
        acc_ref[...] = jnp.exp2(m_old - m_new) * acc_ref[...] + pv
        m_ref[...] = m_new

    def body(j, m_pend):
        for u in range(unroll):
            c = j * unroll + u
            if unroll > 1:
                m_next = scores(jnp.minimum(c + 1, n_chunks - 1), bufs[(u + 1) % 2])
                absorb(c, bufs[u % 2], m_pend)
            else:
                absorb(c, bufs[0], m_pend)
                m_next = scores(jnp.minimum(c + 1, n_chunks - 1), bufs[0])
            m_pend = m_next
        return m_pend

    n_trips = jnp.where(pl.program_id(2) < n_lat_q_blocks, n_chunks // unroll, 0)

    @pl.when(pl.program_id(2) == 0)
    def _():
        r_i = lax.broadcasted_iota(jnp.int32, (LANES, LANES), 0)
        c_i = lax.broadcasted_iota(jnp.int32, (LANES, LANES), 1)
        same_map = jnp.where((r_i // DA_QK) == (c_i // DA_QK), 1.0, 0.0).astype(BF16)

        def max_sq_norm(k):
            kf = k.astype(F32)
            hi, lo = _split(kf * kf)
            return jnp.max(_dot(hi, same_map) + _dot(lo, same_map), axis=0, keepdims=True)

        def body_norm(c, best):
            return jnp.maximum(best, max_sq_norm(kl_ref[chunk(c), :]))
        kn_ref[...] = lax.fori_loop(0, n_chunks, body_norm, max_sq_norm(kc_ref[...]))

    qf = q2t.astype(F32)
    q_sq = jnp.sum(qf * qf, axis=0, keepdims=True)
    col = lax.broadcasted_iota(jnp.int32, q_sq.shape, 1)
    k_sq = jnp.where(col < tq, kn_ref[:, 0:1], kn_ref[:, DA_QK:DA_QK + 1])
    bound = jnp.sqrt(q_sq * k_sq) * 1.01 + 0.01
    bounded = jnp.max(bound - m0) <= ATTN_RANGE_MAX

    def body_bounded(j, carry):
        for u in range(unroll):
            c = j * unroll + u
            st = _dot(kl_ref[chunk(c), :], q2t)
            acc_ref[...] += weighted_values(vlt_ref[:, chunk(c)], st, m0)
        return carry

    @pl.when(bounded)
    def _():
        lax.fori_loop(0, n_trips, body_bounded, 0)

    @pl.when(jnp.logical_not(bounded))
    def _():
        lax.fori_loop(0, n_trips, body, scores(0, bufs[0]))

    lv = lam_ref[...]
    lam = (jnp.exp(jnp.sum(lv[0:1] * lv[1:2], axis=-1, keepdims=True))
           - jnp.exp(jnp.sum(lv[2:3] * lv[3:4], axis=-1, keepdims=True)) + lam_init)
    o = acc_ref[0:HEAD_W, :] / acc_ref[HEAD_W:HEAD_W + 1, :]
    a = o[:, :tq] - lam * o[:, tq:]
    ms = jnp.mean(a * a, axis=0, keepdims=True)
    y = a * lax.rsqrt(ms + EPS) * sn_ref[...] * (1.0 - lam_init)
    o_ref[...] = y.T.astype(o_ref.dtype)


def _attention_call(rows, qt, kh, vt, lam_vec, sub_gain, lam_init, n_rows, prev):
    assert rows.nct == 1
    ctx_blk = lambda b: rows.batch * rows.nlt + b
    tk = _pick(rows.seq, (512, 256))
    if prev is None:
        tq = _pick(rows.seq, (512, 256))
        n_q = rows.seq // tq
        q_blk = lambda b, i: b * n_q + i
        kern = functools.partial(_attn_kernel, lam_init, n_q, rows.seq, tk)
        extra_specs, extra_args, aliases = [], [], {}
    else:
        tq, n_q = ROW_BLK, 0
        q_blk = lambda b, i: ctx_blk(b)
        attn = functools.partial(_attn_kernel, lam_init, n_q, rows.seq, tk)
        kern = lambda *refs: attn(*refs[:7], *refs[8:])
        extra_specs, extra_args, aliases = [pl.BlockSpec(memory_space=pl.ANY)], [prev], {7: 0}
    return pl.pallas_call(
        kern,
        grid=(rows.batch, HEADS, max(n_q, 1)),
        in_specs=[pl.BlockSpec((HEAD_W, tq), lambda b, h, i: (h, q_blk(b, i))),
                  pl.BlockSpec((ROW_BLK, HEAD_W), lambda b, h, i: (ctx_blk(b), h)),
                  pl.BlockSpec((HEAD_W, ROW_BLK), lambda b, h, i: (h, ctx_blk(b))),
                  pl.BlockSpec((rows.seq, HEAD_W), lambda b, h, i: (b, h)),
                  pl.BlockSpec((HEAD_W, rows.seq), lambda b, h, i: (h, b)),
                  pl.BlockSpec(lam_vec.shape, lambda b, h, i: (0, 0)),
                  pl.BlockSpec((HEAD_W, 1), lambda b, h, i: (0, 0))] + extra_specs,
        out_specs=pl.BlockSpec((tq, HEAD_W), lambda b, h, i: (q_blk(b, i), h)),
        out_shape=jax.ShapeDtypeStruct((n_rows, SEG), BF16),
        scratch_shapes=[pltpu.VMEM((HEAD_W + ATTN_PAD_ROWS, 2 * tq), F32),
                        pltpu.VMEM((1, 2 * tq), F32), pltpu.VMEM((tk, 2 * tq), F32),
                        pltpu.VMEM((tk, 2 * tq), F32), pltpu.VMEM((1, LANES), F32)],
        input_output_aliases=aliases,
        compiler_params=_cp(("arbitrary", "arbitrary", "arbitrary")),
        name="diff_attention" if prev is None else "diff_attention_ctx",
    )(qt, kh, vt, kh, vt, lam_vec, sub_gain, *extra_args)


def _attention(rows, qt, kh, vt, lam_vec, sub_gain, lam_init, n_rows):
    da = _attention_call(rows, qt, kh, vt, lam_vec, sub_gain, lam_init, n_rows, None)
    if n_rows > rows.n_lat:
        da = _attention_call(rows, qt, kh, vt, lam_vec, sub_gain, lam_init, n_rows, da)
    return da


def _merge_kernel(ohf_ref, ohb_ref, hg_ref, da_ref, omf_ref, omb_ref, mo_ref, g0_ref, g1_ref,
                  g2_ref, hn_ref, mn_ref, wb_ref, o_ref, h_ref):
    @pl.when(pl.program_id(1) == 0)
    def _():
        for h in range(HEADS):
            cols = slice(h * HEAD_W, (h + 1) * HEAD_W)
            o = ohf_ref[:, cols].astype(F32) + ohb_ref[:, cols].astype(F32)
            y = o * lax.rsqrt(jnp.mean(o * o, axis=-1, keepdims=True) + EPS) * hn_ref[...]
            g = hg_ref[:, cols].astype(F32)
            h_ref[0, :, cols] = (y * (g * _sigmoid(g))).astype(BF16)
            o = omf_ref[:, cols].astype(F32) + omb_ref[:, cols].astype(F32)
            y = o * lax.rsqrt(jnp.mean(o * o, axis=-1, keepdims=True) + EPS) * mn_ref[...]
            h_ref[2, :, cols] = (y * _sigmoid(mo_ref[:, cols].astype(F32))).astype(BF16)
        h_ref[1] = da_ref[...]

    y = _sigmoid(g0_ref[...].astype(F32)) * _dot(h_ref[0], wb_ref[0])
    y = y + _sigmoid(g1_ref[...].astype(F32)) * _dot(h_ref[1], wb_ref[1])
    y = y + _sigmoid(g2_ref[...].astype(F32)) * _dot(h_ref[2], wb_ref[2])
    o_ref[...] = y.astype(o_ref.dtype)


def _merge(p_main, p_merge, ohf, ohb, da, omf, omb, hg_gain, ml_gain, w_branch, d):
    n_all = da.shape[0]
    tm = 512
    tn = _pick(d, (512, 256, 128))
    row = lambda i, j: (i, 0)

    def gate_spec(jj):
        return pl.BlockSpec((tm, tn), lambda i, j: (i, jj * (d // tn) + j))

    return pl.pallas_call(
        _merge_kernel,
        grid=(n_all // tm, d // tn),
        in_specs=[pl.BlockSpec((tm, SEG), row), pl.BlockSpec((tm, SEG), row),
                  pl.BlockSpec((tm, SEG), lambda i, j: (i, S_HG_G)),
                  pl.BlockSpec((tm, SEG), row),
                  pl.BlockSpec((tm, SEG), row), pl.BlockSpec((tm, SEG), row),
                  pl.BlockSpec((tm, SEG), lambda i, j: (i, S_ML_O)),
                  gate_spec(0), gate_spec(1), gate_spec(2),
                  pl.BlockSpec((1, LANES), lambda i, j: (0, 0)),
                  pl.BlockSpec((1, LANES), lambda i, j: (0, 0)),
                  pl.BlockSpec((3, SEG, tn), lambda i, j: (0, 0, j))],
        out_specs=pl.BlockSpec((tm, tn), lambda i, j: (i, j)),
        out_shape=jax.ShapeDtypeStruct((n_all, d), BF16),
        scratch_shapes=[pltpu.VMEM((3, tm, SEG), BF16)],
        compiler_params=_cp(("arbitrary", "arbitrary")),
        name="branch_merge",
    )(ohf, ohb, p_main, da, omf, omb, p_main, p_merge, p_merge, p_merge, hg_gain, ml_gain,
      w_branch)


def _proj_resid_kernel(n_first, a_ref, w_ref, g_ref, *refs):
    x_parts, o_ref = refs[:-1], refs[-1]
    x = _row_tile(x_parts, pl.program_id(1), n_first)
    o_ref[...] = x + g_ref[0] * _dot(a_ref[...], w_ref[...])


def _proj_resid(rows, a, w, x_all, modsflat, layer, which_gate):
    n_all, k = a.shape
    d = w.shape[1]
    tm = 512
    tn = _pick(d, (1024, 512, 256, 128))
    x_specs, x_args, n_first = _split_rows(x_all, (tm, tn), 1)
    return pl.pallas_call(
        functools.partial(_proj_resid_kernel, n_first),
        grid=(d // tn, n_all // tm),
        in_specs=[pl.BlockSpec((tm, k), lambda j, i: (i, 0)),
                  pl.BlockSpec((k, tn), lambda j, i: (0, j)),
                  pl.BlockSpec((1, 1, tn),
                               lambda j, i: (rows.mod_row(layer, i, tm, which_gate), 0, j))]
        + x_specs,
        out_specs=pl.BlockSpec((tm, tn), lambda j, i: (i, j)),
        out_shape=jax.ShapeDtypeStruct((n_all, d), F32),
        compiler_params=_cp(("arbitrary", "arbitrary")),
        name="out_proj_residual",
    )(a, w, modsflat, *x_args)


def _shared_expert_kernel(tok_ref, w1_ref, w3_ref, w2_ref, o_ref):
    tok = tok_ref[...]
    h1 = _dot(tok, w1_ref[...])
    h3 = _dot(tok, w3_ref[...])
    a = (h1 * _sigmoid(h1)) * h3
    o_ref[...] = _dot(a.astype(BF16), w2_ref[...]).astype(o_ref.dtype)


def _shared_expert(tok, w1, w3, w2):
    n_all, d = tok.shape
    de = w1.shape[1]
    tm = 512
    full = lambda i: (0, 0)
    return pl.pallas_call(
        _shared_expert_kernel,
        grid=(n_all // tm,),
        in_specs=[pl.BlockSpec((tm, d), lambda i: (i, 0)),
                  pl.BlockSpec((d, de), full), pl.BlockSpec((d, de), full),
                  pl.BlockSpec((de, d), full)],
        out_specs=pl.BlockSpec((tm, d), lambda i: (i, 0)),
        out_shape=jax.ShapeDtypeStruct((n_all, d), BF16),
        compiler_params=_cp(("arbitrary",)),
        name="moe_shared_expert",
    )(tok, w1, w3, w2)


def _moe_plan_kernel(route_ref, pos_ref, cnt_ref, carry_ref):
    @pl.when(pl.program_id(0) == 0)
    def _():
        carry_ref[...] = jnp.zeros_like(carry_ref)

    r = route_ref[...]
    tm = r.shape[0]
    lane = lax.broadcasted_iota(jnp.int32, r.shape, 1).astype(F32)
    mask = jnp.zeros_like(r)
    for k in range(TOP_K):
        mask = mask + jnp.where(lane == r[:, k:k + 1], 1.0, 0.0)
    r_i = lax.broadcasted_iota(jnp.int32, (tm, tm), 0)
    c_i = lax.broadcasted_iota(jnp.int32, (tm, tm), 1)
    before = jnp.where(c_i < r_i, 1.0, 0.0).astype(BF16)
    rank = _dot(before, mask.astype(BF16)) + carry_ref[...]
    out = jnp.zeros_like(r)
    for k in range(TOP_K):
        pk = jnp.sum(jnp.where(lane == r[:, k:k + 1], rank, 0.0), axis=-1, keepdims=True)
        out = jnp.where(lane == k, pk, out)
    pos_ref[...] = out
    carry = carry_ref[...] + jnp.sum(mask, axis=0, keepdims=True)
    carry_ref[...] = carry
    cnt_ref[...] = jnp.broadcast_to(carry, cnt_ref.shape)


def _moe_plan(route):
    n = route.shape[0]
    tm = ROW_BLK
    return pl.pallas_call(
        _moe_plan_kernel,
        grid=(n // tm,),
        in_specs=[pl.BlockSpec((tm, LANES), lambda i: (i, 0))],
        out_specs=[pl.BlockSpec((tm, LANES), lambda i: (i, 0)),
                   pl.BlockSpec((8, LANES), lambda i: (0, 0))],
        out_shape=[jax.ShapeDtypeStruct((n, LANES), F32), jax.ShapeDtypeStruct((8, LANES), F32)],
        scratch_shapes=[pltpu.VMEM((1, LANES), F32)],
        compiler_params=_cp(("arbitrary",)),
        name="moe_plan",
    )(route)


def _moe_routed_kernel(ns, te_ref, na_ref, st_ref, tok_hbm, sw_ref, w1_ref, w3_ref, w2_ref, o_ref,
                       xbuf, x_ref, sem):
    tm = MOE_TILE
    t = pl.program_id(0)
    n_active = na_ref[0]
    par = t % 2

    def gather_copy(tile, buf, r):
        tok = st_ref[tile * tm + r]
        return pltpu.make_async_copy(
            tok_hbm.at[pl.ds(pl.multiple_of(tok * ns, ns), ns), :],
            xbuf.at[buf, pl.ds(pl.multiple_of(r * ns, ns), ns), :], sem.at[buf])

    def for_each_copy(tile, buf, act):
        def body(r4, carry):
            for u in range(4):
                act(gather_copy(tile, buf, r4 * 4 + u))
            return carry
        lax.fori_loop(0, tm // 4, body, 0)

    @pl.when(t == 0)
    def _():
        for_each_copy(0, 0, lambda cp: cp.start())

    @pl.when(t + 1 < n_active)
    def _():
        for_each_copy(t + 1, 1 - par, lambda cp: cp.start())

    @pl.when(t < n_active)
    def _():
        for_each_copy(t, par, lambda cp: cp.wait())
        d = x_ref.shape[1]

        def emit(s, lo, hi):
            x_ref[:, s * LANES:(s + 1) * LANES] = lo.astype(BF16)
            x_ref[:, d // 2 + s * LANES:d // 2 + (s + 1) * LANES] = hi.astype(BF16)
        _load_slabs(xbuf.at[par], 0, tm, ns, emit)
        x = x_ref[...]
        h1 = _dot(x, w1_ref[...])
        h3 = _dot(x, w3_ref[...])
        a = (h1 * _sigmoid(h1)) * h3 * sw_ref[...]
        _store_slabs(_dot(a.astype(BF16), w2_ref[...]), o_ref)

    @pl.when(t >= n_active)
    def _():
        o_ref[...] = jnp.zeros_like(o_ref)


def _moe_routed(tok_slabs, slot_token, slot_w, tile_expert, n_active, w1, w3, w2):
    n_exp, d, de = w1.shape
    ns = d // (2 * LANES)
    tm = MOE_TILE
    n_slots = slot_token.shape[0]
    nt = n_slots // tm
    grid_spec = pltpu.PrefetchScalarGridSpec(
        num_scalar_prefetch=3,
        grid=(nt,),
        in_specs=[pl.BlockSpec(memory_space=pl.ANY),
                  pl.BlockSpec((tm, 1), lambda t, te, na, st: (t, 0)),
                  pl.BlockSpec((None, d, de), lambda t, te, na, st: (te[t], 0, 0)),
                  pl.BlockSpec((None, d, de), lambda t, te, na, st: (te[t], 0, 0)),
                  pl.BlockSpec((None, de, d), lambda t, te, na, st: (te[t], 0, 0))],
        out_specs=pl.BlockSpec((tm * ns, LANES), lambda t, te, na, st: (t, 0)),
        scratch_shapes=[pltpu.VMEM((2, tm * ns, LANES), jnp.int32), pltpu.VMEM((tm, d), BF16),
                        pltpu.SemaphoreType.DMA((2,))],
    )
    return pl.pallas_call(
        functools.partial(_moe_routed_kernel, ns),
        grid_spec=grid_spec,
        out_shape=jax.ShapeDtypeStruct((n_slots * ns, LANES), jnp.int32),
        compiler_params=_cp(("arbitrary",)),
        name="moe_routed_experts",
    )(tile_expert, n_active, slot_token, tok_slabs, slot_w, w1, w3, w2)


def _moe_combine_kernel(ns, s4_ref, ys_hbm, ysh_ref, x_ref, g_ref, o_ref, ybuf, sem):
    tm = x_ref.shape[0]
    d = x_ref.shape[1]
    i = pl.program_id(0)
    par = i % 2

    def gather_copy(tile, buf, row, k):
        slot = s4_ref[(tile * tm + row) * TOP_K + k]
        return pltpu.make_async_copy(
            ys_hbm.at[pl.ds(pl.multiple_of(slot * ns, ns), ns), :],
            ybuf.at[buf, pl.ds(pl.multiple_of((k * tm + row) * ns, ns), ns), :], sem.at[buf])

    def for_each_copy(tile, buf, act):
        def body(row, carry):
            for k in range(TOP_K):
                act(gather_copy(tile, buf, row, k))
            return carry
        lax.fori_loop(0, tm, body, 0)

    @pl.when(i == 0)
    def _():
        for_each_copy(0, 0, lambda cp: cp.start())

    @pl.when(i + 1 < pl.num_programs(0))
    def _():
        for_each_copy(i + 1, 1 - par, lambda cp: cp.start())

    for_each_copy(i, par, lambda cp: cp.wait())

    gate = g_ref[0]
    for s in range(ns):
        lo_cols = slice(s * LANES, (s + 1) * LANES)
        hi_cols = slice(d // 2 + s * LANES, d // 2 + (s + 1) * LANES)
        y_lo = ysh_ref[:, lo_cols].astype(F32)
        y_hi = ysh_ref[:, hi_cols].astype(F32)
        for k in range(TOP_K):
            lo, hi = _unpack_pair(ybuf[par, pl.ds(k * tm * ns + s, tm, stride=ns), :])
            y_lo = y_lo + lo
            y_hi = y_hi + hi
        o_ref[:, lo_cols] = x_ref[:, lo_cols] + gate[:, lo_cols] * y_lo
        o_ref[:, hi_cols] = x_ref[:, hi_cols] + gate[:, hi_cols] * y_hi


def _moe_combine(rows, x_all, y_shared, ys_slabs, slot4, modsflat, layer, which_gate, n_rows):
    d = x_all.shape[1]
    ns = d // (2 * LANES)
    tm = 128
    grid_spec = pltpu.PrefetchScalarGridSpec(
        num_scalar_prefetch=1,
        grid=(n_rows // tm,),
        in_specs=[pl.BlockSpec(memory_space=pl.ANY),
                  pl.BlockSpec((tm, d), lambda i, s4: (i, 0)),
                  pl.BlockSpec((tm, d), lambda i, s4: (i, 0)),
                  pl.BlockSpec((1, 1, d),
                               lambda i, s4: (rows.mod_row(layer, i, tm, which_gate), 0, 0))],
        out_specs=pl.BlockSpec((tm, d), lambda i, s4: (i, 0)),
        scratch_shapes=[pltpu.VMEM((2, TOP_K * tm * ns, LANES), jnp.int32),
                        pltpu.SemaphoreType.DMA((2,))],
    )
    return pl.pallas_call(
        functools.partial(_moe_combine_kernel, ns),
        grid_spec=grid_spec,
        out_shape=jax.ShapeDtypeStruct((n_rows, d), F32),
        compiler_params=_cp(("arbitrary",)),
        name="moe_combine_residual",
    )(slot4, ys_slabs, y_shared, x_all, modsflat)


def _moe_slots(route, pos, counts, n_exp):
    n = route.shape[0]
    tm = MOE_TILE
    idx4 = route[:, :TOP_K].astype(jnp.int32)
    w4 = route[:, TOP_K:2 * TOP_K]
    pos4 = pos[:, :TOP_K].astype(jnp.int32)
    cnt = counts[0, :n_exp].astype(jnp.int32)
    padded = (cnt + tm - 1) // tm * tm
    ends = jnp.cumsum(padded)
    starts = ends - padded
    slot4 = (starts[idx4] + pos4).reshape(-1)
    n_slots = n * TOP_K + n_exp * tm
    nt = n_slots // tm
    tile_expert = jnp.minimum(
        jnp.searchsorted(ends, jnp.arange(nt, dtype=jnp.int32) * tm, side='right'),
        n_exp - 1).astype(jnp.int32)
    n_active = (ends[-1:] // tm).astype(jnp.int32)
    token_of = jnp.repeat(jnp.arange(n, dtype=jnp.int32), TOP_K)
    pairs = jnp.stack([token_of, lax.bitcast_convert_type(w4.reshape(-1), jnp.int32)], axis=1)
    slot_meta = jnp.zeros((n_slots, 2), jnp.int32).at[slot4].set(pairs)
    slot_token = slot_meta[:, 0]
    slot_w = lax.bitcast_convert_type(slot_meta[:, 1:2], F32)
    return slot4, slot_token, slot_w, tile_expert, n_active


def _rope_tables(seq, ctx_len):
    n = DA_QK // 4
    t = jnp.arange(seq)
    inv = ROPE_THETA ** (-jnp.arange(n, dtype=F32) / n)
    row = (t // GRID_W).astype(F32)
    col = (t % GRID_W).astype(F32)
    ang = jnp.concatenate([row[:, None] * inv, col[:, None] * inv], axis=-1)
    ang = jnp.concatenate([ang, jnp.zeros((ctx_len, 2 * n), F32)], axis=0)
    return jnp.tile(jnp.cos(ang), (1, 4)), jnp.tile(jnp.sin(ang), (1, 4))


def kernel(x, c, ctx, c_ctx, norm1, norm2, w_ada, b_ada, w_in, hg_lb_logits, hg_norm, da_q_norm,
           da_k_norm, da_lambda, da_sub_norm, ml_igate_bias, ml_fgate_bias, ml_norm, w_branch,
           w_out, router_w, router_bias, exp_w1, exp_w3, exp_w2, sh_w1, sh_w3, sh_w2):
    batch, seq, d = x.shape
    ctx_len = ctx.shape[1]
    depth = w_ada.shape[0]
    n_exp = router_w.shape[-1]
    rows = _Rows(batch, seq, ctx_len)
    assert batch + 1 <= 16 and n_exp <= LANES

    x_all = (x.reshape(batch * seq, d), ctx.reshape(batch * ctx_len, d))
    cvec = jnp.zeros((16, d), F32).at[0].set(c_ctx).at[1:1 + batch].set(c)
    mods = _mods(cvec, w_ada, b_ada)
    modsflat = mods[:, :rows.n_groups].reshape(depth * rows.n_groups * 6, 1, d)
    cos_tab, sin_tab = _rope_tables(seq, ctx_len)
    gate_lo = N_SEG * SEG

    for l in range(depth):
        n_rows = rows.n_all if l < depth - 1 else rows.n_lat
        lam_init = 0.8 - 0.6 * math.exp(-0.3 * l)
        w_merge = w_in[l, :, gate_lo + N_GATE_COLS:].astype(BF16)
        w_gate = jnp.pad(w_in[l, :, gate_lo:gate_lo + N_GATE_COLS],
                         ((0, 0), (0, LANES - N_GATE_COLS))).astype(BF16)

        hx = _norm(rows, x_all, rows.n_all, norm1[l], modsflat, l, 1, 0)
        p_main = _matmul_w32(hx, w_in, l, gate_lo, BF16)
        p_merge = _matmul(hx, w_merge, BF16, m=n_rows)
        p_gate = _matmul(hx, w_gate, F32)
        gates_t = p_gate[:, :N_GATE_COLS].T.reshape(N_GATE_COLS, 1, rows.n_all)

        ohf = _hgrn_scan(rows, p_main, hg_lb_logits, l, 0)
        ohb = _hgrn_scan(rows, p_main, hg_lb_logits, l, 1)

        tile2 = lambda g: jnp.tile(g.reshape(1, DA_QK), (1, 2))
        qt, kh, vt = _da_prep(rows, p_main, cos_tab, sin_tab, tile2(da_q_norm[l]),
                              tile2(da_k_norm[l]))
        da = _attention(rows, qt, kh, vt, da_lambda[l], da_sub_norm[l].reshape(HEAD_W, 1),
                        lam_init, n_rows)

        bias = lambda bv: jnp.broadcast_to(bv.reshape(2 * HEADS, 1, 1), (2 * HEADS, 1, LANES))
        i_b, f_b = bias(ml_igate_bias[l]), bias(ml_fgate_bias[l])
        omf, omb = _mlstm_scan(rows, p_main, gates_t, i_b, f_b)

        ymid = _merge(p_main, p_merge, ohf, ohb, da, omf, omb, hg_norm[l].reshape(1, HEAD_W),
                      ml_norm[l].reshape(1, HEAD_W), w_branch[l].astype(BF16), d)
        x_all = _proj_resid(rows, ymid, w_out[l].astype(BF16), x_all, modsflat, l, 2)

        rw = jnp.pad(router_w[l], ((0, 0), (0, LANES - n_exp)))
        rw_hi = rw.astype(BF16)
        rw_lo = (rw - rw_hi.astype(F32)).astype(BF16)
        rb = jnp.pad(router_bias[l], (0, LANES - n_exp)).reshape(1, LANES)
        tok, tok_slabs, route = _norm(rows, x_all, n_rows, norm2[l], modsflat, l, 4, 3,
                                      router=(rw_hi, rw_lo, rb, n_exp))
        pos, counts = _moe_plan(route)
        slot4, slot_token, slot_w, tile_expert, n_active = _moe_slots(route, pos, counts, n_exp)
        ys = _moe_routed(tok_slabs, slot_token, slot_w, tile_expert, n_active,
                         exp_w1[l].astype(BF16), exp_w3[l].astype(BF16), exp_w2[l].astype(BF16))
        y_sh = _shared_expert(tok, sh_w1[l].astype(BF16), sh_w3[l].astype(BF16),
                              sh_w2[l].astype(BF16))
        x_all = _moe_combine(rows, x_all, y_sh, ys, slot4, modsflat, l, 5, n_rows)

    return x_all.reshape(batch, seq, d)
```

```python
import functools
import math

import jax
import jax.numpy as jnp
from jax import lax
from jax.experimental import pallas as pl
from jax.experimental.pallas import tpu as pltpu

F32 = jnp.float32
BF16 = jnp.bfloat16

EPS = 1e-6
NEG_BIG = -1e30
TINY = 1e-30
GRID_W = 64
ROPE_THETA = 10000.0
ROUTED_SCALE = 2.5
TOP_K = 4
LOG2E = 1.4426950408889634

HEADS = 12
HEAD_W = 128
SEG = HEADS * HEAD_W
DA_QK = 64
N_GATE_COLS = 4 * HEADS

S_HG_Q, S_HG_FF, S_HG_FB, S_HG_I, S_HG_G = 0, 1, 2, 3, 4
S_DA_Q, S_DA_K, S_DA_V = 5, 6, 7
S_ML_Q, S_ML_K, S_ML_V, S_ML_O = 8, 9, 10, 11
N_SEG = 12
N_BRANCH = 3

LANES = 128
SUBLANES = 8
ROW_BLK = 256
HG_CHUNK = 64
HG_SUB = 16
HG_FAST_SUB = 32
HG_SPREAD_MAX = 60.0
MOE_TILE = 256
ATTN_PAD_ROWS = 16
ATTN_RANGE_MAX = 60.0
VMEM_LIMIT = 56 * 1024 * 1024


def _cp(sem, vmem=VMEM_LIMIT):
    return pltpu.CompilerParams(dimension_semantics=sem, vmem_limit_bytes=vmem)


def _pick(n, cands):
    for c in cands:
        if n % c == 0:
            return c
    raise ValueError(f"no tile for {n} in {cands}")


def _dot(a, b):
    return jnp.dot(a, b, preferred_element_type=F32)


def _dot_nt(a, b):
    return lax.dot_general(a, b, (((1,), (1,)), ((), ())), preferred_element_type=F32)


def _dot_tn(a, b):
    return lax.dot_general(a, b, (((0,), (0,)), ((), ())), preferred_element_type=F32)


def _split(x):
    hi = x.astype(BF16)
    lo = (x - hi.astype(F32)).astype(BF16)
    return hi, lo


def _sigmoid(x):
    return 1.0 / (1.0 + jnp.exp(-x))


class _Rows:
    def __init__(self, batch, seq, ctx_len):
        self.batch, self.seq, self.ctx = batch, seq, ctx_len
        assert seq % ROW_BLK == 0 and ctx_len % ROW_BLK == 0
        self.nlt = seq // ROW_BLK
        self.nct = ctx_len // ROW_BLK
        self.n_lat = batch * seq
        self.n_all = batch * (seq + ctx_len)
        self.n_groups = batch + 1

    def group(self, i, tm):
        lat_tiles = self.n_lat // tm
        return jnp.where(i < lat_tiles, 1 + i // (self.seq // tm), 0)

    def mod_row(self, layer, i, tm, which):
        return (layer * self.n_groups + self.group(i, tm)) * 6 + which

    def scan_block(self, b, s, reverse):
        if reverse:
            cblk = self.batch * self.nlt + b * self.nct + (self.nct - 1 - s)
            lblk = b * self.nlt + (self.nlt - 1 - (s - self.nct))
        else:
            cblk = self.batch * self.nlt + b * self.nct + s
            lblk = b * self.nlt + (s - self.nct)
        return jnp.where(s < self.nct, cblk, lblk)


def _mods_kernel(c_ref, w_ref, b_ref, o_ref):
    c = c_ref[...]
    s_hi, s_lo = _split(c * _sigmoid(c))
    w_hi, w_lo = _split(w_ref[...])
    acc = _dot(s_hi, w_hi) + _dot(s_lo, w_hi) + _dot(s_hi, w_lo)
    o_ref[...] = acc + b_ref[...]


def _mods(cvec, w_ada, b_ada):
    depth, d, n = w_ada.shape
    tn = _pick(n, (512, 256, 128))
    rows = cvec.shape[0]
    return pl.pallas_call(
        _mods_kernel,
        grid=(depth, n // tn),
        in_specs=[pl.BlockSpec((rows, d), lambda l, j: (0, 0)),
                  pl.BlockSpec((None, d, tn), lambda l, j: (l, 0, j)),
                  pl.BlockSpec((None, 1, tn), lambda l, j: (l, 0, j))],
        out_specs=pl.BlockSpec((None, rows, tn), lambda l, j: (l, 0, j)),
        out_shape=jax.ShapeDtypeStruct((depth, rows, n), F32),
        compiler_params=_cp(("arbitrary", "arbitrary")),
        name="adaln_mods",
    )(cvec, w_ada, b_ada.reshape(depth, 1, n))


def _split_rows(x, block, tile_axis):
    if not isinstance(x, tuple):
        return [pl.BlockSpec(block, lambda *g: (g[tile_axis], _other(g, tile_axis)))], [x], None
    n_first = x[0].shape[0] // block[0]
    first = pl.BlockSpec(block, lambda *g: (jnp.minimum(g[tile_axis], n_first - 1),
                                            _other(g, tile_axis)))
    second = pl.BlockSpec(block, lambda *g: (jnp.maximum(g[tile_axis] - n_first, 0),
                                             _other(g, tile_axis)))
    return [first, second], list(x), n_first


def _other(g, tile_axis):
    return g[1 - tile_axis] if len(g) == 2 else 0


def _row_tile(parts, i, n_first):
    if len(parts) == 1:
        return parts[0][...]
    return jnp.where(i < n_first, parts[0][...], parts[1][...])


def _norm_mod(x, g_ref, sc_ref, sh_ref):
    ms = jnp.mean(x * x, axis=-1, keepdims=True)
    y = x * lax.rsqrt(ms + EPS) * g_ref[...]
    return y * (1.0 + sc_ref[0]) + sh_ref[0]


def _norm_kernel(n_first, *refs):
    x_parts, (g_ref, sc_ref, sh_ref, o_ref) = refs[:-4], refs[-4:]
    x = _row_tile(x_parts, pl.program_id(0), n_first)
    o_ref[...] = _norm_mod(x, g_ref, sc_ref, sh_ref).astype(BF16)


def _pack_pair(lo, hi):
    lo_b = pltpu.bitcast(lo.astype(BF16).astype(F32), jnp.int32)
    hi_b = pltpu.bitcast(hi.astype(BF16).astype(F32), jnp.int32)
    return (hi_b & jnp.int32(-65536)) | lax.shift_right_logical(lo_b, 16)


def _unpack_pair(w):
    lo = pltpu.bitcast(lax.shift_left(w, 16), F32)
    hi = pltpu.bitcast(w & jnp.int32(-65536), F32)
    return lo, hi


def _store_slabs(y, o_ref):
    tm, d = y.shape
    ns = d // (2 * LANES)
    for s in range(ns):
        lo = y[:, s * LANES:(s + 1) * LANES]
        hi = y[:, d // 2 + s * LANES:d // 2 + (s + 1) * LANES]
        o_ref[pl.ds(s, tm, stride=ns), :] = _pack_pair(lo, hi)


def _load_slabs(src_ref, row0, tm, ns, emit):
    for s in range(ns):
        lo, hi = _unpack_pair(src_ref[pl.ds(row0 + s, tm, stride=ns), :])
        emit(s, lo, hi)


def _norm_router_kernel(n_exp, x_ref, g_ref, sc_ref, sh_ref, rwh_ref, rwl_ref, rb_ref,
                        o_ref, slab_ref, route_ref):
    y = _norm_mod(x_ref[...], g_ref, sc_ref, sh_ref)
    o_ref[...] = y.astype(BF16)
    _store_slabs(y, slab_ref)
    y_hi, y_lo = _split(y)
    rwh = rwh_ref[...]
    logits = _dot(y_hi, rwh) + _dot(y_lo, rwh) + _dot(y_hi, rwl_ref[...])
    scores = _sigmoid(logits)
    lane = lax.broadcasted_iota(jnp.int32, scores.shape, 1).astype(F32)
    work = jnp.where(lane < n_exp, scores + rb_ref[...], -jnp.inf)
    route = jnp.zeros_like(scores)
    total = jnp.zeros_like(scores[:, :1])
    for k in range(TOP_K):
        mx = jnp.max(work, axis=-1, keepdims=True)
        first = jnp.min(jnp.where(work == mx, lane, float(LANES)), axis=-1, keepdims=True)
        hit = lane == first
        sc = jnp.sum(jnp.where(hit, scores, 0.0), axis=-1, keepdims=True)
        total = total + sc
        route = jnp.where(lane == k, first, route)
        route = jnp.where(lane == TOP_K + k, sc, route)
        work = jnp.where(hit, -jnp.inf, work)
    is_w = (lane >= TOP_K) & (lane < 2 * TOP_K)
    route_ref[...] = jnp.where(is_w, route / total * ROUTED_SCALE, route)


def _norm(rows, x_all, n_all, gain, modsflat, layer, which_scale, which_shift, router=None):
    d = gain.shape[0]
    tm = ROW_BLK

    def mod_idx(which):
        return lambda i: (rows.mod_row(layer, i, tm, which), 0, 0)

    x_specs, x_args, n_first = _split_rows(x_all, (tm, d), 0)
    in_specs = x_specs + [pl.BlockSpec((1, d), lambda i: (0, 0)),
                          pl.BlockSpec((1, 1, d), mod_idx(which_scale)),
                          pl.BlockSpec((1, 1, d), mod_idx(which_shift))]
    args = x_args + [gain.reshape(1, d), modsflat, modsflat]
    out_spec = pl.BlockSpec((tm, d), lambda i: (i, 0))
    out_shape = jax.ShapeDtypeStruct((n_all, d), BF16)
    if router is None:
        return pl.pallas_call(
            functools.partial(_norm_kernel, n_first), grid=(n_all // tm,), in_specs=in_specs,
            out_specs=out_spec, out_shape=out_shape, compiler_params=_cp(("arbitrary",)),
            name="prenorm",
        )(*args)
    rw_hi, rw_lo, rbias, n_exp = router
    ns = d // (2 * LANES)
    in_specs += [pl.BlockSpec((d, LANES), lambda i: (0, 0)),
                 pl.BlockSpec((d, LANES), lambda i: (0, 0)),
                 pl.BlockSpec((1, LANES), lambda i: (0, 0))]
    return pl.pallas_call(
        functools.partial(_norm_router_kernel, n_exp),
        grid=(n_all // tm,), in_specs=in_specs,
        out_specs=[out_spec, pl.BlockSpec((tm * ns, LANES), lambda i: (i, 0)),
                   pl.BlockSpec((tm, LANES), lambda i: (i, 0))],
        out_shape=[out_shape, jax.ShapeDtypeStruct((n_all * ns, LANES), jnp.int32),
                   jax.ShapeDtypeStruct((n_all, LANES), F32)],
        compiler_params=_cp(("arbitrary",)), name="prenorm_router",
    )(*args, rw_hi, rw_lo, rbias)


def _mm_kernel(a_ref, w_ref, o_ref):
    o_ref[...] = _dot(a_ref[...], w_ref[...]).astype(o_ref.dtype)


def _matmul(a, w, out_dtype, m=None, tm=512):
    k = a.shape[1]
    m = a.shape[0] if m is None else m
    n = w.shape[1]
    tn = _pick(n, (1024, 512, 256, 128))
    return pl.pallas_call(
        _mm_kernel,
        grid=(n // tn, m // tm),
        in_specs=[pl.BlockSpec((tm, k), lambda j, i: (i, 0)),
                  pl.BlockSpec((k, tn), lambda j, i: (0, j))],
        out_specs=pl.BlockSpec((tm, tn), lambda j, i: (i, j)),
        out_shape=jax.ShapeDtypeStruct((m, n), out_dtype),
        compiler_params=_cp(("arbitrary", "arbitrary")),
        name="in_proj",
    )(a, w)


def _mm_wt_kernel(a_ref, wt_ref, o_ref, wb_ref):
    @pl.when(pl.program_id(1) == 0)
    def _():
        wb_ref[...] = wt_ref[0].T.astype(BF16)

    o_ref[...] = _dot(a_ref[...], wb_ref[...]).astype(o_ref.dtype)


def _matmul_wt(a, wt_stack, layer, row0, n, out_dtype, m=None, tm=512):
    k = a.shape[1]
    m = a.shape[0] if m is None else m
    tn = _pick(n, (1024, 512, 256, 128))
    return pl.pallas_call(
        _mm_wt_kernel,
        grid=(n // tn, m // tm),
        in_specs=[pl.BlockSpec((tm, k), lambda j, i: (i, 0)),
                  pl.BlockSpec((pl.Element(1), pl.Element(tn), pl.Element(k)),
                               lambda j, i: (layer, pl.multiple_of(row0 + j * tn, SUBLANES), 0))],
        out_specs=pl.BlockSpec((tm, tn), lambda j, i: (i, j)),
        out_shape=jax.ShapeDtypeStruct((m, n), out_dtype),
        scratch_shapes=[pltpu.VMEM((k, tn), BF16)],
        compiler_params=_cp(("arbitrary", "arbitrary")),
        name="in_proj",
    )(a, wt_stack)


def _hgrn_exact_chunk(q, k, bcum, v, st, reverse):
    c = HG_CHUNK
    last = 0 if reverse else c - 1
    b_last = bcum[last:last + 1]
    o_inter = _dot_nt((q * jnp.exp(bcum)).astype(BF16), st.astype(BF16))
    kd = k * jnp.exp(b_last - bcum)
    st_new = st * jnp.exp(b_last) + _dot_tn(v.astype(BF16), kd.astype(BF16))

    ones = jnp.ones((LANES, LANES), BF16)
    sub_row = lax.broadcasted_iota(jnp.int32, (HG_SUB, LANES), 0)
    v16 = v.astype(BF16)
    outs = []
    for i in range(c // HG_SUB):
        r0 = i * HG_SUB
        bsub = bcum[r0:r0 + HG_SUB]
        qsub = q[r0:r0 + HG_SUB]
        zs = []
        for s in range(HG_SUB):
            keep = (sub_row <= s) if reverse else (sub_row >= s)
            dl = jnp.where(keep, bsub - bcum[r0 + s:r0 + s + 1], NEG_BIG)
            zs.append(qsub * (k[r0 + s:r0 + s + 1] * jnp.exp(dl)))
        red = _dot(jnp.concatenate(zs, axis=0).astype(BF16), ones)
        o_sub = red[0:HG_SUB] * v[r0:r0 + 1]
        for s in range(1, HG_SUB):
            o_sub = o_sub + red[s * HG_SUB:(s + 1) * HG_SUB] * v[r0 + s:r0 + s + 1]
        lo, hi = (r0 + HG_SUB, c) if reverse else (0, r0)
        if hi > lo:
            ref_row = bcum[lo:lo + 1] if reverse else bcum[hi - 1:hi]
            qi = (qsub * jnp.exp(bsub - ref_row)).astype(BF16)
            kt = (k[lo:hi] * jnp.exp(ref_row - bcum[lo:hi])).astype(BF16)
            o_sub = o_sub + _dot(_dot_nt(qi, kt).astype(BF16), v16[lo:hi])
        outs.append(o_sub)
    return o_inter + jnp.concatenate(outs, axis=0), st_new


def _hgrn_fast_block(q_s, k_s, b_s, v_s, o_ref, st_ref, reverse):
    c, sub = HG_CHUNK, HG_FAST_SUB
    n_chunk = ROW_BLK // c
    order = list(range(n_chunk - 1, -1, -1) if reverse else range(n_chunk))
    last = 0 if reverse else c - 1
    qs, xs, eb, scores, vals = {}, {}, {}, {}, {}
    for ci in order:
        rows = pl.ds(ci * c, c)
        q, k, bcum, v = q_s[rows, :], k_s[rows, :], b_s[rows, :], v_s[rows, :]
        v16 = v.astype(BF16)
        b_last = bcum[last:last + 1]
        qs[ci] = (q * jnp.exp(bcum)).astype(BF16)
        eb[ci] = jnp.exp(b_last)
        xs[ci] = _dot_tn(v16, (k * jnp.exp(b_last - bcum)).astype(BF16))
        for i in range(c // sub):
            r0 = i * sub
            lo, hi = (r0, c) if reverse else (0, r0 + sub)
            ref_row = bcum[r0 + sub - 1:r0 + sub] if reverse else bcum[r0:r0 + 1]
            qi = (q[r0:r0 + sub] * jnp.exp(bcum[r0:r0 + sub] - ref_row)).astype(BF16)
            kt = (k[lo:hi] * jnp.exp(ref_row - bcum[lo:hi])).astype(BF16)
            scores[ci, i] = _dot_nt(qi, kt)
            vals[ci, i] = v16[lo:hi]
    outs = {}
    for ci in order:
        parts = []
        for i in range(c // sub):
            r0 = i * sub
            lo, hi = (r0, c) if reverse else (0, r0 + sub)
            key = lax.broadcasted_iota(jnp.int32, (sub, hi - lo), 1) + lo
            qry = lax.broadcasted_iota(jnp.int32, (sub, hi - lo), 0) + r0
            a = jnp.where((key >= qry) if reverse else (key <= qry), scores[ci, i], 0.0)
            parts.append(_dot(a.astype(BF16), vals[ci, i]))
        outs[ci] = jnp.concatenate(parts, axis=0)
    st = st_ref[...]
    for ci in order:
        o = outs[ci] + _dot_nt(qs[ci], st.astype(BF16))
        o_ref[pl.ds(ci * c, c), :] = o.astype(o_ref.dtype)
        st = st * eb[ci] + xs[ci]
    st_ref[...] = st


def _hgrn_kernel(layer, reverse, q_ref, f_ref, v_ref, lb_ref, o_ref, st_ref, q_s, k_s, b_s, v_s):
    @pl.when(pl.program_id(2) == 0)
    def _():
        st_ref[...] = jnp.zeros_like(st_ref)

    lg = lb_ref[0]
    e = jnp.exp(lg - jnp.max(lg, axis=0, keepdims=True))
    p = e / jnp.sum(e, axis=0, keepdims=True)
    lb = jnp.zeros((1, LANES), F32)
    for i in range(1, layer + 1):
        lb = lb + p[i:i + 1]

    qp = q_ref[...].astype(F32)
    fp = f_ref[...].astype(F32)
    q_s[...] = qp * _sigmoid(qp) * HEAD_W ** -0.5
    k_s[...] = (1.0 - lb) * _sigmoid(-fp)
    v_s[...] = v_ref[...].astype(F32)
    lf = jnp.log(jnp.maximum(lb + (1.0 - lb) * _sigmoid(fp), TINY))
    r_i = lax.broadcasted_iota(jnp.int32, (ROW_BLK, ROW_BLK), 0)
    c_i = lax.broadcasted_iota(jnp.int32, (ROW_BLK, ROW_BLK), 1)
    tri = (r_i // HG_CHUNK == c_i // HG_CHUNK) & ((c_i >= r_i) if reverse else (c_i <= r_i))
    tri = jnp.where(tri, 1.0, 0.0).astype(BF16)
    lf_hi, lf_lo = _split(lf)
    bcum = _dot(tri, lf_hi) + _dot(tri, lf_lo)
    b_s[...] = bcum

    spread = jnp.zeros((1, LANES), F32)
    for r0 in range(0, ROW_BLK, HG_FAST_SUB):
        spread = jnp.maximum(
            spread, jnp.abs(bcum[r0:r0 + 1] - bcum[r0 + HG_FAST_SUB - 1:r0 + HG_FAST_SUB]))
    small = jnp.max(spread) < HG_SPREAD_MAX

    @pl.when(small)
    def _():
        _hgrn_fast_block(q_s, k_s, b_s, v_s, o_ref, st_ref, reverse)

    @pl.when(jnp.logical_not(small))
    def _():
        n_chunk = ROW_BLK // HG_CHUNK
        st = st_ref[...]
        for ci in (range(n_chunk - 1, -1, -1) if reverse else range(n_chunk)):
            rows = pl.ds(ci * HG_CHUNK, HG_CHUNK)
            o, st = _hgrn_exact_chunk(q_s[rows, :], k_s[rows, :], b_s[rows, :], v_s[rows, :], st,
                                      reverse)
            o_ref[rows, :] = o.astype(o_ref.dtype)
        st_ref[...] = st


def _hgrn_scan(rows, p_main, lb_logits, layer, direction):
    reverse = direction == 1
    n_all = p_main.shape[0]
    steps = rows.nct + rows.nlt
    depth = lb_logits.shape[1]

    def spec(seg):
        return pl.BlockSpec((ROW_BLK, HEAD_W),
                            lambda b, h, s: (rows.scan_block(b, s, reverse), seg * HEADS + h))

    f_seg = S_HG_FB if reverse else S_HG_FF
    return pl.pallas_call(
        functools.partial(_hgrn_kernel, layer, reverse),
        grid=(rows.batch, HEADS, steps),
        in_specs=[spec(S_HG_Q), spec(f_seg), spec(S_HG_I),
                  pl.BlockSpec((1, depth, HEAD_W), lambda b, h, s: (direction, 0, h))],
        out_specs=pl.BlockSpec((ROW_BLK, HEAD_W),
                               lambda b, h, s: (rows.scan_block(b, s, reverse), h)),
        out_shape=jax.ShapeDtypeStruct((n_all, SEG), BF16),
        scratch_shapes=[pltpu.VMEM((HEAD_W, HEAD_W), F32)]
        + [pltpu.VMEM((ROW_BLK, HEAD_W), F32)] * 4,
        compiler_params=_cp(("arbitrary", "arbitrary", "arbitrary")),
        name="hgrn2_scan",
    )(p_main, p_main, p_main, lb_logits)


def _mlstm_kernel(*refs):
    c = ROW_BLK
    n_in = 7
    dirs = [dict(reverse=False, ins=refs[0:n_in], o=refs[2 * n_in], s=refs[2 * n_in + 2],
                 m=refs[2 * n_in + 3]),
            dict(reverse=True, ins=refs[n_in:2 * n_in], o=refs[2 * n_in + 1], s=refs[2 * n_in + 4],
                 m=refs[2 * n_in + 5])]

    @pl.when(pl.program_id(2) == 0)
    def _():
        for d in dirs:
            d['s'][...] = jnp.zeros_like(d['s'])
            d['m'][...] = jnp.zeros_like(d['m'])

    r_i = lax.broadcasted_iota(jnp.int32, (c, c), 0)
    c_i = lax.broadcasted_iota(jnp.int32, (c, c), 1)
    for d in dirs:
        q_ref, k_ref, v_ref, ig_ref, fg_ref, ib_ref, fb_ref = d['ins']
        d['q'] = q_ref[...]
        d['kt'] = (k_ref[...].astype(F32) * HEAD_W ** -0.5).T
        d['v_cat'] = jnp.concatenate([v_ref[...], jnp.ones((c, HEAD_W), BF16)], axis=1)
        d['ig'] = ig_ref[0] + ib_ref[0][:, :1]
        fx = fg_ref[0] + fb_ref[0][:, :1]
        lf = jnp.minimum(fx, 0.0) - jnp.log1p(jnp.exp(-jnp.abs(fx)))
        cum = jnp.where((r_i >= c_i) if d['reverse'] else (r_i <= c_i), 1.0, 0.0).astype(BF16)
        lf_hi, lf_lo = _split(jnp.broadcast_to(lf, (8, c)))
        d['brow'] = (_dot(lf_hi, cum) + _dot(lf_lo, cum))[0:1]
    for d in dirs:
        d['qk'] = _dot(d['q'], d['kt'].astype(BF16))
        d['state'] = d['s'][...]
        d['q_state'] = _dot(d['q'], d['state'].astype(BF16))
    for d in dirs:
        b_rows = jnp.broadcast_to(d['brow'], (c, c))
        b_cols = b_rows.T
        keep = (c_i >= r_i) if d['reverse'] else (c_i <= r_i)
        dlog = jnp.where(keep, b_cols - b_rows + d['ig'], NEG_BIG)
        d['m_prev'] = d['m'][:, :1]
        inter = b_cols[:, :1] + d['m_prev']
        d['m_t'] = jnp.maximum(jnp.max(dlog, axis=-1, keepdims=True), inter)
        d['w'] = (jnp.exp(dlog - d['m_t']) * d['qk']).astype(BF16)
        d['w_inter'] = jnp.exp(inter - d['m_t'])
    for d in dirs:
        nd = _dot(d['w'], d['v_cat']) + d['w_inter'] * d['q_state']
        num, den = nd[:, :HEAD_W], nd[:, HEAD_W:]
        d['o'][...] = (num / jnp.maximum(jnp.abs(den), jnp.exp(-d['m_t']))).astype(d['o'].dtype)
    for d in dirs:
        last = 0 if d['reverse'] else c - 1
        m_new = d['m_t'][last:last + 1]
        b_last = d['brow'][:, last:last + 1]
        wk = jnp.exp(b_last - d['brow'] + d['ig'] - m_new)
        dec = jnp.exp(b_last + d['m_prev'] - m_new)
        d['s'][...] = dec * d['state'] + _dot((d['kt'] * wk).astype(BF16), d['v_cat'])
        d['m'][...] = jnp.broadcast_to(m_new, d['m'].shape)


def _mlstm_scan(rows, p_main, gates_t, i_bias, f_bias):
    n_all = p_main.shape[0]
    steps = rows.nct + rows.nlt

    def dir_specs(direction):
        reverse = direction == 1
        blk = lambda b, s: rows.scan_block(b, s, reverse)
        spec = lambda seg: pl.BlockSpec((ROW_BLK, HEAD_W),
                                        lambda b, h, s: (blk(b, s), seg * HEADS + h))
        gate = lambda which: pl.BlockSpec((1, 1, ROW_BLK),
                                          lambda b, h, s: (which * HEADS + h, 0, blk(b, s)))
        bias = pl.BlockSpec((1, 1, LANES), lambda b, h, s: (direction * HEADS + h, 0, 0))
        ins = [spec(S_ML_Q), spec(S_ML_K), spec(S_ML_V), gate(direction), gate(2 + direction),
               bias, bias]
        return ins, pl.BlockSpec((ROW_BLK, HEAD_W), lambda b, h, s: (blk(b, s), h))

    ins_f, out_f = dir_specs(0)
    ins_b, out_b = dir_specs(1)
    out = jax.ShapeDtypeStruct((n_all, SEG), BF16)
    args = (p_main, p_main, p_main, gates_t, gates_t, i_bias, f_bias)
    return pl.pallas_call(
        _mlstm_kernel,
        grid=(rows.batch, HEADS, steps),
        in_specs=ins_f + ins_b,
        out_specs=[out_f, out_b],
        out_shape=[out, out],
        scratch_shapes=[pltpu.VMEM((HEAD_W, 2 * HEAD_W), F32), pltpu.VMEM((1, LANES), F32)] * 2,
        compiler_params=_cp(("arbitrary", "arbitrary", "arbitrary")),
        name="mlstm_scan",
    )(*args, *args)


def _da_prep_kernel(q_ref, k_ref, v_ref, cos_ref, sin_ref, qg_ref, kg_ref, qt_ref, ko_ref, vt_ref):
    r_i = lax.broadcasted_iota(jnp.int32, (LANES, LANES), 0)
    c_i = lax.broadcasted_iota(jnp.int32, (LANES, LANES), 1)
    blockdiag = jnp.where((r_i // DA_QK) == (c_i // DA_QK), 1.0 / DA_QK, 0.0).astype(BF16)
    cos, sin = cos_ref[...], sin_ref[...]
    lane = lax.broadcasted_iota(jnp.int32, cos.shape, 1)
    first_half = (lane % DA_QK) < (DA_QK // 2)

    def qk_norm_rope(x, gain):
        x_hi, x_lo = _split(x * x)
        ms = _dot(x_hi, blockdiag) + _dot(x_lo, blockdiag)
        y = x * lax.rsqrt(ms + EPS) * gain
        rot = jnp.where(first_half, -pltpu.roll(y, LANES - DA_QK // 2, 1),
                        pltpu.roll(y, DA_QK // 2, 1))
        return y * cos + rot * sin

    for h in range(HEADS):
        cols = slice(h * HEAD_W, (h + 1) * HEAD_W)
        q = qk_norm_rope(q_ref[:, cols].astype(F32), qg_ref[...]) * (DA_QK ** -0.5 * LOG2E)
        qt_ref[cols, :] = q.T.astype(BF16)
        ko_ref[:, cols] = qk_norm_rope(k_ref[:, cols].astype(F32), kg_ref[...]).astype(BF16)
        vt_ref[cols, :] = v_ref[:, cols].astype(F32).T.astype(BF16)


def _da_prep(rows, p_main, cos_tab, sin_tab, q_gain, k_gain):
    n_all = p_main.shape[0]
    tm = ROW_BLK
    out_t = jax.ShapeDtypeStruct((SEG, n_all), BF16)

    def tab_idx(i):
        lat = i % rows.nlt
        ctx = rows.nlt + (i - rows.batch * rows.nlt) % rows.nct
        return (jnp.where(i < rows.batch * rows.nlt, lat, ctx), 0)

    out = jax.ShapeDtypeStruct((n_all, SEG), BF16)
    return pl.pallas_call(
        _da_prep_kernel,
        grid=(n_all // tm,),
        in_specs=[pl.BlockSpec((tm, SEG), lambda i: (i, S_DA_Q)),
                  pl.BlockSpec((tm, SEG), lambda i: (i, S_DA_K)),
                  pl.BlockSpec((tm, SEG), lambda i: (i, S_DA_V)),
                  pl.BlockSpec((tm, LANES), tab_idx),
                  pl.BlockSpec((tm, LANES), tab_idx),
                  pl.BlockSpec((1, LANES), lambda i: (0, 0)),
                  pl.BlockSpec((1, LANES), lambda i: (0, 0))],
        out_specs=[pl.BlockSpec((SEG, tm), lambda i: (0, i)),
                   pl.BlockSpec((tm, SEG), lambda i: (i, 0)),
                   pl.BlockSpec((SEG, tm), lambda i: (0, i))],
        out_shape=[out_t, out, out_t],
        compiler_params=_cp(("arbitrary",)),
        name="da_qk_prep",
    )(p_main, p_main, p_main, cos_tab, sin_tab, q_gain, k_gain)


def _attn_kernel(lam_init, n_lat_q_blocks, n_lat_keys, tk, qt_ref, kc_ref, vct_ref, kl_ref, vlt_ref,
                 lam_ref, sn_ref, o_ref, acc_ref, m_ref, sa_ref, sb_ref, kn_ref):
    tq = qt_ref.shape[1]
    qt = qt_ref[...]
    chan = lax.broadcasted_iota(jnp.int32, qt.shape, 0)
    zero = jnp.zeros_like(qt)
    q2t = jnp.concatenate([jnp.where(chan < DA_QK, qt, zero), jnp.where(chan >= DA_QK, qt, zero)],
                          axis=1)

    def weighted_values(vt, st, m):
        p = jnp.exp2((st - m).astype(BF16))
        ones = jnp.ones((ATTN_PAD_ROWS, vt.shape[1]), BF16)
        return _dot(jnp.concatenate([vt, ones], axis=0), p)

    st = _dot(kc_ref[...], q2t)
    m0 = jnp.max(st, axis=0, keepdims=True)
    m_ref[...] = m0
    acc_ref[...] = weighted_values(vct_ref[...], st, m0)

    n_chunks = n_lat_keys // tk
    unroll = _pick(n_chunks, (4, 2, 1))
    bufs = (sa_ref, sb_ref) if unroll > 1 else (sa_ref, sa_ref)

    def chunk(c):
        return pl.ds(pl.multiple_of(c * tk, tk), tk)

    def scores(c, buf):
        st = _dot(kl_ref[chunk(c), :], q2t)
        buf[...] = st
        return jnp.max(st, axis=0, keepdims=True)

    def absorb(c, buf, m_cur):
        m_old = m_ref[...]
        m_new = jnp.maximum(m_old, m_cur)
        pv = weighted_values(vlt_ref[:, chunk(c)], buf[...], m_new)
        acc_ref[...] = jnp.exp2(m_old - m_new) * acc_ref[...] + pv
        m_ref[...] = m_new

    def body(j, m_pend):
        for u in range(unroll):
            c = j * unroll + u
            if unroll > 1:
                m_next = scores(jnp.minimum(c + 1, n_chunks - 1), bufs[(u + 1) % 2])
                absorb(c, bufs[u % 2], m_pend)
            else:
                absorb(c, bufs[0], m_pend)
                m_next = scores(jnp.minimum(c + 1, n_chunks - 1), bufs[0])
            m_pend = m_next
        return m_pend

    n_trips = jnp.where(pl.program_id(2) < n_lat_q_blocks, n_chunks // unroll, 0)

    @pl.when(pl.program_id(2) == 0)
    def _():
        r_i = lax.broadcasted_iota(jnp.int32, (LANES, LANES), 0)
        c_i = lax.broadcasted_iota(jnp.int32, (LANES, LANES), 1)
        same_map = jnp.where((r_i // DA_QK) == (c_i // DA_QK), 1.0, 0.0).astype(BF16)

        def max_sq_norm(k):
            kf = k.astype(F32)
            hi, lo = _split(kf * kf)
            return jnp.max(_dot(hi, same_map) + _dot(lo, same_map), axis=0, keepdims=True)

        def body_norm(c, best):
            return jnp.maximum(best, max_sq_norm(kl_ref[chunk(c), :]))
        kn_ref[...] = lax.fori_loop(0, n_chunks, body_norm, max_sq_norm(kc_ref[...]))

    qf = q2t.astype(F32)
    q_sq = jnp.sum(qf * qf, axis=0, keepdims=True)
    col = lax.broadcasted_iota(jnp.int32, q_sq.shape, 1)
    k_sq = jnp.where(col < tq, kn_ref[:, 0:1], kn_ref[:, DA_QK:DA_QK + 1])
    bound = jnp.sqrt(q_sq * k_sq) * 1.01 + 0.01
    bounded = jnp.max(bound - m0) <= ATTN_RANGE_MAX

    def body_bounded(j, carry):
        for u in range(unroll):
            c = j * unroll + u
            st = _dot(kl_ref[chunk(c), :], q2t)
            acc_ref[...] += weighted_values(vlt_ref[:, chunk(c)], st, m0)
        return carry

    @pl.when(bounded)
    def _():
        lax.fori_loop(0, n_trips, body_bounded, 0)

    @pl.when(jnp.logical_not(bounded))
    def _():
        lax.fori_loop(0, n_trips, body, scores(0, bufs[0]))

    lv = lam_ref[...]
    lam = (jnp.exp(jnp.sum(lv[0:1] * lv[1:2], axis=-1, keepdims=True))
           - jnp.exp(jnp.sum(lv[2:3] * lv[3:4], axis=-1, keepdims=True)) + lam_init)
    o = acc_ref[0:HEAD_W, :] / acc_ref[HEAD_W:HEAD_W + 1, :]
    a = o[:, :tq] - lam * o[:, tq:]
    ms = jnp.mean(a * a, axis=0, keepdims=True)
    y = a * lax.rsqrt(ms + EPS) * sn_ref[...] * (1.0 - lam_init)
    o_ref[...] = y.T.astype(o_ref.dtype)


def _attention_call(rows, qt, kh, vt, lam_vec, sub_gain, lam_init, n_rows, prev):
    assert rows.nct == 1
    ctx_blk = lambda b: rows.batch * rows.nlt + b
    tk = _pick(rows.seq, (512, 256))
    if prev is None:
        tq = _pick(rows.seq, (512, 256))
        n_q = rows.seq // tq
        q_blk = lambda b, i: b * n_q + i
        kern = functools.partial(_attn_kernel, lam_init, n_q, rows.seq, tk)
        extra_specs, extra_args, aliases = [], [], {}
    else:
        tq, n_q = ROW_BLK, 0
        q_blk = lambda b, i: ctx_blk(b)
        attn = functools.partial(_attn_kernel, lam_init, n_q, rows.seq, tk)
        kern = lambda *refs: attn(*refs[:7], *refs[8:])
        extra_specs, extra_args, aliases = [pl.BlockSpec(memory_space=pl.ANY)], [prev], {7: 0}
    return pl.pallas_call(
        kern,
        grid=(rows.batch, HEADS, max(n_q, 1)),
        in_specs=[pl.BlockSpec((HEAD_W, tq), lambda b, h, i: (h, q_blk(b, i))),
                  pl.BlockSpec((ROW_BLK, HEAD_W), lambda b, h, i: (ctx_blk(b), h)),
                  pl.BlockSpec((HEAD_W, ROW_BLK), lambda b, h, i: (h, ctx_blk(b))),
                  pl.BlockSpec((rows.seq, HEAD_W), lambda b, h, i: (b, h)),
                  pl.BlockSpec((HEAD_W, rows.seq), lambda b, h, i: (h, b)),
                  pl.BlockSpec(lam_vec.shape, lambda b, h, i: (0, 0)),
                  pl.BlockSpec((HEAD_W, 1), lambda b, h, i: (0, 0))] + extra_specs,
        out_specs=pl.BlockSpec((tq, HEAD_W), lambda b, h, i: (q_blk(b, i), h)),
        out_shape=jax.ShapeDtypeStruct((n_rows, SEG), BF16),
        scratch_shapes=[pltpu.VMEM((HEAD_W + ATTN_PAD_ROWS, 2 * tq), F32),
                        pltpu.VMEM((1, 2 * tq), F32), pltpu.VMEM((tk, 2 * tq), F32),
                        pltpu.VMEM((tk, 2 * tq), F32), pltpu.VMEM((1, LANES), F32)],
        input_output_aliases=aliases,
        compiler_params=_cp(("arbitrary", "arbitrary", "arbitrary")),
        name="diff_attention" if prev is None else "diff_attention_ctx",
    )(qt, kh, vt, kh, vt, lam_vec, sub_gain, *extra_args)


def _attention(rows, qt, kh, vt, lam_vec, sub_gain, lam_init, n_rows):
    da = _attention_call(rows, qt, kh, vt, lam_vec, sub_gain, lam_init, n_rows, None)
    if n_rows > rows.n_lat:
        da = _attention_call(rows, qt, kh, vt, lam_vec, sub_gain, lam_init, n_rows, da)
    return da


def _merge_kernel(ohf_ref, ohb_ref, hg_ref, da_ref, omf_ref, omb_ref, mo_ref, g0_ref, g1_ref,
                  g2_ref, hn_ref, mn_ref, wb_ref, o_ref, h_ref):
    @pl.when(pl.program_id(1) == 0)
    def _():
        for h in range(HEADS):
            cols = slice(h * HEAD_W, (h + 1) * HEAD_W)
            o = ohf_ref[:, cols].astype(F32) + ohb_ref[:, cols].astype(F32)
            y = o * lax.rsqrt(jnp.mean(o * o, axis=-1, keepdims=True) + EPS) * hn_ref[...]
            g = hg_ref[:, cols].astype(F32)
            h_ref[0, :, cols] = (y * (g * _sigmoid(g))).astype(BF16)
            o = omf_ref[:, cols].astype(F32) + omb_ref[:, cols].astype(F32)
            y = o * lax.rsqrt(jnp.mean(o * o, axis=-1, keepdims=True) + EPS) * mn_ref[...]
            h_ref[2, :, cols] = (y * _sigmoid(mo_ref[:, cols].astype(F32))).astype(BF16)
        h_ref[1] = da_ref[...]

    y = _sigmoid(g0_ref[...].astype(F32)) * _dot(h_ref[0], wb_ref[0])
    y = y + _sigmoid(g1_ref[...].astype(F32)) * _dot(h_ref[1], wb_ref[1])
    y = y + _sigmoid(g2_ref[...].astype(F32)) * _dot(h_ref[2], wb_ref[2])
    o_ref[...] = y.astype(o_ref.dtype)


def _merge(p_main, p_merge, ohf, ohb, da, omf, omb, hg_gain, ml_gain, w_branch, d):
    n_all = da.shape[0]
    tm = 512
    tn = _pick(d, (512, 256, 128))
    row = lambda i, j: (i, 0)

    def gate_spec(jj):
        return pl.BlockSpec((tm, tn), lambda i, j: (i, jj * (d // tn) + j))

    return pl.pallas_call(
        _merge_kernel,
        grid=(n_all // tm, d // tn),
        in_specs=[pl.BlockSpec((tm, SEG), row), pl.BlockSpec((tm, SEG), row),
                  pl.BlockSpec((tm, SEG), lambda i, j: (i, S_HG_G)),
                  pl.BlockSpec((tm, SEG), row),
                  pl.BlockSpec((tm, SEG), row), pl.BlockSpec((tm, SEG), row),
                  pl.BlockSpec((tm, SEG), lambda i, j: (i, S_ML_O)),
                  gate_spec(0), gate_spec(1), gate_spec(2),
                  pl.BlockSpec((1, LANES), lambda i, j: (0, 0)),
                  pl.BlockSpec((1, LANES), lambda i, j: (0, 0)),
                  pl.BlockSpec((3, SEG, tn), lambda i, j: (0, 0, j))],
        out_specs=pl.BlockSpec((tm, tn), lambda i, j: (i, j)),
        out_shape=jax.ShapeDtypeStruct((n_all, d), BF16),
        scratch_shapes=[pltpu.VMEM((3, tm, SEG), BF16)],
        compiler_params=_cp(("arbitrary", "arbitrary")),
        name="branch_merge",
    )(ohf, ohb, p_main, da, omf, omb, p_main, p_merge, p_merge, p_merge, hg_gain, ml_gain,
      w_branch)


def _proj_resid_kernel(n_first, a_ref, w_ref, g_ref, *refs):
    x_parts, o_ref = refs[:-1], refs[-1]
    x = _row_tile(x_parts, pl.program_id(1), n_first)
    o_ref[...] = x + g_ref[0] * _dot(a_ref[...], w_ref[...])


def _proj_resid(rows, a, w, x_all, modsflat, layer, which_gate):
    n_all, k = a.shape
    d = w.shape[1]
    tm = 512
    tn = _pick(d, (1024, 512, 256, 128))
    x_specs, x_args, n_first = _split_rows(x_all, (tm, tn), 1)
    return pl.pallas_call(
        functools.partial(_proj_resid_kernel, n_first),
        grid=(d // tn, n_all // tm),
        in_specs=[pl.BlockSpec((tm, k), lambda j, i: (i, 0)),
                  pl.BlockSpec((k, tn), lambda j, i: (0, j)),
                  pl.BlockSpec((1, 1, tn),
                               lambda j, i: (rows.mod_row(layer, i, tm, which_gate), 0, j))]
        + x_specs,
        out_specs=pl.BlockSpec((tm, tn), lambda j, i: (i, j)),
        out_shape=jax.ShapeDtypeStruct((n_all, d), F32),
        compiler_params=_cp(("arbitrary", "arbitrary")),
        name="out_proj_residual",
    )(a, w, modsflat, *x_args)


def _swiglu_hidden(x, w1_ref, w3_ref):
    h1 = _dot(x, w1_ref[...])
    h3 = _dot(x, w3_ref[...])
    return (h1 * _sigmoid(h1)) * h3


def _shared_expert_kernel(tok_ref, w1_ref, w3_ref, w2_ref, o_ref):
    a = _swiglu_hidden(tok_ref[...], w1_ref, w3_ref)
    o_ref[...] = _dot(a.astype(BF16), w2_ref[...]).astype(o_ref.dtype)


def _shared_expert(tok, w1, w3, w2):
    n_all, d = tok.shape
    de = w1.shape[1]
    tm = 512
    full = lambda i: (0, 0)
    return pl.pallas_call(
        _shared_expert_kernel,
        grid=(n_all // tm,),
        in_specs=[pl.BlockSpec((tm, d), lambda i: (i, 0)),
                  pl.BlockSpec((d, de), full), pl.BlockSpec((d, de), full),
                  pl.BlockSpec((de, d), full)],
        out_specs=pl.BlockSpec((tm, d), lambda i: (i, 0)),
        out_shape=jax.ShapeDtypeStruct((n_all, d), BF16),
        compiler_params=_cp(("arbitrary",)),
        name="moe_shared_expert",
    )(tok, w1, w3, w2)


def _moe_plan_kernel(route_ref, pos_ref, cnt_ref, carry_ref):
    @pl.when(pl.program_id(0) == 0)
    def _():
        carry_ref[...] = jnp.zeros_like(carry_ref)

    r = route_ref[...]
    tm = r.shape[0]
    lane = lax.broadcasted_iota(jnp.int32, r.shape, 1).astype(F32)
    mask = jnp.zeros_like(r)
    for k in range(TOP_K):
        mask = mask + jnp.where(lane == r[:, k:k + 1], 1.0, 0.0)
    r_i = lax.broadcasted_iota(jnp.int32, (tm, tm), 0)
    c_i = lax.broadcasted_iota(jnp.int32, (tm, tm), 1)
    before = jnp.where(c_i < r_i, 1.0, 0.0).astype(BF16)
    rank = _dot(before, mask.astype(BF16)) + carry_ref[...]
    out = jnp.zeros_like(r)
    for k in range(TOP_K):
        pk = jnp.sum(jnp.where(lane == r[:, k:k + 1], rank, 0.0), axis=-1, keepdims=True)
        out = jnp.where(lane == k, pk, out)
    pos_ref[...] = out
    carry = carry_ref[...] + jnp.sum(mask, axis=0, keepdims=True)
    carry_ref[...] = carry
    cnt_ref[...] = jnp.broadcast_to(carry, cnt_ref.shape)


def _moe_plan(route):
    n = route.shape[0]
    tm = ROW_BLK
    return pl.pallas_call(
        _moe_plan_kernel,
        grid=(n // tm,),
        in_specs=[pl.BlockSpec((tm, LANES), lambda i: (i, 0))],
        out_specs=[pl.BlockSpec((tm, LANES), lambda i: (i, 0)),
                   pl.BlockSpec((8, LANES), lambda i: (0, 0))],
        out_shape=[jax.ShapeDtypeStruct((n, LANES), F32), jax.ShapeDtypeStruct((8, LANES), F32)],
        scratch_shapes=[pltpu.VMEM((1, LANES), F32)],
        compiler_params=_cp(("arbitrary",)),
        name="moe_plan",
    )(route)


def _moe_routed_kernel(ns, te_ref, na_ref, st_ref, tok_hbm, sw_ref, w1_ref, w3_ref, w2_ref, o_ref,
                       xbuf, x_ref, sem):
    tm = MOE_TILE
    t = pl.program_id(0)
    n_active = na_ref[0]
    par = t % 2

    def gather_copy(tile, buf, r):
        tok = st_ref[tile * tm + r]
        return pltpu.make_async_copy(
            tok_hbm.at[pl.ds(pl.multiple_of(tok * ns, ns), ns), :],
            xbuf.at[buf, pl.ds(pl.multiple_of(r * ns, ns), ns), :], sem.at[buf])

    def for_each_copy(tile, buf, act):
        def body(r4, carry):
            for u in range(4):
                act(gather_copy(tile, buf, r4 * 4 + u), u)
            return carry
        lax.fori_loop(0, tm // 4, body, 0)

    start = lambda cp, u: cp.start(priority=u % 2)

    @pl.when(t == 0)
    def _():
        for_each_copy(0, 0, start)

    @pl.when(t + 1 < n_active)
    def _():
        for_each_copy(t + 1, 1 - par, start)

    @pl.when(t < n_active)
    def _():
        for_each_copy(t, par, lambda cp, u: cp.wait())
        d = x_ref.shape[1]

        def emit(s, lo, hi):
            x_ref[:, s * LANES:(s + 1) * LANES] = lo.astype(BF16)
            x_ref[:, d // 2 + s * LANES:d // 2 + (s + 1) * LANES] = hi.astype(BF16)
        _load_slabs(xbuf.at[par], 0, tm, ns, emit)
        a = _swiglu_hidden(x_ref[...], w1_ref, w3_ref) * sw_ref[...]
        _store_slabs(_dot(a.astype(BF16), w2_ref[...]), o_ref)

    @pl.when(t >= n_active)
    def _():
        o_ref[...] = jnp.zeros_like(o_ref)


def _moe_routed(tok_slabs, slot_token, slot_w, tile_expert, n_active, w1, w3, w2):
    n_exp, d, de = w1.shape
    ns = d // (2 * LANES)
    tm = MOE_TILE
    n_slots = slot_token.shape[0]
    nt = n_slots // tm
    grid_spec = pltpu.PrefetchScalarGridSpec(
        num_scalar_prefetch=3,
        grid=(nt,),
        in_specs=[pl.BlockSpec(memory_space=pl.ANY),
                  pl.BlockSpec((tm, 1), lambda t, te, na, st: (t, 0)),
                  pl.BlockSpec((None, d, de), lambda t, te, na, st: (te[t], 0, 0)),
                  pl.BlockSpec((None, d, de), lambda t, te, na, st: (te[t], 0, 0)),
                  pl.BlockSpec((None, de, d), lambda t, te, na, st: (te[t], 0, 0))],
        out_specs=pl.BlockSpec((tm * ns, LANES), lambda t, te, na, st: (t, 0)),
        scratch_shapes=[pltpu.VMEM((2, tm * ns, LANES), jnp.int32), pltpu.VMEM((tm, d), BF16),
                        pltpu.SemaphoreType.DMA((2,))],
    )
    return pl.pallas_call(
        functools.partial(_moe_routed_kernel, ns),
        grid_spec=grid_spec,
        out_shape=jax.ShapeDtypeStruct((n_slots * ns, LANES), jnp.int32),
        compiler_params=_cp(("arbitrary",)),
        name="moe_routed_experts",
    )(tile_expert, n_active, slot_token, tok_slabs, slot_w, w1, w3, w2)


def _moe_combine_kernel(ns, s4_ref, ys_hbm, ysh_ref, x_ref, g_ref, o_ref, ybuf, sem):
    tm = x_ref.shape[0]
    d = x_ref.shape[1]
    i = pl.program_id(0)
    par = i % 2

    def gather_copy(tile, buf, row, k):
        slot = s4_ref[(tile * tm + row) * TOP_K + k]
        return pltpu.make_async_copy(
            ys_hbm.at[pl.ds(pl.multiple_of(slot * ns, ns), ns), :],
            ybuf.at[buf, pl.ds(pl.multiple_of((k * tm + row) * ns, ns), ns), :], sem.at[buf])

    def for_each_copy(tile, buf, act):
        def body(row, carry):
            for k in range(TOP_K):
                act(gather_copy(tile, buf, row, k), k)
            return carry
        lax.fori_loop(0, tm, body, 0)

    start = lambda cp, k: cp.start(priority=k % 2)

    @pl.when(i == 0)
    def _():
        for_each_copy(0, 0, start)

    @pl.when(i + 1 < pl.num_programs(0))
    def _():
        for_each_copy(i + 1, 1 - par, start)

    for_each_copy(i, par, lambda cp, k: cp.wait())

    gate = g_ref[0]
    for s in range(ns):
        lo_cols = slice(s * LANES, (s + 1) * LANES)
        hi_cols = slice(d // 2 + s * LANES, d // 2 + (s + 1) * LANES)
        y_lo = ysh_ref[:, lo_cols].astype(F32)
        y_hi = ysh_ref[:, hi_cols].astype(F32)
        for k in range(TOP_K):
            lo, hi = _unpack_pair(ybuf[par, pl.ds(k * tm * ns + s, tm, stride=ns), :])
            y_lo = y_lo + lo
            y_hi = y_hi + hi
        o_ref[:, lo_cols] = x_ref[:, lo_cols] + gate[:, lo_cols] * y_lo
        o_ref[:, hi_cols] = x_ref[:, hi_cols] + gate[:, hi_cols] * y_hi


def _moe_combine(rows, x_all, y_shared, ys_slabs, slot4, modsflat, layer, which_gate, n_rows):
    d = x_all.shape[1]
    ns = d // (2 * LANES)
    tm = 128
    grid_spec = pltpu.PrefetchScalarGridSpec(
        num_scalar_prefetch=1,
        grid=(n_rows // tm,),
        in_specs=[pl.BlockSpec(memory_space=pl.ANY),
                  pl.BlockSpec((tm, d), lambda i, s4: (i, 0)),
                  pl.BlockSpec((tm, d), lambda i, s4: (i, 0)),
                  pl.BlockSpec((1, 1, d),
                               lambda i, s4: (rows.mod_row(layer, i, tm, which_gate), 0, 0))],
        out_specs=pl.BlockSpec((tm, d), lambda i, s4: (i, 0)),
        scratch_shapes=[pltpu.VMEM((2, TOP_K * tm * ns, LANES), jnp.int32),
                        pltpu.SemaphoreType.DMA((2,))],
    )
    return pl.pallas_call(
        functools.partial(_moe_combine_kernel, ns),
        grid_spec=grid_spec,
        out_shape=jax.ShapeDtypeStruct((n_rows, d), F32),
        compiler_params=_cp(("arbitrary",)),
        name="moe_combine_residual",
    )(slot4, ys_slabs, y_shared, x_all, modsflat)


def _moe_slots(route, pos, counts, n_exp):
    n = route.shape[0]
    tm = MOE_TILE
    idx4 = route[:, :TOP_K].astype(jnp.int32)
    w4 = route[:, TOP_K:2 * TOP_K]
    pos4 = pos[:, :TOP_K].astype(jnp.int32)
    cnt = counts[0, :n_exp].astype(jnp.int32)
    padded = (cnt + tm - 1) // tm * tm
    ends = jnp.cumsum(padded)
    starts = ends - padded
    slot4 = (starts[idx4] + pos4).reshape(-1)
    n_slots = n * TOP_K + n_exp * tm
    nt = n_slots // tm
    tile_expert = jnp.minimum(
        jnp.searchsorted(ends, jnp.arange(nt, dtype=jnp.int32) * tm, side='right'),
        n_exp - 1).astype(jnp.int32)
    n_active = (ends[-1:] // tm).astype(jnp.int32)
    token_of = jnp.repeat(jnp.arange(n, dtype=jnp.int32), TOP_K)
    pairs = jnp.stack([token_of, lax.bitcast_convert_type(w4.reshape(-1), jnp.int32)], axis=1)
    slot_meta = jnp.zeros((n_slots, 2), jnp.int32).at[slot4].set(pairs)
    slot_token = slot_meta[:, 0]
    slot_w = lax.bitcast_convert_type(slot_meta[:, 1:2], F32)
    return slot4, slot_token, slot_w, tile_expert, n_active


def _rope_tables(seq, ctx_len):
    n = DA_QK // 4
    t = jnp.arange(seq)
    inv = ROPE_THETA ** (-jnp.arange(n, dtype=F32) / n)
    row = (t // GRID_W).astype(F32)
    col = (t % GRID_W).astype(F32)
    ang = jnp.concatenate([row[:, None] * inv, col[:, None] * inv], axis=-1)
    ang = jnp.concatenate([ang, jnp.zeros((ctx_len, 2 * n), F32)], axis=0)
    return jnp.tile(jnp.cos(ang), (1, 4)), jnp.tile(jnp.sin(ang), (1, 4))


def kernel(x, c, ctx, c_ctx, norm1, norm2, w_ada, b_ada, w_in, hg_lb_logits, hg_norm, da_q_norm,
           da_k_norm, da_lambda, da_sub_norm, ml_igate_bias, ml_fgate_bias, ml_norm, w_branch,
           w_out, router_w, router_bias, exp_w1, exp_w3, exp_w2, sh_w1, sh_w3, sh_w2):
    batch, seq, d = x.shape
    ctx_len = ctx.shape[1]
    depth = w_ada.shape[0]
    n_exp = router_w.shape[-1]
    rows = _Rows(batch, seq, ctx_len)
    assert batch + 1 <= 16 and n_exp <= LANES

    x_all = (x.reshape(batch * seq, d), ctx.reshape(batch * ctx_len, d))
    cvec = jnp.zeros((16, d), F32).at[0].set(c_ctx).at[1:1 + batch].set(c)
    mods = _mods(cvec, w_ada, b_ada)
    modsflat = mods[:, :rows.n_groups].reshape(depth * rows.n_groups * 6, 1, d)
    cos_tab, sin_tab = _rope_tables(seq, ctx_len)
    gate_lo = N_SEG * SEG
    w_in_t = jnp.swapaxes(w_in, 1, 2)

    for l in range(depth):
        n_rows = rows.n_all if l < depth - 1 else rows.n_lat
        lam_init = 0.8 - 0.6 * math.exp(-0.3 * l)
        hx = _norm(rows, x_all, rows.n_all, norm1[l], modsflat, l, 1, 0)
        p_main = _matmul_wt(hx, w_in_t, l, 0, gate_lo, BF16)
        p_gate = _matmul_wt(hx, w_in_t, l, gate_lo, LANES, F32)
        p_merge = _matmul_wt(hx, w_in_t, l, gate_lo + N_GATE_COLS, N_BRANCH * d, BF16, m=n_rows)
        gates_t = p_gate[:, :N_GATE_COLS].T.reshape(N_GATE_COLS, 1, rows.n_all)

        ohf = _hgrn_scan(rows, p_main, hg_lb_logits, l, 0)
        ohb = _hgrn_scan(rows, p_main, hg_lb_logits, l, 1)

        tile2 = lambda g: jnp.tile(g.reshape(1, DA_QK), (1, 2))
        qt, kh, vt = _da_prep(rows, p_main, cos_tab, sin_tab, tile2(da_q_norm[l]),
                              tile2(da_k_norm[l]))
        da = _attention(rows, qt, kh, vt, da_lambda[l], da_sub_norm[l].reshape(HEAD_W, 1),
                        lam_init, n_rows)

        bias = lambda bv: jnp.broadcast_to(bv.reshape(2 * HEADS, 1, 1), (2 * HEADS, 1, LANES))
        i_b, f_b = bias(ml_igate_bias[l]), bias(ml_fgate_bias[l])
        omf, omb = _mlstm_scan(rows, p_main, gates_t, i_b, f_b)

        ymid = _merge(p_main, p_merge, ohf, ohb, da, omf, omb, hg_norm[l].reshape(1, HEAD_W),
                      ml_norm[l].reshape(1, HEAD_W), w_branch[l].astype(BF16), d)
        x_all = _proj_resid(rows, ymid, w_out[l].astype(BF16), x_all, modsflat, l, 2)

        rw = jnp.pad(router_w[l], ((0, 0), (0, LANES - n_exp)))
        rw_hi = rw.astype(BF16)
        rw_lo = (rw - rw_hi.astype(F32)).astype(BF16)
        rb = jnp.pad(router_bias[l], (0, LANES - n_exp)).reshape(1, LANES)
        tok, tok_slabs, route = _norm(rows, x_all, n_rows, norm2[l], modsflat, l, 4, 3,
                                      router=(rw_hi, rw_lo, rb, n_exp))
        pos, counts = _moe_plan(route)
        slot4, slot_token, slot_w, tile_expert, n_active = _moe_slots(route, pos, counts, n_exp)
        ys = _moe_routed(tok_slabs, slot_token, slot_w, tile_expert, n_active,
                         exp_w1[l].astype(BF16), exp_w3[l].astype(BF16), exp_w2[l].astype(BF16))
        y_sh = _shared_expert(tok, sh_w1[l].astype(BF16), sh_w3[l].astype(BF16),
                              sh_w2[l].astype(BF16))
        x_all = _moe_combine(rows, x_all, y_sh, ys, slot4, modsflat, l, 5, n_rows)

    return x_all.reshape(batch, seq, d)
```

```python
import functools
import math

import jax
import jax.numpy as jnp
from jax import lax
from jax.experimental import pallas as pl
from jax.experimental.pallas import tpu as pltpu

F32 = jnp.float32
BF16 = jnp.bfloat16

EPS = 1e-6
NEG_BIG = -1e30
TINY = 1e-30
GRID_W = 64
ROPE_THETA = 10000.0
ROUTED_SCALE = 2.5
TOP_K = 4
LOG2E = 1.4426950408889634

HEADS = 12
HEAD_W = 128
SEG = HEADS * HEAD_W
DA_QK = 64
N_GATE_COLS = 4 * HEADS

S_HG_Q, S_HG_FF, S_HG_FB, S_HG_I, S_HG_G = 0, 1, 2, 3, 4
S_DA_Q, S_DA_K, S_DA_V = 5, 6, 7
S_ML_Q, S_ML_K, S_ML_V, S_ML_O = 8, 9, 10, 11
N_SEG = 12
N_BRANCH = 3

LANES = 128
SUBLANES = 8
ROW_BLK = 256
HG_CHUNK = 64
HG_SUB = 16
HG_FAST_SUB = 32
HG_SPREAD_MAX = 60.0
MOE_TILE = 256
ATTN_PAD_ROWS = 16
ATTN_RANGE_MAX = 60.0
VMEM_LIMIT = 56 * 1024 * 1024


def _cp(sem, vmem=VMEM_LIMIT):
    return pltpu.CompilerParams(dimension_semantics=sem, vmem_limit_bytes=vmem)


def _pick(n, cands):
    for c in cands:
        if n % c == 0:
            return c
    raise ValueError(f"no tile for {n} in {cands}")


def _dot(a, b):
    return jnp.dot(a, b, preferred_element_type=F32)


def _dot_nt(a, b):
    return lax.dot_general(a, b, (((1,), (1,)), ((), ())), preferred_element_type=F32)


def _dot_tn(a, b):
    return lax.dot_general(a, b, (((0,), (0,)), ((), ())), preferred_element_type=F32)


def _split(x):
    hi = x.astype(BF16)
    lo = (x - hi.astype(F32)).astype(BF16)
    return hi, lo


def _sigmoid(x):
    return 1.0 / (1.0 + jnp.exp(-x))


class _Rows:
    def __init__(self, batch, seq, ctx_len):
        self.batch, self.seq, self.ctx = batch, seq, ctx_len
        assert seq % ROW_BLK == 0 and ctx_len % ROW_BLK == 0
        self.nlt = seq // ROW_BLK
        self.nct = ctx_len // ROW_BLK
        self.n_lat = batch * seq
        self.n_all = batch * (seq + ctx_len)
        self.n_groups = batch + 1

    def group(self, i, tm):
        lat_tiles = self.n_lat // tm
        return jnp.where(i < lat_tiles, 1 + i // (self.seq // tm), 0)

    def mod_row(self, layer, i, tm, which):
        return (layer * self.n_groups + self.group(i, tm)) * 6 + which

    def scan_block(self, b, s, reverse):
        if reverse:
            cblk = self.batch * self.nlt + b * self.nct + (self.nct - 1 - s)
            lblk = b * self.nlt + (self.nlt - 1 - (s - self.nct))
        else:
            cblk = self.batch * self.nlt + b * self.nct + s
            lblk = b * self.nlt + (s - self.nct)
        return jnp.where(s < self.nct, cblk, lblk)


def _mods_kernel(c_ref, w_ref, b_ref, o_ref):
    c = c_ref[...]
    s_hi, s_lo = _split(c * _sigmoid(c))
    w_hi, w_lo = _split(w_ref[...])
    acc = _dot(s_hi, w_hi) + _dot(s_lo, w_hi) + _dot(s_hi, w_lo)
    o_ref[...] = acc + b_ref[...]


def _mods(cvec, w_ada, b_ada):
    depth, d, n = w_ada.shape
    tn = _pick(n, (512, 256, 128))
    rows = cvec.shape[0]
    return pl.pallas_call(
        _mods_kernel,
        grid=(depth, n // tn),
        in_specs=[pl.BlockSpec((rows, d), lambda l, j: (0, 0)),
                  pl.BlockSpec((None, d, tn), lambda l, j: (l, 0, j)),
                  pl.BlockSpec((None, 1, tn), lambda l, j: (l, 0, j))],
        out_specs=pl.BlockSpec((None, rows, tn), lambda l, j: (l, 0, j)),
        out_shape=jax.ShapeDtypeStruct((depth, rows, n), F32),
        compiler_params=_cp(("arbitrary", "arbitrary")),
        name="adaln_mods",
    )(cvec, w_ada, b_ada.reshape(depth, 1, n))


def _split_rows(x, block, tile_axis):
    if not isinstance(x, tuple):
        return [pl.BlockSpec(block, lambda *g: (g[tile_axis], _other(g, tile_axis)))], [x], None
    n_first = x[0].shape[0] // block[0]
    first = pl.BlockSpec(block, lambda *g: (jnp.minimum(g[tile_axis], n_first - 1),
                                            _other(g, tile_axis)))
    second = pl.BlockSpec(block, lambda *g: (jnp.maximum(g[tile_axis] - n_first, 0),
                                             _other(g, tile_axis)))
    return [first, second], list(x), n_first


def _other(g, tile_axis):
    return g[1 - tile_axis] if len(g) == 2 else 0


def _row_tile(parts, i, n_first):
    if len(parts) == 1:
        return parts[0][...]
    return jnp.where(i < n_first, parts[0][...], parts[1][...])


def _norm_mod(x, g_ref, sc_ref, sh_ref):
    ms = jnp.mean(x * x, axis=-1, keepdims=True)
    y = x * lax.rsqrt(ms + EPS) * g_ref[...]
    return y * (1.0 + sc_ref[0]) + sh_ref[0]


def _norm_kernel(n_first, *refs):
    x_parts, (g_ref, sc_ref, sh_ref, o_ref) = refs[:-4], refs[-4:]
    x = _row_tile(x_parts, pl.program_id(0), n_first)
    o_ref[...] = _norm_mod(x, g_ref, sc_ref, sh_ref).astype(BF16)


def _pack_pair(lo, hi):
    lo_b = pltpu.bitcast(lo.astype(BF16).astype(F32), jnp.int32)
    hi_b = pltpu.bitcast(hi.astype(BF16).astype(F32), jnp.int32)
    return (hi_b & jnp.int32(-65536)) | lax.shift_right_logical(lo_b, 16)


def _unpack_pair(w):
    lo = pltpu.bitcast(lax.shift_left(w, 16), F32)
    hi = pltpu.bitcast(w & jnp.int32(-65536), F32)
    return lo, hi


def _store_slabs(y, o_ref):
    tm, d = y.shape
    ns = d // (2 * LANES)
    for s in range(ns):
        lo = y[:, s * LANES:(s + 1) * LANES]
        hi = y[:, d // 2 + s * LANES:d // 2 + (s + 1) * LANES]
        o_ref[pl.ds(s, tm, stride=ns), :] = _pack_pair(lo, hi)


def _load_slabs(src_ref, row0, tm, ns, emit):
    for s in range(ns):
        lo, hi = _unpack_pair(src_ref[pl.ds(row0 + s, tm, stride=ns), :])
        emit(s, lo, hi)


def _norm_router_kernel(n_exp, x_ref, g_ref, sc_ref, sh_ref, rwh_ref, rwl_ref, rb_ref,
                        o_ref, slab_ref, route_ref):
    y = _norm_mod(x_ref[...], g_ref, sc_ref, sh_ref)
    o_ref[...] = y.astype(BF16)
    _store_slabs(y, slab_ref)
    y_hi, y_lo = _split(y)
    rwh = rwh_ref[...]
    logits = _dot(y_hi, rwh) + _dot(y_lo, rwh) + _dot(y_hi, rwl_ref[...])
    scores = _sigmoid(logits)
    lane = lax.broadcasted_iota(jnp.int32, scores.shape, 1).astype(F32)
    work = jnp.where(lane < n_exp, scores + rb_ref[...], -jnp.inf)
    route = jnp.zeros_like(scores)
    total = jnp.zeros_like(scores[:, :1])
    for k in range(TOP_K):
        mx = jnp.max(work, axis=-1, keepdims=True)
        first = jnp.min(jnp.where(work == mx, lane, float(LANES)), axis=-1, keepdims=True)
        hit = lane == first
        sc = jnp.sum(jnp.where(hit, scores, 0.0), axis=-1, keepdims=True)
        total = total + sc
        route = jnp.where(lane == k, first, route)
        route = jnp.where(lane == TOP_K + k, sc, route)
        work = jnp.where(hit, -jnp.inf, work)
    is_w = (lane >= TOP_K) & (lane < 2 * TOP_K)
    route_ref[...] = jnp.where(is_w, route / total * ROUTED_SCALE, route)


def _norm(rows, x_all, n_all, gain, modsflat, layer, which_scale, which_shift, router=None):
    d = gain.shape[0]
    tm = ROW_BLK

    def mod_idx(which):
        return lambda i: (rows.mod_row(layer, i, tm, which), 0, 0)

    x_specs, x_args, n_first = _split_rows(x_all, (tm, d), 0)
    in_specs = x_specs + [pl.BlockSpec((1, d), lambda i: (0, 0)),
                          pl.BlockSpec((1, 1, d), mod_idx(which_scale)),
                          pl.BlockSpec((1, 1, d), mod_idx(which_shift))]
    args = x_args + [gain.reshape(1, d), modsflat, modsflat]
    out_spec = pl.BlockSpec((tm, d), lambda i: (i, 0))
    out_shape = jax.ShapeDtypeStruct((n_all, d), BF16)
    if router is None:
        return pl.pallas_call(
            functools.partial(_norm_kernel, n_first), grid=(n_all // tm,), in_specs=in_specs,
            out_specs=out_spec, out_shape=out_shape, compiler_params=_cp(("arbitrary",)),
            name="prenorm",
        )(*args)
    rw_hi, rw_lo, rbias, n_exp = router
    ns = d // (2 * LANES)
    in_specs += [pl.BlockSpec((d, LANES), lambda i: (0, 0)),
                 pl.BlockSpec((d, LANES), lambda i: (0, 0)),
                 pl.BlockSpec((1, LANES), lambda i: (0, 0))]
    return pl.pallas_call(
        functools.partial(_norm_router_kernel, n_exp),
        grid=(n_all // tm,), in_specs=in_specs,
        out_specs=[out_spec, pl.BlockSpec((tm * ns, LANES), lambda i: (i, 0)),
                   pl.BlockSpec((tm, LANES), lambda i: (i, 0))],
        out_shape=[out_shape, jax.ShapeDtypeStruct((n_all * ns, LANES), jnp.int32),
                   jax.ShapeDtypeStruct((n_all, LANES), F32)],
        compiler_params=_cp(("arbitrary",)), name="prenorm_router",
    )(*args, rw_hi, rw_lo, rbias)


def _mm_kernel(a_ref, w_ref, o_ref):
    o_ref[...] = _dot(a_ref[...], w_ref[...]).astype(o_ref.dtype)


def _matmul(a, w, out_dtype, m=None, tm=512):
    k = a.shape[1]
    m = a.shape[0] if m is None else m
    n = w.shape[1]
    tn = _pick(n, (1024, 512, 256, 128))
    return pl.pallas_call(
        _mm_kernel,
        grid=(n // tn, m // tm),
        in_specs=[pl.BlockSpec((tm, k), lambda j, i: (i, 0)),
                  pl.BlockSpec((k, tn), lambda j, i: (0, j))],
        out_specs=pl.BlockSpec((tm, tn), lambda j, i: (i, j)),
        out_shape=jax.ShapeDtypeStruct((m, n), out_dtype),
        compiler_params=_cp(("arbitrary", "arbitrary")),
        name="in_proj",
    )(a, w)


def _mm_wt_kernel(a_ref, wt_ref, o_ref, wb_ref):
    @pl.when(pl.program_id(1) == 0)
    def _():
        wb_ref[...] = wt_ref[0].T.astype(BF16)

    o_ref[...] = _dot(a_ref[...], wb_ref[...]).astype(o_ref.dtype)


def _matmul_wt(a, wt_stack, layer, row0, n, out_dtype, m=None, tm=512):
    k = a.shape[1]
    m = a.shape[0] if m is None else m
    tn = _pick(n, (1024, 512, 256, 128))
    return pl.pallas_call(
        _mm_wt_kernel,
        grid=(n // tn, m // tm),
        in_specs=[pl.BlockSpec((tm, k), lambda j, i: (i, 0)),
                  pl.BlockSpec((pl.Element(1), pl.Element(tn), pl.Element(k)),
                               lambda j, i: (layer, pl.multiple_of(row0 + j * tn, SUBLANES), 0))],
        out_specs=pl.BlockSpec((tm, tn), lambda j, i: (i, j)),
        out_shape=jax.ShapeDtypeStruct((m, n), out_dtype),
        scratch_shapes=[pltpu.VMEM((k, tn), BF16)],
        compiler_params=_cp(("arbitrary", "arbitrary")),
        name="in_proj",
    )(a, wt_stack)


def _hgrn_exact_chunk(q, k, bcum, v, st, reverse):
    c = HG_CHUNK
    last = 0 if reverse else c - 1
    b_last = bcum[last:last + 1]
    o_inter = _dot_nt((q * jnp.exp(bcum)).astype(BF16), st.astype(BF16))
    kd = k * jnp.exp(b_last - bcum)
    st_new = st * jnp.exp(b_last) + _dot_tn(v.astype(BF16), kd.astype(BF16))

    ones = jnp.ones((LANES, LANES), BF16)
    sub_row = lax.broadcasted_iota(jnp.int32, (HG_SUB, LANES), 0)
    v16 = v.astype(BF16)
    outs = []
    for i in range(c // HG_SUB):
        r0 = i * HG_SUB
        bsub = bcum[r0:r0 + HG_SUB]
        qsub = q[r0:r0 + HG_SUB]
        zs = []
        for s in range(HG_SUB):
            keep = (sub_row <= s) if reverse else (sub_row >= s)
            dl = jnp.where(keep, bsub - bcum[r0 + s:r0 + s + 1], NEG_BIG)
            zs.append(qsub * (k[r0 + s:r0 + s + 1] * jnp.exp(dl)))
        red = _dot(jnp.concatenate(zs, axis=0).astype(BF16), ones)
        o_sub = red[0:HG_SUB] * v[r0:r0 + 1]
        for s in range(1, HG_SUB):
            o_sub = o_sub + red[s * HG_SUB:(s + 1) * HG_SUB] * v[r0 + s:r0 + s + 1]
        lo, hi = (r0 + HG_SUB, c) if reverse else (0, r0)
        if hi > lo:
            ref_row = bcum[lo:lo + 1] if reverse else bcum[hi - 1:hi]
            qi = (qsub * jnp.exp(bsub - ref_row)).astype(BF16)
            kt = (k[lo:hi] * jnp.exp(ref_row - bcum[lo:hi])).astype(BF16)
            o_sub = o_sub + _dot(_dot_nt(qi, kt).astype(BF16), v16[lo:hi])
        outs.append(o_sub)
    return o_inter + jnp.concatenate(outs, axis=0), st_new


def _hgrn_fast_blocks(dirs):
    c, sub = HG_CHUNK, HG_FAST_SUB
    n_chunk = ROW_BLK // c
    units = [(d, n_chunk - 1 - step if d['reverse'] else step)
             for step in range(n_chunk) for d in dirs]
    qs, xs, eb, scores, vals, outs = {}, {}, {}, {}, {}, {}
    for d, ci in units:
        reverse, u = d['reverse'], (id(d), ci)
        rows = pl.ds(ci * c, c)
        q, k, bcum, v = d['q_s'][rows, :], d['k_s'][rows, :], d['b_s'][rows, :], d['v_s'][rows, :]
        v16 = v.astype(BF16)
        last = 0 if reverse else c - 1
        b_last = bcum[last:last + 1]
        qs[u] = (q * jnp.exp(bcum)).astype(BF16)
        eb[u] = jnp.exp(b_last)
        xs[u] = _dot_tn(v16, (k * jnp.exp(b_last - bcum)).astype(BF16))
        for i in range(c // sub):
            r0 = i * sub
            lo, hi = (r0, c) if reverse else (0, r0 + sub)
            ref_row = bcum[r0 + sub - 1:r0 + sub] if reverse else bcum[r0:r0 + 1]
            qi = (q[r0:r0 + sub] * jnp.exp(bcum[r0:r0 + sub] - ref_row)).astype(BF16)
            kt = (k[lo:hi] * jnp.exp(ref_row - bcum[lo:hi])).astype(BF16)
            scores[u, i] = _dot_nt(qi, kt)
            vals[u, i] = v16[lo:hi]
    for d, ci in units:
        reverse, u = d['reverse'], (id(d), ci)
        parts = []
        for i in range(c // sub):
            r0 = i * sub
            lo, hi = (r0, c) if reverse else (0, r0 + sub)
            key = lax.broadcasted_iota(jnp.int32, (sub, hi - lo), 1) + lo
            qry = lax.broadcasted_iota(jnp.int32, (sub, hi - lo), 0) + r0
            a = jnp.where((key >= qry) if reverse else (key <= qry), scores[u, i], 0.0)
            parts.append(_dot(a.astype(BF16), vals[u, i]))
        outs[u] = jnp.concatenate(parts, axis=0)
    st = {id(d): d['st'][...] for d in dirs}
    for d, ci in units:
        u = (id(d), ci)
        o = outs[u] + _dot_nt(qs[u], st[id(d)].astype(BF16))
        d['o'][pl.ds(ci * c, c), :] = o.astype(d['o'].dtype)
        st[id(d)] = st[id(d)] * eb[u] + xs[u]
    for d in dirs:
        d['st'][...] = st[id(d)]


def _hgrn_kernel(layer, *refs):
    n_in, n_scr = 4, 5
    outs = refs[2 * n_in:2 * n_in + 2]
    scr = refs[2 * n_in + 2:]
    dirs = []
    for di, reverse in enumerate((False, True)):
        q_ref, f_ref, v_ref, lb_ref = refs[di * n_in:(di + 1) * n_in]
        st, q_s, k_s, b_s, v_s = scr[di * n_scr:(di + 1) * n_scr]
        dirs.append(dict(reverse=reverse, q=q_ref, f=f_ref, v=v_ref, lb=lb_ref, o=outs[di], st=st,
                         q_s=q_s, k_s=k_s, b_s=b_s, v_s=v_s))

    @pl.when(pl.program_id(2) == 0)
    def _():
        for d in dirs:
            d['st'][...] = jnp.zeros_like(d['st'])

    r_i = lax.broadcasted_iota(jnp.int32, (ROW_BLK, ROW_BLK), 0)
    c_i = lax.broadcasted_iota(jnp.int32, (ROW_BLK, ROW_BLK), 1)
    same_chunk = r_i // HG_CHUNK == c_i // HG_CHUNK
    spread = jnp.zeros((1, LANES), F32)
    for d in dirs:
        lg = d['lb'][0]
        e = jnp.exp(lg - jnp.max(lg, axis=0, keepdims=True))
        p = e / jnp.sum(e, axis=0, keepdims=True)
        lb = jnp.zeros((1, LANES), F32)
        for i in range(1, layer + 1):
            lb = lb + p[i:i + 1]

        qp = d['q'][...].astype(F32)
        fp = d['f'][...].astype(F32)
        d['q_s'][...] = qp * _sigmoid(qp) * HEAD_W ** -0.5
        d['k_s'][...] = (1.0 - lb) * _sigmoid(-fp)
        d['v_s'][...] = d['v'][...].astype(F32)
        lf = jnp.log(jnp.maximum(lb + (1.0 - lb) * _sigmoid(fp), TINY))
        tri = same_chunk & ((c_i >= r_i) if d['reverse'] else (c_i <= r_i))
        tri = jnp.where(tri, 1.0, 0.0).astype(BF16)
        lf_hi, lf_lo = _split(lf)
        bcum = _dot(tri, lf_hi) + _dot(tri, lf_lo)
        d['b_s'][...] = bcum
        for r0 in range(0, ROW_BLK, HG_FAST_SUB):
            spread = jnp.maximum(
                spread, jnp.abs(bcum[r0:r0 + 1] - bcum[r0 + HG_FAST_SUB - 1:r0 + HG_FAST_SUB]))
    small = jnp.max(spread) < HG_SPREAD_MAX

    @pl.when(small)
    def _():
        _hgrn_fast_blocks(dirs)

    @pl.when(jnp.logical_not(small))
    def _():
        n_chunk = ROW_BLK // HG_CHUNK
        for d in dirs:
            st = d['st'][...]
            for ci in (range(n_chunk - 1, -1, -1) if d['reverse'] else range(n_chunk)):
                rows = pl.ds(ci * HG_CHUNK, HG_CHUNK)
                o, st = _hgrn_exact_chunk(d['q_s'][rows, :], d['k_s'][rows, :], d['b_s'][rows, :],
                                          d['v_s'][rows, :], st, d['reverse'])
                d['o'][rows, :] = o.astype(d['o'].dtype)
            d['st'][...] = st


def _hgrn_scan(rows, p_main, lb_logits, layer):
    n_all = p_main.shape[0]
    steps = rows.nct + rows.nlt
    depth = lb_logits.shape[1]

    def dir_specs(direction):
        reverse = direction == 1
        blk = lambda b, s: rows.scan_block(b, s, reverse)
        spec = lambda seg: pl.BlockSpec((ROW_BLK, HEAD_W),
                                        lambda b, h, s: (blk(b, s), seg * HEADS + h))
        ins = [spec(S_HG_Q), spec(S_HG_FB if reverse else S_HG_FF), spec(S_HG_I),
               pl.BlockSpec((1, depth, HEAD_W), lambda b, h, s: (direction, 0, h))]
        return ins, pl.BlockSpec((ROW_BLK, HEAD_W), lambda b, h, s: (blk(b, s), h))

    ins_f, out_f = dir_specs(0)
    ins_b, out_b = dir_specs(1)
    out = jax.ShapeDtypeStruct((n_all, SEG), BF16)
    args = (p_main, p_main, p_main, lb_logits)
    return pl.pallas_call(
        functools.partial(_hgrn_kernel, layer),
        grid=(rows.batch, HEADS, steps),
        in_specs=ins_f + ins_b,
        out_specs=[out_f, out_b],
        out_shape=[out, out],
        scratch_shapes=([pltpu.VMEM((HEAD_W, HEAD_W), F32)]
                        + [pltpu.VMEM((ROW_BLK, HEAD_W), F32)] * 4) * 2,
        compiler_params=_cp(("arbitrary", "arbitrary", "arbitrary")),
        name="hgrn2_scan",
    )(*args, *args)


def _mlstm_kernel(*refs):
    c = ROW_BLK
    n_in = 7
    dirs = [dict(reverse=False, ins=refs[0:n_in], o=refs[2 * n_in], s=refs[2 * n_in + 2],
                 m=refs[2 * n_in + 3]),
            dict(reverse=True, ins=refs[n_in:2 * n_in], o=refs[2 * n_in + 1], s=refs[2 * n_in + 4],
                 m=refs[2 * n_in + 5])]

    @pl.when(pl.program_id(2) == 0)
    def _():
        for d in dirs:
            d['s'][...] = jnp.zeros_like(d['s'])
            d['m'][...] = jnp.zeros_like(d['m'])

    r_i = lax.broadcasted_iota(jnp.int32, (c, c), 0)
    c_i = lax.broadcasted_iota(jnp.int32, (c, c), 1)
    for d in dirs:
        q_ref, k_ref, v_ref, ig_ref, fg_ref, ib_ref, fb_ref = d['ins']
        d['q'] = q_ref[...]
        d['kt'] = (k_ref[...].astype(F32) * HEAD_W ** -0.5).T
        d['v_cat'] = jnp.concatenate([v_ref[...], jnp.ones((c, HEAD_W), BF16)], axis=1)
        d['ig'] = ig_ref[0] + ib_ref[0][:, :1]
        fx = fg_ref[0] + fb_ref[0][:, :1]
        lf = jnp.minimum(fx, 0.0) - jnp.log1p(jnp.exp(-jnp.abs(fx)))
        cum = jnp.where((r_i >= c_i) if d['reverse'] else (r_i <= c_i), 1.0, 0.0).astype(BF16)
        lf_hi, lf_lo = _split(jnp.broadcast_to(lf, (8, c)))
        d['brow'] = (_dot(lf_hi, cum) + _dot(lf_lo, cum))[0:1]
    for d in dirs:
        d['qk'] = _dot(d['q'], d['kt'].astype(BF16))
        d['state'] = d['s'][...]
        d['q_state'] = _dot(d['q'], d['state'].astype(BF16))
    for d in dirs:
        b_rows = jnp.broadcast_to(d['brow'], (c, c))
        b_cols = b_rows.T
        keep = (c_i >= r_i) if d['reverse'] else (c_i <= r_i)
        dlog = jnp.where(keep, b_cols - b_rows + d['ig'], NEG_BIG)
        d['m_prev'] = d['m'][:, :1]
        inter = b_cols[:, :1] + d['m_prev']
        d['m_t'] = jnp.maximum(jnp.max(dlog, axis=-1, keepdims=True), inter)
        d['w'] = (jnp.exp(dlog - d['m_t']) * d['qk']).astype(BF16)
        d['w_inter'] = jnp.exp(inter - d['m_t'])
    for d in dirs:
        nd = _dot(d['w'], d['v_cat']) + d['w_inter'] * d['q_state']
        num, den = nd[:, :HEAD_W], nd[:, HEAD_W:]
        d['o'][...] = (num / jnp.maximum(jnp.abs(den), jnp.exp(-d['m_t']))).astype(d['o'].dtype)
    for d in dirs:
        last = 0 if d['reverse'] else c - 1
        m_new = d['m_t'][last:last + 1]
        b_last = d['brow'][:, last:last + 1]
        wk = jnp.exp(b_last - d['brow'] + d['ig'] - m_new)
        dec = jnp.exp(b_last + d['m_prev'] - m_new)
        d['s'][...] = dec * d['state'] + _dot((d['kt'] * wk).astype(BF16), d['v_cat'])
        d['m'][...] = jnp.broadcast_to(m_new, d['m'].shape)


def _mlstm_scan(rows, p_main, gates_t, i_bias, f_bias):
    n_all = p_main.shape[0]
    steps = rows.nct + rows.nlt

    def dir_specs(direction):
        reverse = direction == 1
        blk = lambda b, s: rows.scan_block(b, s, reverse)
        spec = lambda seg: pl.BlockSpec((ROW_BLK, HEAD_W),
                                        lambda b, h, s: (blk(b, s), seg * HEADS + h))
        gate = lambda which: pl.BlockSpec((1, 1, ROW_BLK),
                                          lambda b, h, s: (which * HEADS + h, 0, blk(b, s)))
        bias = pl.BlockSpec((1, 1, LANES), lambda b, h, s: (direction * HEADS + h, 0, 0))
        ins = [spec(S_ML_Q), spec(S_ML_K), spec(S_ML_V), gate(direction), gate(2 + direction),
               bias, bias]
        return ins, pl.BlockSpec((ROW_BLK, HEAD_W), lambda b, h, s: (blk(b, s), h))

    ins_f, out_f = dir_specs(0)
    ins_b, out_b = dir_specs(1)
    out = jax.ShapeDtypeStruct((n_all, SEG), BF16)
    args = (p_main, p_main, p_main, gates_t, gates_t, i_bias, f_bias)
    return pl.pallas_call(
        _mlstm_kernel,
        grid=(rows.batch, HEADS, steps),
        in_specs=ins_f + ins_b,
        out_specs=[out_f, out_b],
        out_shape=[out, out],
        scratch_shapes=[pltpu.VMEM((HEAD_W, 2 * HEAD_W), F32), pltpu.VMEM((1, LANES), F32)] * 2,
        compiler_params=_cp(("arbitrary", "arbitrary", "arbitrary")),
        name="mlstm_scan",
    )(*args, *args)


def _da_prep_kernel(q_ref, k_ref, v_ref, cos_ref, sin_ref, qg_ref, kg_ref, qt_ref, ko_ref, vt_ref):
    r_i = lax.broadcasted_iota(jnp.int32, (LANES, LANES), 0)
    c_i = lax.broadcasted_iota(jnp.int32, (LANES, LANES), 1)
    blockdiag = jnp.where((r_i // DA_QK) == (c_i // DA_QK), 1.0 / DA_QK, 0.0).astype(BF16)
    cos, sin = cos_ref[...], sin_ref[...]
    lane = lax.broadcasted_iota(jnp.int32, cos.shape, 1)
    first_half = (lane % DA_QK) < (DA_QK // 2)

    def qk_norm_rope(x, gain):
        x_hi, x_lo = _split(x * x)
        ms = _dot(x_hi, blockdiag) + _dot(x_lo, blockdiag)
        y = x * lax.rsqrt(ms + EPS) * gain
        rot = jnp.where(first_half, -pltpu.roll(y, LANES - DA_QK // 2, 1),
                        pltpu.roll(y, DA_QK // 2, 1))
        return y * cos + rot * sin

    for h in range(HEADS):
        cols = slice(h * HEAD_W, (h + 1) * HEAD_W)
        q = qk_norm_rope(q_ref[:, cols].astype(F32), qg_ref[...]) * (DA_QK ** -0.5 * LOG2E)
        qt_ref[cols, :] = q.T.astype(BF16)
        ko_ref[:, cols] = qk_norm_rope(k_ref[:, cols].astype(F32), kg_ref[...]).astype(BF16)
        vt_ref[cols, :] = v_ref[:, cols].astype(F32).T.astype(BF16)


def _da_prep(rows, p_main, cos_tab, sin_tab, q_gain, k_gain):
    n_all = p_main.shape[0]
    tm = ROW_BLK
    out_t = jax.ShapeDtypeStruct((SEG, n_all), BF16)

    def tab_idx(i):
        lat = i % rows.nlt
        ctx = rows.nlt + (i - rows.batch * rows.nlt) % rows.nct
        return (jnp.where(i < rows.batch * rows.nlt, lat, ctx), 0)

    out = jax.ShapeDtypeStruct((n_all, SEG), BF16)
    return pl.pallas_call(
        _da_prep_kernel,
        grid=(n_all // tm,),
        in_specs=[pl.BlockSpec((tm, SEG), lambda i: (i, S_DA_Q)),
                  pl.BlockSpec((tm, SEG), lambda i: (i, S_DA_K)),
                  pl.BlockSpec((tm, SEG), lambda i: (i, S_DA_V)),
                  pl.BlockSpec((tm, LANES), tab_idx),
                  pl.BlockSpec((tm, LANES), tab_idx),
                  pl.BlockSpec((1, LANES), lambda i: (0, 0)),
                  pl.BlockSpec((1, LANES), lambda i: (0, 0))],
        out_specs=[pl.BlockSpec((SEG, tm), lambda i: (0, i)),
                   pl.BlockSpec((tm, SEG), lambda i: (i, 0)),
                   pl.BlockSpec((SEG, tm), lambda i: (0, i))],
        out_shape=[out_t, out, out_t],
        compiler_params=_cp(("arbitrary",)),
        name="da_qk_prep",
    )(p_main, p_main, p_main, cos_tab, sin_tab, q_gain, k_gain)


def _attn_kernel(lam_init, n_lat_q_blocks, n_lat_keys, tk, qt_ref, kc_ref, vct_ref, kl_ref, vlt_ref,
                 lam_ref, sn_ref, o_ref, acc_ref, m_ref, sa_ref, sb_ref, kn_ref):
    tq = qt_ref.shape[1]
    qt = qt_ref[...]
    chan = lax.broadcasted_iota(jnp.int32, qt.shape, 0)
    zero = jnp.zeros_like(qt)
    q2t = jnp.concatenate([jnp.where(chan < DA_QK, qt, zero), jnp.where(chan >= DA_QK, qt, zero)],
                          axis=1)

    def weighted_values(vt, st, m):
        p = jnp.exp2((st - m).astype(BF16))
        ones = jnp.ones((ATTN_PAD_ROWS, vt.shape[1]), BF16)
        return _dot(jnp.concatenate([vt, ones], axis=0), p)

    st = _dot(kc_ref[...], q2t)
    m0 = jnp.max(st, axis=0, keepdims=True)
    m_ref[...] = m0
    acc_ref[...] = weighted_values(vct_ref[...], st, m0)

    n_chunks = n_lat_keys // tk
    unroll = _pick(n_chunks, (4, 2, 1))
    bufs = (sa_ref, sb_ref) if unroll > 1 else (sa_ref, sa_ref)

    def chunk(c):
        return pl.ds(pl.multiple_of(c * tk, tk), tk)

    def scores(c, buf):
        st = _dot(kl_ref[chunk(c), :], q2t)
        buf[...] = st
        return jnp.max(st, axis=0, keepdims=True)

    def absorb(c, buf, m_cur):
        m_old = m_ref[...]
        m_new = jnp.maximum(m_old, m_cur)
        pv = weighted_values(vlt_ref[:, chunk(c)], buf[...], m_new)
        acc_ref[...] = jnp.exp2(m_old - m_new) * acc_ref[...] + pv
        m_ref[...] = m_new

    def body(j, m_pend):
        for u in range(unroll):
            c = j * unroll + u
            if unroll > 1:
                m_next = scores(jnp.minimum(c + 1, n_chunks - 1), bufs[(u + 1) % 2])
                absorb(c, bufs[u % 2], m_pend)
            else:
                absorb(c, bufs[0], m_pend)
                m_next = scores(jnp.minimum(c + 1, n_chunks - 1), bufs[0])
            m_pend = m_next
        return m_pend

    n_trips = jnp.where(pl.program_id(2) < n_lat_q_blocks, n_chunks // unroll, 0)

    @pl.when(pl.program_id(2) == 0)
    def _():
        r_i = lax.broadcasted_iota(jnp.int32, (LANES, LANES), 0)
        c_i = lax.broadcasted_iota(jnp.int32, (LANES, LANES), 1)
        same_map = jnp.where((r_i // DA_QK) == (c_i // DA_QK), 1.0, 0.0).astype(BF16)

        def max_sq_norm(k):
            kf = k.astype(F32)
            hi, lo = _split(kf * kf)
            return jnp.max(_dot(hi, same_map) + _dot(lo, same_map), axis=0, keepdims=True)

        def body_norm(c, best):
            return jnp.maximum(best, max_sq_norm(kl_ref[chunk(c), :]))
        kn_ref[...] = lax.fori_loop(0, n_chunks, body_norm, max_sq_norm(kc_ref[...]))

    qf = q2t.astype(F32)
    q_sq = jnp.sum(qf * qf, axis=0, keepdims=True)
    col = lax.broadcasted_iota(jnp.int32, q_sq.shape, 1)
    k_sq = jnp.where(col < tq, kn_ref[:, 0:1], kn_ref[:, DA_QK:DA_QK + 1])
    bound = jnp.sqrt(q_sq * k_sq) * 1.01 + 0.01
    bounded = jnp.max(bound - m0) <= ATTN_RANGE_MAX

    def body_bounded(j, carry):
        for u in range(unroll):
            c = j * unroll + u
            st = _dot(kl_ref[chunk(c), :], q2t)
            acc_ref[...] += weighted_values(vlt_ref[:, chunk(c)], st, m0)
        return carry

    @pl.when(bounded)
    def _():
        lax.fori_loop(0, n_trips, body_bounded, 0)

    @pl.when(jnp.logical_not(bounded))
    def _():
        lax.fori_loop(0, n_trips, body, scores(0, bufs[0]))

    lv = lam_ref[...]
    lam = (jnp.exp(jnp.sum(lv[0:1] * lv[1:2], axis=-1, keepdims=True))
           - jnp.exp(jnp.sum(lv[2:3] * lv[3:4], axis=-1, keepdims=True)) + lam_init)
    o = acc_ref[0:HEAD_W, :] / acc_ref[HEAD_W:HEAD_W + 1, :]
    a = o[:, :tq] - lam * o[:, tq:]
    ms = jnp.mean(a * a, axis=0, keepdims=True)
    y = a * lax.rsqrt(ms + EPS) * sn_ref[...] * (1.0 - lam_init)
    o_ref[...] = y.T.astype(o_ref.dtype)


def _attention_call(rows, qt, kh, vt, lam_vec, sub_gain, lam_init, n_rows, prev):
    assert rows.nct == 1
    ctx_blk = lambda b: rows.batch * rows.nlt + b
    tk = _pick(rows.seq, (512, 256))
    if prev is None:
        tq = _pick(rows.seq, (512, 256))
        n_q = rows.seq // tq
        q_blk = lambda b, i: b * n_q + i
        kern = functools.partial(_attn_kernel, lam_init, n_q, rows.seq, tk)
        extra_specs, extra_args, aliases = [], [], {}
    else:
        tq, n_q = ROW_BLK, 0
        q_blk = lambda b, i: ctx_blk(b)
        attn = functools.partial(_attn_kernel, lam_init, n_q, rows.seq, tk)
        kern = lambda *refs: attn(*refs[:7], *refs[8:])
        extra_specs, extra_args, aliases = [pl.BlockSpec(memory_space=pl.ANY)], [prev], {7: 0}
    return pl.pallas_call(
        kern,
        grid=(rows.batch, HEADS, max(n_q, 1)),
        in_specs=[pl.BlockSpec((HEAD_W, tq), lambda b, h, i: (h, q_blk(b, i))),
                  pl.BlockSpec((ROW_BLK, HEAD_W), lambda b, h, i: (ctx_blk(b), h)),
                  pl.BlockSpec((HEAD_W, ROW_BLK), lambda b, h, i: (h, ctx_blk(b))),
                  pl.BlockSpec((rows.seq, HEAD_W), lambda b, h, i: (b, h)),
                  pl.BlockSpec((HEAD_W, rows.seq), lambda b, h, i: (h, b)),
                  pl.BlockSpec(lam_vec.shape, lambda b, h, i: (0, 0)),
                  pl.BlockSpec((HEAD_W, 1), lambda b, h, i: (0, 0))] + extra_specs,
        out_specs=pl.BlockSpec((tq, HEAD_W), lambda b, h, i: (q_blk(b, i), h)),
        out_shape=jax.ShapeDtypeStruct((n_rows, SEG), BF16),
        scratch_shapes=[pltpu.VMEM((HEAD_W + ATTN_PAD_ROWS, 2 * tq), F32),
                        pltpu.VMEM((1, 2 * tq), F32), pltpu.VMEM((tk, 2 * tq), F32),
                        pltpu.VMEM((tk, 2 * tq), F32), pltpu.VMEM((1, LANES), F32)],
        input_output_aliases=aliases,
        compiler_params=_cp(("arbitrary", "arbitrary", "arbitrary")),
        name="diff_attention" if prev is None else "diff_attention_ctx",
    )(qt, kh, vt, kh, vt, lam_vec, sub_gain, *extra_args)


def _attention(rows, qt, kh, vt, lam_vec, sub_gain, lam_init, n_rows):
    da = _attention_call(rows, qt, kh, vt, lam_vec, sub_gain, lam_init, n_rows, None)
    if n_rows > rows.n_lat:
        da = _attention_call(rows, qt, kh, vt, lam_vec, sub_gain, lam_init, n_rows, da)
    return da


def _merge_kernel(ohf_ref, ohb_ref, hg_ref, da_ref, omf_ref, omb_ref, mo_ref, g0_ref, g1_ref,
                  g2_ref, hn_ref, mn_ref, wb_ref, o_ref, h_ref):
    @pl.when(pl.program_id(1) == 0)
    def _():
        for h in range(HEADS):
            cols = slice(h * HEAD_W, (h + 1) * HEAD_W)
            o = ohf_ref[:, cols].astype(F32) + ohb_ref[:, cols].astype(F32)
            y = o * lax.rsqrt(jnp.mean(o * o, axis=-1, keepdims=True) + EPS) * hn_ref[...]
            g = hg_ref[:, cols].astype(F32)
            h_ref[0, :, cols] = (y * (g * _sigmoid(g))).astype(BF16)
            o = omf_ref[:, cols].astype(F32) + omb_ref[:, cols].astype(F32)
            y = o * lax.rsqrt(jnp.mean(o * o, axis=-1, keepdims=True) + EPS) * mn_ref[...]
            h_ref[2, :, cols] = (y * _sigmoid(mo_ref[:, cols].astype(F32))).astype(BF16)
        h_ref[1] = da_ref[...]

    y = _sigmoid(g0_ref[...].astype(F32)) * _dot(h_ref[0], wb_ref[0])
    y = y + _sigmoid(g1_ref[...].astype(F32)) * _dot(h_ref[1], wb_ref[1])
    y = y + _sigmoid(g2_ref[...].astype(F32)) * _dot(h_ref[2], wb_ref[2])
    o_ref[...] = y.astype(o_ref.dtype)


def _merge(p_main, p_merge, ohf, ohb, da, omf, omb, hg_gain, ml_gain, w_branch, d):
    n_all = da.shape[0]
    tm = 512
    tn = _pick(d, (512, 256, 128))
    row = lambda i, j: (i, 0)

    def gate_spec(jj):
        return pl.BlockSpec((tm, tn), lambda i, j: (i, jj * (d // tn) + j))

    return pl.pallas_call(
        _merge_kernel,
        grid=(n_all // tm, d // tn),
        in_specs=[pl.BlockSpec((tm, SEG), row), pl.BlockSpec((tm, SEG), row),
                  pl.BlockSpec((tm, SEG), lambda i, j: (i, S_HG_G)),
                  pl.BlockSpec((tm, SEG), row),
                  pl.BlockSpec((tm, SEG), row), pl.BlockSpec((tm, SEG), row),
                  pl.BlockSpec((tm, SEG), lambda i, j: (i, S_ML_O)),
                  gate_spec(0), gate_spec(1), gate_spec(2),
                  pl.BlockSpec((1, LANES), lambda i, j: (0, 0)),
                  pl.BlockSpec((1, LANES), lambda i, j: (0, 0)),
                  pl.BlockSpec((3, SEG, tn), lambda i, j: (0, 0, j))],
        out_specs=pl.BlockSpec((tm, tn), lambda i, j: (i, j)),
        out_shape=jax.ShapeDtypeStruct((n_all, d), BF16),
        scratch_shapes=[pltpu.VMEM((3, tm, SEG), BF16)],
        compiler_params=_cp(("arbitrary", "arbitrary")),
        name="branch_merge",
    )(ohf, ohb, p_main, da, omf, omb, p_main, p_merge, p_merge, p_merge, hg_gain, ml_gain,
      w_branch)


def _proj_resid_kernel(n_first, a_ref, w_ref, g_ref, *refs):
    x_parts, o_ref = refs[:-1], refs[-1]
    x = _row_tile(x_parts, pl.program_id(1), n_first)
    o_ref[...] = x + g_ref[0] * _dot(a_ref[...], w_ref[...])


def _proj_resid(rows, a, w, x_all, modsflat, layer, which_gate):
    n_all, k = a.shape
    d = w.shape[1]
    tm = 512
    tn = _pick(d, (1024, 512, 256, 128))
    x_specs, x_args, n_first = _split_rows(x_all, (tm, tn), 1)
    return pl.pallas_call(
        functools.partial(_proj_resid_kernel, n_first),
        grid=(d // tn, n_all // tm),
        in_specs=[pl.BlockSpec((tm, k), lambda j, i: (i, 0)),
                  pl.BlockSpec((k, tn), lambda j, i: (0, j)),
                  pl.BlockSpec((1, 1, tn),
                               lambda j, i: (rows.mod_row(layer, i, tm, which_gate), 0, j))]
        + x_specs,
        out_specs=pl.BlockSpec((tm, tn), lambda j, i: (i, j)),
        out_shape=jax.ShapeDtypeStruct((n_all, d), F32),
        compiler_params=_cp(("arbitrary", "arbitrary")),
        name="out_proj_residual",
    )(a, w, modsflat, *x_args)


def _swiglu_hidden(x, w1_ref, w3_ref):
    h1 = _dot(x, w1_ref[...])
    h3 = _dot(x, w3_ref[...])
    return (h1 * _sigmoid(h1)) * h3


def _shared_expert_kernel(tok_ref, w1_ref, w3_ref, w2_ref, o_ref):
    a = _swiglu_hidden(tok_ref[...], w1_ref, w3_ref)
    o_ref[...] = _dot(a.astype(BF16), w2_ref[...]).astype(o_ref.dtype)


def _shared_expert(tok, w1, w3, w2):
    n_all, d = tok.shape
    de = w1.shape[1]
    tm = 512
    full = lambda i: (0, 0)
    return pl.pallas_call(
        _shared_expert_kernel,
        grid=(n_all // tm,),
        in_specs=[pl.BlockSpec((tm, d), lambda i: (i, 0)),
                  pl.BlockSpec((d, de), full), pl.BlockSpec((d, de), full),
                  pl.BlockSpec((de, d), full)],
        out_specs=pl.BlockSpec((tm, d), lambda i: (i, 0)),
        out_shape=jax.ShapeDtypeStruct((n_all, d), BF16),
        compiler_params=_cp(("arbitrary",)),
        name="moe_shared_expert",
    )(tok, w1, w3, w2)


def _moe_plan_kernel(route_ref, pos_ref, cnt_ref, carry_ref):
    @pl.when(pl.program_id(0) == 0)
    def _():
        carry_ref[...] = jnp.zeros_like(carry_ref)

    r = route_ref[...]
    tm = r.shape[0]
    lane = lax.broadcasted_iota(jnp.int32, r.shape, 1).astype(F32)
    mask = jnp.zeros_like(r)
    for k in range(TOP_K):
        mask = mask + jnp.where(lane == r[:, k:k + 1], 1.0, 0.0)
    r_i = lax.broadcasted_iota(jnp.int32, (tm, tm), 0)
    c_i = lax.broadcasted_iota(jnp.int32, (tm, tm), 1)
    before = jnp.where(c_i < r_i, 1.0, 0.0).astype(BF16)
    rank = _dot(before, mask.astype(BF16)) + carry_ref[...]
    out = jnp.zeros_like(r)
    for k in range(TOP_K):
        pk = jnp.sum(jnp.where(lane == r[:, k:k + 1], rank, 0.0), axis=-1, keepdims=True)
        out = jnp.where(lane == k, pk, out)
    pos_ref[...] = out
    carry = carry_ref[...] + jnp.sum(mask, axis=0, keepdims=True)
    carry_ref[...] = carry
    cnt_ref[...] = jnp.broadcast_to(carry, cnt_ref.shape)


def _moe_plan(route):
    n = route.shape[0]
    tm = ROW_BLK
    return pl.pallas_call(
        _moe_plan_kernel,
        grid=(n // tm,),
        in_specs=[pl.BlockSpec((tm, LANES), lambda i: (i, 0))],
        out_specs=[pl.BlockSpec((tm, LANES), lambda i: (i, 0)),
                   pl.BlockSpec((8, LANES), lambda i: (0, 0))],
        out_shape=[jax.ShapeDtypeStruct((n, LANES), F32), jax.ShapeDtypeStruct((8, LANES), F32)],
        scratch_shapes=[pltpu.VMEM((1, LANES), F32)],
        compiler_params=_cp(("arbitrary",)),
        name="moe_plan",
    )(route)


def _moe_routed_kernel(ns, te_ref, na_ref, st_ref, tok_hbm, sw_ref, w1_ref, w3_ref, w2_ref, o_ref,
                       xbuf, x_ref, sem):
    tm = MOE_TILE
    t = pl.program_id(0)
    n_active = na_ref[0]
    par = t % 2

    def gather_copy(tile, buf, r):
        tok = st_ref[tile * tm + r]
        return pltpu.make_async_copy(
            tok_hbm.at[pl.ds(pl.multiple_of(tok * ns, ns), ns), :],
            xbuf.at[buf, pl.ds(pl.multiple_of(r * ns, ns), ns), :], sem.at[buf])

    def for_each_copy(tile, buf, act):
        def body(r4, carry):
            for u in range(4):
                act(gather_copy(tile, buf, r4 * 4 + u), u)
            return carry
        lax.fori_loop(0, tm // 4, body, 0)

    start = lambda cp, u: cp.start(priority=u % 2)

    @pl.when(t == 0)
    def _():
        for_each_copy(0, 0, start)

    @pl.when(t + 1 < n_active)
    def _():
        for_each_copy(t + 1, 1 - par, start)

    @pl.when(t < n_active)
    def _():
        for_each_copy(t, par, lambda cp, u: cp.wait())
        d = x_ref.shape[1]

        def emit(s, lo, hi):
            x_ref[:, s * LANES:(s + 1) * LANES] = lo.astype(BF16)
            x_ref[:, d // 2 + s * LANES:d // 2 + (s + 1) * LANES] = hi.astype(BF16)
        _load_slabs(xbuf.at[par], 0, tm, ns, emit)
        a = _swiglu_hidden(x_ref[...], w1_ref, w3_ref) * sw_ref[...]
        _store_slabs(_dot(a.astype(BF16), w2_ref[...]), o_ref)

    @pl.when(t >= n_active)
    def _():
        o_ref[...] = jnp.zeros_like(o_ref)


def _moe_routed(tok_slabs, slot_token, slot_w, tile_expert, n_active, w1, w3, w2, layer):
    _, n_exp, d, de = w1.shape
    ns = d // (2 * LANES)
    tm = MOE_TILE
    n_slots = slot_token.shape[0]
    nt = n_slots // tm
    grid_spec = pltpu.PrefetchScalarGridSpec(
        num_scalar_prefetch=3,
        grid=(nt,),
        in_specs=[pl.BlockSpec(memory_space=pl.ANY),
                  pl.BlockSpec((tm, 1), lambda t, te, na, st: (t, 0)),
                  pl.BlockSpec((None, None, d, de), lambda t, te, na, st: (layer, te[t], 0, 0)),
                  pl.BlockSpec((None, None, d, de), lambda t, te, na, st: (layer, te[t], 0, 0)),
                  pl.BlockSpec((None, None, de, d), lambda t, te, na, st: (layer, te[t], 0, 0))],
        out_specs=pl.BlockSpec((tm * ns, LANES), lambda t, te, na, st: (t, 0)),
        scratch_shapes=[pltpu.VMEM((2, tm * ns, LANES), jnp.int32), pltpu.VMEM((tm, d), BF16),
                        pltpu.SemaphoreType.DMA((2,))],
    )
    return pl.pallas_call(
        functools.partial(_moe_routed_kernel, ns),
        grid_spec=grid_spec,
        out_shape=jax.ShapeDtypeStruct((n_slots * ns, LANES), jnp.int32),
        compiler_params=_cp(("arbitrary",)),
        name="moe_routed_experts",
    )(tile_expert, n_active, slot_token, tok_slabs, slot_w, w1, w3, w2)


def _moe_combine_kernel(ns, s4_ref, ys_hbm, ysh_ref, x_ref, g_ref, o_ref, ybuf, sem):
    tm = x_ref.shape[0]
    d = x_ref.shape[1]
    i = pl.program_id(0)
    par = i % 2

    def gather_copy(tile, buf, row, k):
        slot = s4_ref[(tile * tm + row) * TOP_K + k]
        return pltpu.make_async_copy(
            ys_hbm.at[pl.ds(pl.multiple_of(slot * ns, ns), ns), :],
            ybuf.at[buf, pl.ds(pl.multiple_of((k * tm + row) * ns, ns), ns), :], sem.at[buf])

    def for_each_copy(tile, buf, act):
        def body(row, carry):
            for k in range(TOP_K):
                act(gather_copy(tile, buf, row, k), k)
            return carry
        lax.fori_loop(0, tm, body, 0)

    start = lambda cp, k: cp.start(priority=k % 2)

    @pl.when(i == 0)
    def _():
        for_each_copy(0, 0, start)

    @pl.when(i + 1 < pl.num_programs(0))
    def _():
        for_each_copy(i + 1, 1 - par, start)

    for_each_copy(i, par, lambda cp, k: cp.wait())

    gate = g_ref[0]
    for s in range(ns):
        lo_cols = slice(s * LANES, (s + 1) * LANES)
        hi_cols = slice(d // 2 + s * LANES, d // 2 + (s + 1) * LANES)
        y_lo = ysh_ref[:, lo_cols].astype(F32)
        y_hi = ysh_ref[:, hi_cols].astype(F32)
        for k in range(TOP_K):
            lo, hi = _unpack_pair(ybuf[par, pl.ds(k * tm * ns + s, tm, stride=ns), :])
            y_lo = y_lo + lo
            y_hi = y_hi + hi
        o_ref[:, lo_cols] = x_ref[:, lo_cols] + gate[:, lo_cols] * y_lo
        o_ref[:, hi_cols] = x_ref[:, hi_cols] + gate[:, hi_cols] * y_hi


def _moe_combine(rows, x_all, y_shared, ys_slabs, slot4, modsflat, layer, which_gate, n_rows):
    d = x_all.shape[1]
    ns = d // (2 * LANES)
    tm = 128
    grid_spec = pltpu.PrefetchScalarGridSpec(
        num_scalar_prefetch=1,
        grid=(n_rows // tm,),
        in_specs=[pl.BlockSpec(memory_space=pl.ANY),
                  pl.BlockSpec((tm, d), lambda i, s4: (i, 0)),
                  pl.BlockSpec((tm, d), lambda i, s4: (i, 0)),
                  pl.BlockSpec((1, 1, d),
                               lambda i, s4: (rows.mod_row(layer, i, tm, which_gate), 0, 0))],
        out_specs=pl.BlockSpec((tm, d), lambda i, s4: (i, 0)),
        scratch_shapes=[pltpu.VMEM((2, TOP_K * tm * ns, LANES), jnp.int32),
                        pltpu.SemaphoreType.DMA((2,))],
    )
    return pl.pallas_call(
        functools.partial(_moe_combine_kernel, ns),
        grid_spec=grid_spec,
        out_shape=jax.ShapeDtypeStruct((n_rows, d), F32),
        compiler_params=_cp(("arbitrary",)),
        name="moe_combine_residual",
    )(slot4, ys_slabs, y_shared, x_all, modsflat)


def _moe_slots(route, pos, counts, n_exp):
    n = route.shape[0]
    tm = MOE_TILE
    idx4 = route[:, :TOP_K].astype(jnp.int32)
    w4 = route[:, TOP_K:2 * TOP_K]
    pos4 = pos[:, :TOP_K].astype(jnp.int32)
    cnt = counts[0, :n_exp].astype(jnp.int32)
    padded = (cnt + tm - 1) // tm * tm
    ends = jnp.cumsum(padded)
    starts = ends - padded
    slot4 = (starts[idx4] + pos4).reshape(-1)
    n_slots = n * TOP_K + n_exp * tm
    nt = n_slots // tm
    tile_start = jnp.arange(nt, dtype=jnp.int32) * tm
    tile_expert = jnp.minimum(
        jnp.sum((ends[None, :] <= tile_start[:, None]).astype(jnp.int32), axis=1), n_exp - 1)
    n_active = (ends[-1:] // tm).astype(jnp.int32)
    token_of = jnp.repeat(jnp.arange(n, dtype=jnp.int32), TOP_K)
    pairs = jnp.stack([token_of, lax.bitcast_convert_type(w4.reshape(-1), jnp.int32)], axis=1)
    slot_meta = jnp.zeros((n_slots, 2), jnp.int32).at[slot4].set(pairs)
    slot_token = slot_meta[:, 0]
    slot_w = lax.bitcast_convert_type(slot_meta[:, 1:2], F32)
    return slot4, slot_token, slot_w, tile_expert, n_active


def _rope_tables(seq, ctx_len):
    n = DA_QK // 4
    t = jnp.arange(seq)
    inv = ROPE_THETA ** (-jnp.arange(n, dtype=F32) / n)
    row = (t // GRID_W).astype(F32)
    col = (t % GRID_W).astype(F32)
    ang = jnp.concatenate([row[:, None] * inv, col[:, None] * inv], axis=-1)
    ang = jnp.concatenate([ang, jnp.zeros((ctx_len, 2 * n), F32)], axis=0)
    return jnp.tile(jnp.cos(ang), (1, 4)), jnp.tile(jnp.sin(ang), (1, 4))


def kernel(x, c, ctx, c_ctx, norm1, norm2, w_ada, b_ada, w_in, hg_lb_logits, hg_norm, da_q_norm,
           da_k_norm, da_lambda, da_sub_norm, ml_igate_bias, ml_fgate_bias, ml_norm, w_branch,
           w_out, router_w, router_bias, exp_w1, exp_w3, exp_w2, sh_w1, sh_w3, sh_w2):
    batch, seq, d = x.shape
    ctx_len = ctx.shape[1]
    depth = w_ada.shape[0]
    n_exp = router_w.shape[-1]
    rows = _Rows(batch, seq, ctx_len)
    assert batch + 1 <= 16 and n_exp <= LANES

    x_all = (x.reshape(batch * seq, d), ctx.reshape(batch * ctx_len, d))
    cvec = jnp.zeros((16, d), F32).at[0].set(c_ctx).at[1:1 + batch].set(c)
    mods = _mods(cvec, w_ada, b_ada)
    modsflat = mods[:, :rows.n_groups].reshape(depth * rows.n_groups * 6, 1, d)
    cos_tab, sin_tab = _rope_tables(seq, ctx_len)
    gate_lo = N_SEG * SEG
    w_in_t = jnp.swapaxes(w_in, 1, 2)
    exp_w1_b, exp_w3_b, exp_w2_b = (w.astype(BF16) for w in (exp_w1, exp_w3, exp_w2))

    for l in range(depth):
        n_rows = rows.n_all if l < depth - 1 else rows.n_lat
        lam_init = 0.8 - 0.6 * math.exp(-0.3 * l)
        hx = _norm(rows, x_all, rows.n_all, norm1[l], modsflat, l, 1, 0)
        p_main = _matmul_wt(hx, w_in_t, l, 0, gate_lo, BF16)
        p_gate = _matmul_wt(hx, w_in_t, l, gate_lo, LANES, F32)
        p_merge = _matmul_wt(hx, w_in_t, l, gate_lo + N_GATE_COLS, N_BRANCH * d, BF16, m=n_rows)
        gates_t = p_gate[:, :N_GATE_COLS].T.reshape(N_GATE_COLS, 1, rows.n_all)

        ohf, ohb = _hgrn_scan(rows, p_main, hg_lb_logits, l)

        tile2 = lambda g: jnp.tile(g.reshape(1, DA_QK), (1, 2))
        qt, kh, vt = _da_prep(rows, p_main, cos_tab, sin_tab, tile2(da_q_norm[l]),
                              tile2(da_k_norm[l]))
        da = _attention(rows, qt, kh, vt, da_lambda[l], da_sub_norm[l].reshape(HEAD_W, 1),
                        lam_init, n_rows)

        bias = lambda bv: jnp.broadcast_to(bv.reshape(2 * HEADS, 1, 1), (2 * HEADS, 1, LANES))
        i_b, f_b = bias(ml_igate_bias[l]), bias(ml_fgate_bias[l])
        omf, omb = _mlstm_scan(rows, p_main, gates_t, i_b, f_b)

        ymid = _merge(p_main, p_merge, ohf, ohb, da, omf, omb, hg_norm[l].reshape(1, HEAD_W),
                      ml_norm[l].reshape(1, HEAD_W), w_branch[l].astype(BF16), d)
        x_all = _proj_resid(rows, ymid, w_out[l].astype(BF16), x_all, modsflat, l, 2)

        rw = jnp.pad(router_w[l], ((0, 0), (0, LANES - n_exp)))
        rw_hi = rw.astype(BF16)
        rw_lo = (rw - rw_hi.astype(F32)).astype(BF16)
        rb = jnp.pad(router_bias[l], (0, LANES - n_exp)).reshape(1, LANES)
        tok, tok_slabs, route = _norm(rows, x_all, n_rows, norm2[l], modsflat, l, 4, 3,
                                      router=(rw_hi, rw_lo, rb, n_exp))
        pos, counts = _moe_plan(route)
        slot4, slot_token, slot_w, tile_expert, n_active = _moe_slots(route, pos, counts, n_exp)
        ys = _moe_routed(tok_slabs, slot_token, slot_w, tile_expert, n_active,
                         exp_w1_b, exp_w3_b, exp_w2_b, l)
        y_sh = _shared_expert(tok, sh_w1[l].astype(BF16), sh_w3[l].astype(BF16),
                              sh_w2[l].astype(BF16))
        x_all = _moe_combine(rows, x_all, y_sh, ys, slot4, modsflat, l, 5, n_rows)

    return x_all.reshape(batch, seq, d)
```

```python
import functools
import math

import jax
import jax.numpy as jnp
from jax import lax
from jax.experimental import pallas as pl
from jax.experimental.pallas import tpu as pltpu

F32 = jnp.float32
BF16 = jnp.bfloat16

EPS = 1e-6
NEG_BIG = -1e30
TINY = 1e-30
GRID_W = 64
ROPE_THETA = 10000.0
ROUTED_SCALE = 2.5
TOP_K = 4
LOG2E = 1.4426950408889634

HEADS = 12
HEAD_W = 128
SEG = HEADS * HEAD_W
DA_QK = 64
N_GATE_COLS = 4 * HEADS

S_HG_Q, S_HG_FF, S_HG_FB, S_HG_I, S_HG_G = 0, 1, 2, 3, 4
S_DA_Q, S_DA_K, S_DA_V = 5, 6, 7
S_ML_Q, S_ML_K, S_ML_V, S_ML_O = 8, 9, 10, 11
N_SEG = 12
N_BRANCH = 3

LANES = 128
SUBLANES = 8
ROW_BLK = 256
SCAN_HEADS = 2
HG_CHUNK = 64
HG_SUB = 16
HG_FAST_SUB = 32
HG_SPREAD_MAX = 60.0
MOE_TILE = 256
ATTN_PAD_ROWS = 16
ATTN_RANGE_MAX = 60.0
VMEM_LIMIT = 56 * 1024 * 1024


def _cp(sem, vmem=VMEM_LIMIT):
    return pltpu.CompilerParams(dimension_semantics=sem, vmem_limit_bytes=vmem)


def _pick(n, cands):
    for c in cands:
        if n % c == 0:
            return c
    raise ValueError(f"no tile for {n} in {cands}")


def _dot(a, b):
    return jnp.dot(a, b, preferred_element_type=F32)


def _dot_nt(a, b):
    return lax.dot_general(a, b, (((1,), (1,)), ((), ())), preferred_element_type=F32)


def _dot_tn(a, b):
    return lax.dot_general(a, b, (((0,), (0,)), ((), ())), preferred_element_type=F32)


def _split(x):
    hi = x.astype(BF16)
    lo = (x - hi.astype(F32)).astype(BF16)
    return hi, lo


def _sigmoid(x):
    return 1.0 / (1.0 + jnp.exp(-x))


class _Rows:
    def __init__(self, batch, seq, ctx_len):
        self.batch, self.seq, self.ctx = batch, seq, ctx_len
        assert seq % ROW_BLK == 0 and ctx_len % ROW_BLK == 0
        self.nlt = seq // ROW_BLK
        self.nct = ctx_len // ROW_BLK
        self.n_lat = batch * seq
        self.n_all = batch * (seq + ctx_len)
        self.n_groups = batch + 1

    def group(self, i, tm):
        lat_tiles = self.n_lat // tm
        return jnp.where(i < lat_tiles, 1 + i // (self.seq // tm), 0)

    def mod_row(self, layer, i, tm, which):
        return (layer * self.n_groups + self.group(i, tm)) * 6 + which

    def scan_block(self, b, s, reverse):
        if reverse:
            cblk = self.batch * self.nlt + b * self.nct + (self.nct - 1 - s)
            lblk = b * self.nlt + (self.nlt - 1 - (s - self.nct))
        else:
            cblk = self.batch * self.nlt + b * self.nct + s
            lblk = b * self.nlt + (s - self.nct)
        return jnp.where(s < self.nct, cblk, lblk)


def _mods_kernel(c_ref, w_ref, b_ref, o_ref):
    c = c_ref[...]
    s_hi, s_lo = _split(c * _sigmoid(c))
    w_hi, w_lo = _split(w_ref[...])
    acc = _dot(s_hi, w_hi) + _dot(s_lo, w_hi) + _dot(s_hi, w_lo)
    o_ref[...] = acc + b_ref[...]


def _mods(cvec, w_ada, b_ada):
    depth, d, n = w_ada.shape
    tn = _pick(n, (512, 256, 128))
    rows = cvec.shape[0]
    return pl.pallas_call(
        _mods_kernel,
        grid=(depth, n // tn),
        in_specs=[pl.BlockSpec((rows, d), lambda l, j: (0, 0)),
                  pl.BlockSpec((None, d, tn), lambda l, j: (l, 0, j)),
                  pl.BlockSpec((None, 1, tn), lambda l, j: (l, 0, j))],
        out_specs=pl.BlockSpec((None, rows, tn), lambda l, j: (l, 0, j)),
        out_shape=jax.ShapeDtypeStruct((depth, rows, n), F32),
        compiler_params=_cp(("arbitrary", "arbitrary")),
        name="adaln_mods",
    )(cvec, w_ada, b_ada.reshape(depth, 1, n))


def _split_rows(x, block, tile_axis):
    if not isinstance(x, tuple):
        return [pl.BlockSpec(block, lambda *g: (g[tile_axis], _other(g, tile_axis)))], [x], None
    n_first = x[0].shape[0] // block[0]
    first = pl.BlockSpec(block, lambda *g: (jnp.minimum(g[tile_axis], n_first - 1),
                                            _other(g, tile_axis)))
    second = pl.BlockSpec(block, lambda *g: (jnp.maximum(g[tile_axis] - n_first, 0),
                                             _other(g, tile_axis)))
    return [first, second], list(x), n_first


def _other(g, tile_axis):
    return g[1 - tile_axis] if len(g) == 2 else 0


def _row_tile(parts, i, n_first):
    if len(parts) == 1:
        return parts[0][...]
    return jnp.where(i < n_first, parts[0][...], parts[1][...])


def _norm_mod(x, g_ref, sc_ref, sh_ref):
    ms = jnp.mean(x * x, axis=-1, keepdims=True)
    y = x * lax.rsqrt(ms + EPS) * g_ref[...]
    return y * (1.0 + sc_ref[0]) + sh_ref[0]


def _norm_kernel(n_first, *refs):
    x_parts, (g_ref, sc_ref, sh_ref, o_ref) = refs[:-4], refs[-4:]
    x = _row_tile(x_parts, pl.program_id(0), n_first)
    o_ref[...] = _norm_mod(x, g_ref, sc_ref, sh_ref).astype(BF16)


def _pack_pair(lo, hi):
    lo_b = pltpu.bitcast(lo.astype(BF16).astype(F32), jnp.int32)
    hi_b = pltpu.bitcast(hi.astype(BF16).astype(F32), jnp.int32)
    return (hi_b & jnp.int32(-65536)) | lax.shift_right_logical(lo_b, 16)


def _unpack_pair(w):
    lo = pltpu.bitcast(lax.shift_left(w, 16), F32)
    hi = pltpu.bitcast(w & jnp.int32(-65536), F32)
    return lo, hi


def _store_slabs(y, o_ref):
    tm, d = y.shape
    ns = d // (2 * LANES)
    for s in range(ns):
        lo = y[:, s * LANES:(s + 1) * LANES]
        hi = y[:, d // 2 + s * LANES:d // 2 + (s + 1) * LANES]
        o_ref[pl.ds(s, tm, stride=ns), :] = _pack_pair(lo, hi)


def _load_slabs(src_ref, row0, tm, ns, emit):
    for s in range(ns):
        lo, hi = _unpack_pair(src_ref[pl.ds(row0 + s, tm, stride=ns), :])
        emit(s, lo, hi)


def _norm_router_kernel(n_exp, x_ref, g_ref, sc_ref, sh_ref, rwh_ref, rwl_ref, rb_ref,
                        o_ref, slab_ref, route_ref):
    y = _norm_mod(x_ref[...], g_ref, sc_ref, sh_ref)
    o_ref[...] = y.astype(BF16)
    _store_slabs(y, slab_ref)
    y_hi, y_lo = _split(y)
    rwh = rwh_ref[...]
    logits = _dot(y_hi, rwh) + _dot(y_lo, rwh) + _dot(y_hi, rwl_ref[...])
    scores = _sigmoid(logits)
    lane = lax.broadcasted_iota(jnp.int32, scores.shape, 1).astype(F32)
    work = jnp.where(lane < n_exp, scores + rb_ref[...], -jnp.inf)
    route = jnp.zeros_like(scores)
    total = jnp.zeros_like(scores[:, :1])
    for k in range(TOP_K):
        mx = jnp.max(work, axis=-1, keepdims=True)
        first = jnp.min(jnp.where(work == mx, lane, float(LANES)), axis=-1, keepdims=True)
        hit = lane == first
        sc = jnp.sum(jnp.where(hit, scores, 0.0), axis=-1, keepdims=True)
        total = total + sc
        route = jnp.where(lane == k, first, route)
        route = jnp.where(lane == TOP_K + k, sc, route)
        work = jnp.where(hit, -jnp.inf, work)
    is_w = (lane >= TOP_K) & (lane < 2 * TOP_K)
    route_ref[...] = jnp.where(is_w, route / total * ROUTED_SCALE, route)


def _norm(rows, x_all, n_all, gain, modsflat, layer, which_scale, which_shift, router=None):
    d = gain.shape[0]
    tm = ROW_BLK

    def mod_idx(which):
        return lambda i: (rows.mod_row(layer, i, tm, which), 0, 0)

    x_specs, x_args, n_first = _split_rows(x_all, (tm, d), 0)
    in_specs = x_specs + [pl.BlockSpec((1, d), lambda i: (0, 0)),
                          pl.BlockSpec((1, 1, d), mod_idx(which_scale)),
                          pl.BlockSpec((1, 1, d), mod_idx(which_shift))]
    args = x_args + [gain.reshape(1, d), modsflat, modsflat]
    out_spec = pl.BlockSpec((tm, d), lambda i: (i, 0))
    out_shape = jax.ShapeDtypeStruct((n_all, d), BF16)
    if router is None:
        return pl.pallas_call(
            functools.partial(_norm_kernel, n_first), grid=(n_all // tm,), in_specs=in_specs,
            out_specs=out_spec, out_shape=out_shape, compiler_params=_cp(("arbitrary",)),
            name="prenorm",
        )(*args)
    rw_hi, rw_lo, rbias, n_exp = router
    ns = d // (2 * LANES)
    in_specs += [pl.BlockSpec((d, LANES), lambda i: (0, 0)),
                 pl.BlockSpec((d, LANES), lambda i: (0, 0)),
                 pl.BlockSpec((1, LANES), lambda i: (0, 0))]
    return pl.pallas_call(
        functools.partial(_norm_router_kernel, n_exp),
        grid=(n_all // tm,), in_specs=in_specs,
        out_specs=[out_spec, pl.BlockSpec((tm * ns, LANES), lambda i: (i, 0)),
                   pl.BlockSpec((tm, LANES), lambda i: (i, 0))],
        out_shape=[out_shape, jax.ShapeDtypeStruct((n_all * ns, LANES), jnp.int32),
                   jax.ShapeDtypeStruct((n_all, LANES), F32)],
        compiler_params=_cp(("arbitrary",)), name="prenorm_router",
    )(*args, rw_hi, rw_lo, rbias)


def _mm_kernel(a_ref, w_ref, o_ref):
    o_ref[...] = _dot(a_ref[...], w_ref[...]).astype(o_ref.dtype)


def _matmul(a, w, out_dtype, m=None, tm=512):
    k = a.shape[1]
    m = a.shape[0] if m is None else m
    n = w.shape[1]
    tn = _pick(n, (1024, 512, 256, 128))
    return pl.pallas_call(
        _mm_kernel,
        grid=(n // tn, m // tm),
        in_specs=[pl.BlockSpec((tm, k), lambda j, i: (i, 0)),
                  pl.BlockSpec((k, tn), lambda j, i: (0, j))],
        out_specs=pl.BlockSpec((tm, tn), lambda j, i: (i, j)),
        out_shape=jax.ShapeDtypeStruct((m, n), out_dtype),
        compiler_params=_cp(("arbitrary", "arbitrary")),
        name="in_proj",
    )(a, w)


def _mm_wt_kernel(a_ref, wt_ref, o_ref, wb_ref):
    @pl.when(pl.program_id(1) == 0)
    def _():
        wb_ref[...] = wt_ref[0].T.astype(BF16)

    o_ref[...] = _dot(a_ref[...], wb_ref[...]).astype(o_ref.dtype)


def _matmul_wt(a, wt_stack, layer, row0, n, out_dtype, m=None, tm=512):
    k = a.shape[1]
    m = a.shape[0] if m is None else m
    tn = _pick(n, (1024, 512, 256, 128))
    return pl.pallas_call(
        _mm_wt_kernel,
        grid=(n // tn, m // tm),
        in_specs=[pl.BlockSpec((tm, k), lambda j, i: (i, 0)),
                  pl.BlockSpec((pl.Element(1), pl.Element(tn), pl.Element(k)),
                               lambda j, i: (layer, pl.multiple_of(row0 + j * tn, SUBLANES), 0))],
        out_specs=pl.BlockSpec((tm, tn), lambda j, i: (i, j)),
        out_shape=jax.ShapeDtypeStruct((m, n), out_dtype),
        scratch_shapes=[pltpu.VMEM((k, tn), BF16)],
        compiler_params=_cp(("arbitrary", "arbitrary")),
        name="in_proj",
    )(a, wt_stack)


def _hgrn_exact_chunk(q, k, bcum, v, st, reverse):
    c = HG_CHUNK
    last = 0 if reverse else c - 1
    b_last = bcum[last:last + 1]
    o_inter = _dot_nt((q * jnp.exp(bcum)).astype(BF16), st.astype(BF16))
    kd = k * jnp.exp(b_last - bcum)
    st_new = st * jnp.exp(b_last) + _dot_tn(v.astype(BF16), kd.astype(BF16))

    ones = jnp.ones((LANES, LANES), BF16)
    sub_row = lax.broadcasted_iota(jnp.int32, (HG_SUB, LANES), 0)
    v16 = v.astype(BF16)
    outs = []
    for i in range(c // HG_SUB):
        r0 = i * HG_SUB
        bsub = bcum[r0:r0 + HG_SUB]
        qsub = q[r0:r0 + HG_SUB]
        zs = []
        for s in range(HG_SUB):
            keep = (sub_row <= s) if reverse else (sub_row >= s)
            dl = jnp.where(keep, bsub - bcum[r0 + s:r0 + s + 1], NEG_BIG)
            zs.append(qsub * (k[r0 + s:r0 + s + 1] * jnp.exp(dl)))
        red = _dot(jnp.concatenate(zs, axis=0).astype(BF16), ones)
        o_sub = red[0:HG_SUB] * v[r0:r0 + 1]
        for s in range(1, HG_SUB):
            o_sub = o_sub + red[s * HG_SUB:(s + 1) * HG_SUB] * v[r0 + s:r0 + s + 1]
        lo, hi = (r0 + HG_SUB, c) if reverse else (0, r0)
        if hi > lo:
            ref_row = bcum[lo:lo + 1] if reverse else bcum[hi - 1:hi]
            qi = (qsub * jnp.exp(bsub - ref_row)).astype(BF16)
            kt = (k[lo:hi] * jnp.exp(ref_row - bcum[lo:hi])).astype(BF16)
            o_sub = o_sub + _dot(_dot_nt(qi, kt).astype(BF16), v16[lo:hi])
        outs.append(o_sub)
    return o_inter + jnp.concatenate(outs, axis=0), st_new


def _hgrn_fast_blocks(dirs):
    c, sub = HG_CHUNK, HG_FAST_SUB
    n_chunk = ROW_BLK // c
    units = [(d, n_chunk - 1 - step if d['reverse'] else step)
             for step in range(n_chunk) for d in dirs]
    qs, xs, eb, scores, vals, outs = {}, {}, {}, {}, {}, {}
    for d, ci in units:
        reverse, u = d['reverse'], (id(d), ci)
        rows = pl.ds(ci * c, c)
        q, k, bcum, v = d['q_s'][rows, :], d['k_s'][rows, :], d['b_s'][rows, :], d['v_s'][rows, :]
        v16 = v.astype(BF16)
        last = 0 if reverse else c - 1
        b_last = bcum[last:last + 1]
        qs[u] = (q * jnp.exp(bcum)).astype(BF16)
        eb[u] = jnp.exp(b_last)
        xs[u] = _dot_tn(v16, (k * jnp.exp(b_last - bcum)).astype(BF16))
        for i in range(c // sub):
            r0 = i * sub
            lo, hi = (r0, c) if reverse else (0, r0 + sub)
            ref_row = bcum[r0 + sub - 1:r0 + sub] if reverse else bcum[r0:r0 + 1]
            qi = (q[r0:r0 + sub] * jnp.exp(bcum[r0:r0 + sub] - ref_row)).astype(BF16)
            kt = (k[lo:hi] * jnp.exp(ref_row - bcum[lo:hi])).astype(BF16)
            scores[u, i] = _dot_nt(qi, kt)
            vals[u, i] = v16[lo:hi]
    for d, ci in units:
        reverse, u = d['reverse'], (id(d), ci)
        parts = []
        for i in range(c // sub):
            r0 = i * sub
            lo, hi = (r0, c) if reverse else (0, r0 + sub)
            key = lax.broadcasted_iota(jnp.int32, (sub, hi - lo), 1) + lo
            qry = lax.broadcasted_iota(jnp.int32, (sub, hi - lo), 0) + r0
            a = jnp.where((key >= qry) if reverse else (key <= qry), scores[u, i], 0.0)
            parts.append(_dot(a.astype(BF16), vals[u, i]))
        outs[u] = jnp.concatenate(parts, axis=0)
    st = {id(d): d['st'][...] for d in dirs}
    for d, ci in units:
        u = (id(d), ci)
        o = outs[u] + _dot_nt(qs[u], st[id(d)].astype(BF16))
        d['o'][pl.ds(ci * c, c), d['cols']] = o.astype(d['o'].dtype)
        st[id(d)] = st[id(d)] * eb[u] + xs[u]
    for d in dirs:
        d['st'][...] = st[id(d)]


def _hgrn_kernel(layer, *refs):
    n_in, n_scr = 4, 5
    outs = refs[2 * n_in:2 * n_in + 2]
    scr = refs[2 * n_in + 2:]
    dirs = []
    for hh in range(SCAN_HEADS):
        cols = slice(hh * HEAD_W, (hh + 1) * HEAD_W)
        for di, reverse in enumerate((False, True)):
            q_ref, f_ref, v_ref, lb_ref = refs[di * n_in:(di + 1) * n_in]
            st, q_s, k_s, b_s, v_s = (r.at[hh] for r in scr[di * n_scr:(di + 1) * n_scr])
            dirs.append(dict(reverse=reverse, cols=cols, q=q_ref, f=f_ref, v=v_ref, lb=lb_ref,
                             o=outs[di], st=st, q_s=q_s, k_s=k_s, b_s=b_s, v_s=v_s))

    @pl.when(pl.program_id(2) == 0)
    def _():
        for d in dirs:
            d['st'][...] = jnp.zeros_like(d['st'])

    r_i = lax.broadcasted_iota(jnp.int32, (ROW_BLK, ROW_BLK), 0)
    c_i = lax.broadcasted_iota(jnp.int32, (ROW_BLK, ROW_BLK), 1)
    same_chunk = r_i // HG_CHUNK == c_i // HG_CHUNK
    tris = {rev: jnp.where(same_chunk & ((c_i >= r_i) if rev else (c_i <= r_i)), 1.0, 0.0
                           ).astype(BF16) for rev in (False, True)}
    spread = jnp.zeros((1, LANES), F32)
    for d in dirs:
        cols = d['cols']
        lg = d['lb'][0][:, cols]
        e = jnp.exp(lg - jnp.max(lg, axis=0, keepdims=True))
        p = e / jnp.sum(e, axis=0, keepdims=True)
        lb = jnp.zeros((1, LANES), F32)
        for i in range(1, layer + 1):
            lb = lb + p[i:i + 1]

        qp = d['q'][:, cols].astype(F32)
        fp = d['f'][:, cols].astype(F32)
        d['q_s'][...] = qp * _sigmoid(qp) * HEAD_W ** -0.5
        d['k_s'][...] = (1.0 - lb) * _sigmoid(-fp)
        d['v_s'][...] = d['v'][:, cols].astype(F32)
        lf = jnp.log(jnp.maximum(lb + (1.0 - lb) * _sigmoid(fp), TINY))
        tri = tris[d['reverse']]
        lf_hi, lf_lo = _split(lf)
        bcum = _dot(tri, lf_hi) + _dot(tri, lf_lo)
        d['b_s'][...] = bcum
        for r0 in range(0, ROW_BLK, HG_FAST_SUB):
            spread = jnp.maximum(
                spread, jnp.abs(bcum[r0:r0 + 1] - bcum[r0 + HG_FAST_SUB - 1:r0 + HG_FAST_SUB]))
    small = jnp.max(spread) < HG_SPREAD_MAX

    @pl.when(small)
    def _():
        _hgrn_fast_blocks(dirs)

    @pl.when(jnp.logical_not(small))
    def _():
        n_chunk = ROW_BLK // HG_CHUNK
        for d in dirs:
            st = d['st'][...]
            for ci in (range(n_chunk - 1, -1, -1) if d['reverse'] else range(n_chunk)):
                rows = pl.ds(ci * HG_CHUNK, HG_CHUNK)
                o, st = _hgrn_exact_chunk(d['q_s'][rows, :], d['k_s'][rows, :], d['b_s'][rows, :],
                                          d['v_s'][rows, :], st, d['reverse'])
                d['o'][rows, d['cols']] = o.astype(d['o'].dtype)
            d['st'][...] = st


def _hgrn_scan(rows, p_main, lb_logits, layer):
    n_all = p_main.shape[0]
    steps = rows.nct + rows.nlt
    depth = lb_logits.shape[1]

    def dir_specs(direction):
        reverse = direction == 1
        blk = lambda b, s: rows.scan_block(b, s, reverse)
        spec = lambda seg: pl.BlockSpec((ROW_BLK, width),
                                        lambda b, g, s: (blk(b, s), seg * groups + g))
        ins = [spec(S_HG_Q), spec(S_HG_FB if reverse else S_HG_FF), spec(S_HG_I),
               pl.BlockSpec((1, depth, width), lambda b, g, s: (direction, 0, g))]
        return ins, pl.BlockSpec((ROW_BLK, width), lambda b, g, s: (blk(b, s), g))

    groups, width = HEADS // SCAN_HEADS, SCAN_HEADS * HEAD_W
    ins_f, out_f = dir_specs(0)
    ins_b, out_b = dir_specs(1)
    out = jax.ShapeDtypeStruct((n_all, SEG), BF16)
    args = (p_main, p_main, p_main, lb_logits)
    return pl.pallas_call(
        functools.partial(_hgrn_kernel, layer),
        grid=(rows.batch, groups, steps),
        in_specs=ins_f + ins_b,
        out_specs=[out_f, out_b],
        out_shape=[out, out],
        scratch_shapes=([pltpu.VMEM((SCAN_HEADS, HEAD_W, HEAD_W), F32)]
                        + [pltpu.VMEM((SCAN_HEADS, ROW_BLK, HEAD_W), F32)] * 4) * 2,
        compiler_params=_cp(("arbitrary", "arbitrary", "arbitrary")),
        name="hgrn2_scan",
    )(*args, *args)


def _mlstm_kernel(*refs):
    c = ROW_BLK
    n_in = 7
    chains = []
    for hh in range(SCAN_HEADS):
        cols = slice(hh * HEAD_W, (hh + 1) * HEAD_W)
        for di, reverse in enumerate((False, True)):
            chains.append(dict(reverse=reverse, hh=hh, cols=cols, ins=refs[di * n_in:(di + 1) * n_in],
                               o=refs[2 * n_in + di], s=refs[2 * n_in + 2 + 2 * di].at[hh],
                               m=refs[2 * n_in + 3 + 2 * di].at[hh]))
    dirs = chains

    @pl.when(pl.program_id(2) == 0)
    def _():
        for d in dirs:
            d['s'][...] = jnp.zeros_like(d['s'])
            d['m'][...] = jnp.zeros_like(d['m'])

    r_i = lax.broadcasted_iota(jnp.int32, (c, c), 0)
    c_i = lax.broadcasted_iota(jnp.int32, (c, c), 1)
    for d in dirs:
        q_ref, k_ref, v_ref, ig_ref, fg_ref, ib_ref, fb_ref = d['ins']
        hh, cols = d['hh'], d['cols']
        d['q'] = q_ref[:, cols]
        d['kt'] = (k_ref[:, cols].astype(F32) * HEAD_W ** -0.5).T
        d['v_cat'] = jnp.concatenate([v_ref[:, cols], jnp.ones((c, HEAD_W), BF16)], axis=1)
        d['ig'] = ig_ref[hh] + ib_ref[hh][:, :1]
        fx = fg_ref[hh] + fb_ref[hh][:, :1]
        lf = jnp.minimum(fx, 0.0) - jnp.log1p(jnp.exp(-jnp.abs(fx)))
        cum = jnp.where((r_i >= c_i) if d['reverse'] else (r_i <= c_i), 1.0, 0.0).astype(BF16)
        lf_hi, lf_lo = _split(jnp.broadcast_to(lf, (8, c)))
        d['brow'] = (_dot(lf_hi, cum) + _dot(lf_lo, cum))[0:1]
    for d in dirs:
        d['qk'] = _dot(d['q'], d['kt'].astype(BF16))
        d['state'] = d['s'][...]
        d['q_state'] = _dot(d['q'], d['state'].astype(BF16))
    for d in dirs:
        b_rows = jnp.broadcast_to(d['brow'], (c, c))
        b_cols = b_rows.T
        keep = (c_i >= r_i) if d['reverse'] else (c_i <= r_i)
        dlog = jnp.where(keep, b_cols - b_rows + d['ig'], NEG_BIG)
        d['m_prev'] = d['m'][:, :1]
        inter = b_cols[:, :1] + d['m_prev']
        d['m_t'] = jnp.maximum(jnp.max(dlog, axis=-1, keepdims=True), inter)
        d['w'] = (jnp.exp(dlog - d['m_t']) * d['qk']).astype(BF16)
        d['w_inter'] = jnp.exp(inter - d['m_t'])
    for d in dirs:
        nd = _dot(d['w'], d['v_cat']) + d['w_inter'] * d['q_state']
        num, den = nd[:, :HEAD_W], nd[:, HEAD_W:]
        d['o'][:, d['cols']] = (num / jnp.maximum(jnp.abs(den), jnp.exp(-d['m_t']))
                                ).astype(d['o'].dtype)
    for d in dirs:
        last = 0 if d['reverse'] else c - 1
        m_new = d['m_t'][last:last + 1]
        b_last = d['brow'][:, last:last + 1]
        wk = jnp.exp(b_last - d['brow'] + d['ig'] - m_new)
        dec = jnp.exp(b_last + d['m_prev'] - m_new)
        d['s'][...] = dec * d['state'] + _dot((d['kt'] * wk).astype(BF16), d['v_cat'])
        d['m'][...] = jnp.broadcast_to(m_new, d['m'].shape)


def _mlstm_scan(rows, p_main, gates_t, i_bias, f_bias):
    n_all = p_main.shape[0]
    steps = rows.nct + rows.nlt

    def dir_specs(direction):
        reverse = direction == 1
        blk = lambda b, s: rows.scan_block(b, s, reverse)
        spec = lambda seg: pl.BlockSpec((ROW_BLK, width),
                                        lambda b, g, s: (blk(b, s), seg * groups + g))
        gate = lambda which: pl.BlockSpec((SCAN_HEADS, 1, ROW_BLK),
                                          lambda b, g, s: (which * groups + g, 0, blk(b, s)))
        bias = pl.BlockSpec((SCAN_HEADS, 1, LANES), lambda b, g, s: (direction * groups + g, 0, 0))
        ins = [spec(S_ML_Q), spec(S_ML_K), spec(S_ML_V), gate(direction), gate(2 + direction),
               bias, bias]
        return ins, pl.BlockSpec((ROW_BLK, width), lambda b, g, s: (blk(b, s), g))

    groups, width = HEADS // SCAN_HEADS, SCAN_HEADS * HEAD_W
    ins_f, out_f = dir_specs(0)
    ins_b, out_b = dir_specs(1)
    out = jax.ShapeDtypeStruct((n_all, SEG), BF16)
    args = (p_main, p_main, p_main, gates_t, gates_t, i_bias, f_bias)
    return pl.pallas_call(
        _mlstm_kernel,
        grid=(rows.batch, groups, steps),
        in_specs=ins_f + ins_b,
        out_specs=[out_f, out_b],
        out_shape=[out, out],
        scratch_shapes=[pltpu.VMEM((SCAN_HEADS, HEAD_W, 2 * HEAD_W), F32),
                        pltpu.VMEM((SCAN_HEADS, 1, LANES), F32)] * 2,
        compiler_params=_cp(("arbitrary", "arbitrary", "arbitrary")),
        name="mlstm_scan",
    )(*args, *args)


def _da_prep_kernel(q_ref, k_ref, v_ref, cos_ref, sin_ref, qg_ref, kg_ref, qt_ref, ko_ref, vt_ref):
    r_i = lax.broadcasted_iota(jnp.int32, (LANES, LANES), 0)
    c_i = lax.broadcasted_iota(jnp.int32, (LANES, LANES), 1)
    blockdiag = jnp.where((r_i // DA_QK) == (c_i // DA_QK), 1.0 / DA_QK, 0.0).astype(BF16)
    cos, sin = cos_ref[...], sin_ref[...]
    lane = lax.broadcasted_iota(jnp.int32, cos.shape, 1)
    first_half = (lane % DA_QK) < (DA_QK // 2)

    def qk_norm_rope(x, gain):
        x_hi, x_lo = _split(x * x)
        ms = _dot(x_hi, blockdiag) + _dot(x_lo, blockdiag)
        y = x * lax.rsqrt(ms + EPS) * gain
        rot = jnp.where(first_half, -pltpu.roll(y, LANES - DA_QK // 2, 1),
                        pltpu.roll(y, DA_QK // 2, 1))
        return y * cos + rot * sin

    for h in range(HEADS):
        cols = slice(h * HEAD_W, (h + 1) * HEAD_W)
        q = qk_norm_rope(q_ref[:, cols].astype(F32), qg_ref[...]) * (DA_QK ** -0.5 * LOG2E)
        qt_ref[cols, :] = q.T.astype(BF16)
        ko_ref[:, cols] = qk_norm_rope(k_ref[:, cols].astype(F32), kg_ref[...]).astype(BF16)
        vt_ref[cols, :] = v_ref[:, cols].astype(F32).T.astype(BF16)


def _da_prep(rows, p_main, cos_tab, sin_tab, q_gain, k_gain):
    n_all = p_main.shape[0]
    tm = ROW_BLK
    out_t = jax.ShapeDtypeStruct((SEG, n_all), BF16)

    def tab_idx(i):
        lat = i % rows.nlt
        ctx = rows.nlt + (i - rows.batch * rows.nlt) % rows.nct
        return (jnp.where(i < rows.batch * rows.nlt, lat, ctx), 0)

    out = jax.ShapeDtypeStruct((n_all, SEG), BF16)
    return pl.pallas_call(
        _da_prep_kernel,
        grid=(n_all // tm,),
        in_specs=[pl.BlockSpec((tm, SEG), lambda i: (i, S_DA_Q)),
                  pl.BlockSpec((tm, SEG), lambda i: (i, S_DA_K)),
                  pl.BlockSpec((tm, SEG), lambda i: (i, S_DA_V)),
                  pl.BlockSpec((tm, LANES), tab_idx),
                  pl.BlockSpec((tm, LANES), tab_idx),
                  pl.BlockSpec((1, LANES), lambda i: (0, 0)),
                  pl.BlockSpec((1, LANES), lambda i: (0, 0))],
        out_specs=[pl.BlockSpec((SEG, tm), lambda i: (0, i)),
                   pl.BlockSpec((tm, SEG), lambda i: (i, 0)),
                   pl.BlockSpec((SEG, tm), lambda i: (0, i))],
        out_shape=[out_t, out, out_t],
        compiler_params=_cp(("arbitrary",)),
        name="da_qk_prep",
    )(p_main, p_main, p_main, cos_tab, sin_tab, q_gain, k_gain)


def _attn_kernel(lam_init, n_lat_q_blocks, n_lat_keys, tk, qt_ref, kc_ref, vct_ref, kl_ref, vlt_ref,
                 lam_ref, sn_ref, o_ref, acc_ref, m_ref, sa_ref, sb_ref, kn_ref):
    tq = qt_ref.shape[1]
    qt = qt_ref[...]
    chan = lax.broadcasted_iota(jnp.int32, qt.shape, 0)
    zero = jnp.zeros_like(qt)
    q2t = jnp.concatenate([jnp.where(chan < DA_QK, qt, zero), jnp.where(chan >= DA_QK, qt, zero)],
                          axis=1)

    def weighted_values(vt, st, m):
        p = jnp.exp2((st - m).astype(BF16))
        ones = jnp.ones((ATTN_PAD_ROWS, vt.shape[1]), BF16)
        return _dot(jnp.concatenate([vt, ones], axis=0), p)

    st = _dot(kc_ref[...], q2t)
    m0 = jnp.max(st, axis=0, keepdims=True)
    m_ref[...] = m0
    acc_ref[...] = weighted_values(vct_ref[...], st, m0)

    n_chunks = n_lat_keys // tk
    unroll = _pick(n_chunks, (4, 2, 1))
    bufs = (sa_ref, sb_ref) if unroll > 1 else (sa_ref, sa_ref)

    def chunk(c):
        return pl.ds(pl.multiple_of(c * tk, tk), tk)

    def scores(c, buf):
        st = _dot(kl_ref[chunk(c), :], q2t)
        buf[...] = st
        return jnp.max(st, axis=0, keepdims=True)

    def absorb(c, buf, m_cur):
        m_old = m_ref[...]
        m_new = jnp.maximum(m_old, m_cur)
        pv = weighted_values(vlt_ref[:, chunk(c)], buf[...], m_new)
        acc_ref[...] = jnp.exp2(m_old - m_new) * acc_ref[...] + pv
        m_ref[...] = m_new

    def body(j, m_pend):
        for u in range(unroll):
            c = j * unroll + u
            if unroll > 1:
                m_next = scores(jnp.minimum(c + 1, n_chunks - 1), bufs[(u + 1) % 2])
                absorb(c, bufs[u % 2], m_pend)
            else:
                absorb(c, bufs[0], m_pend)
                m_next = scores(jnp.minimum(c + 1, n_chunks - 1), bufs[0])
            m_pend = m_next
        return m_pend

    n_trips = jnp.where(pl.program_id(2) < n_lat_q_blocks, n_chunks // unroll, 0)

    @pl.when(pl.program_id(2) == 0)
    def _():
        r_i = lax.broadcasted_iota(jnp.int32, (LANES, LANES), 0)
        c_i = lax.broadcasted_iota(jnp.int32, (LANES, LANES), 1)
        same_map = jnp.where((r_i // DA_QK) == (c_i // DA_QK), 1.0, 0.0).astype(BF16)

        def max_sq_norm(k):
            kf = k.astype(F32)
            hi, lo = _split(kf * kf)
            return jnp.max(_dot(hi, same_map) + _dot(lo, same_map), axis=0, keepdims=True)

        def body_norm(c, best):
            return jnp.maximum(best, max_sq_norm(kl_ref[chunk(c), :]))
        kn_ref[...] = lax.fori_loop(0, n_chunks, body_norm, max_sq_norm(kc_ref[...]))

    qf = q2t.astype(F32)
    q_sq = jnp.sum(qf * qf, axis=0, keepdims=True)
    col = lax.broadcasted_iota(jnp.int32, q_sq.shape, 1)
    k_sq = jnp.where(col < tq, kn_ref[:, 0:1], kn_ref[:, DA_QK:DA_QK + 1])
    bound = jnp.sqrt(q_sq * k_sq) * 1.01 + 0.01
    bounded = jnp.max(bound - m0) <= ATTN_RANGE_MAX

    def body_bounded(j, carry):
        for u in range(unroll):
            c = j * unroll + u
            st = _dot(kl_ref[chunk(c), :], q2t)
            acc_ref[...] += weighted_values(vlt_ref[:, chunk(c)], st, m0)
        return carry

    @pl.when(bounded)
    def _():
        lax.fori_loop(0, n_trips, body_bounded, 0)

    @pl.when(jnp.logical_not(bounded))
    def _():
        lax.fori_loop(0, n_trips, body, scores(0, bufs[0]))

    lv = lam_ref[...]
    lam = (jnp.exp(jnp.sum(lv[0:1] * lv[1:2], axis=-1, keepdims=True))
           - jnp.exp(jnp.sum(lv[2:3] * lv[3:4], axis=-1, keepdims=True)) + lam_init)
    o = acc_ref[0:HEAD_W, :] / acc_ref[HEAD_W:HEAD_W + 1, :]
    a = o[:, :tq] - lam * o[:, tq:]
    ms = jnp.mean(a * a, axis=0, keepdims=True)
    y = a * lax.rsqrt(ms + EPS) * sn_ref[...] * (1.0 - lam_init)
    o_ref[...] = y.T.astype(o_ref.dtype)


def _attention_call(rows, qt, kh, vt, lam_vec, sub_gain, lam_init, n_rows, prev):
    assert rows.nct == 1
    ctx_blk = lambda b: rows.batch * rows.nlt + b
    tk = _pick(rows.seq, (512, 256))
    if prev is None:
        tq = _pick(rows.seq, (512, 256))
        n_q = rows.seq // tq
        q_blk = lambda b, i: b * n_q + i
        kern = functools.partial(_attn_kernel, lam_init, n_q, rows.seq, tk)
        extra_specs, extra_args, aliases = [], [], {}
    else:
        tq, n_q = ROW_BLK, 0
        q_blk = lambda b, i: ctx_blk(b)
        attn = functools.partial(_attn_kernel, lam_init, n_q, rows.seq, tk)
        kern = lambda *refs: attn(*refs[:7], *refs[8:])
        extra_specs, extra_args, aliases = [pl.BlockSpec(memory_space=pl.ANY)], [prev], {7: 0}
    return pl.pallas_call(
        kern,
        grid=(rows.batch, HEADS, max(n_q, 1)),
        in_specs=[pl.BlockSpec((HEAD_W, tq), lambda b, h, i: (h, q_blk(b, i))),
                  pl.BlockSpec((ROW_BLK, HEAD_W), lambda b, h, i: (ctx_blk(b), h)),
                  pl.BlockSpec((HEAD_W, ROW_BLK), lambda b, h, i: (h, ctx_blk(b))),
                  pl.BlockSpec((rows.seq, HEAD_W), lambda b, h, i: (b, h)),
                  pl.BlockSpec((HEAD_W, rows.seq), lambda b, h, i: (h, b)),
                  pl.BlockSpec(lam_vec.shape, lambda b, h, i: (0, 0)),
                  pl.BlockSpec((HEAD_W, 1), lambda b, h, i: (0, 0))] + extra_specs,
        out_specs=pl.BlockSpec((tq, HEAD_W), lambda b, h, i: (q_blk(b, i), h)),
        out_shape=jax.ShapeDtypeStruct((n_rows, SEG), BF16),
        scratch_shapes=[pltpu.VMEM((HEAD_W + ATTN_PAD_ROWS, 2 * tq), F32),
                        pltpu.VMEM((1, 2 * tq), F32), pltpu.VMEM((tk, 2 * tq), F32),
                        pltpu.VMEM((tk, 2 * tq), F32), pltpu.VMEM((1, LANES), F32)],
        input_output_aliases=aliases,
        compiler_params=_cp(("arbitrary", "arbitrary", "arbitrary")),
        name="diff_attention" if prev is None else "diff_attention_ctx",
    )(qt, kh, vt, kh, vt, lam_vec, sub_gain, *extra_args)


def _attention(rows, qt, kh, vt, lam_vec, sub_gain, lam_init, n_rows):
    da = _attention_call(rows, qt, kh, vt, lam_vec, sub_gain, lam_init, n_rows, None)
    if n_rows > rows.n_lat:
        da = _attention_call(rows, qt, kh, vt, lam_vec, sub_gain, lam_init, n_rows, da)
    return da


def _merge_kernel(ohf_ref, ohb_ref, hg_ref, da_ref, omf_ref, omb_ref, mo_ref, g0_ref, g1_ref,
                  g2_ref, hn_ref, mn_ref, wb_ref, o_ref, h_ref):
    @pl.when(pl.program_id(1) == 0)
    def _():
        for h in range(HEADS):
            cols = slice(h * HEAD_W, (h + 1) * HEAD_W)
            o = ohf_ref[:, cols].astype(F32) + ohb_ref[:, cols].astype(F32)
            y = o * lax.rsqrt(jnp.mean(o * o, axis=-1, keepdims=True) + EPS) * hn_ref[...]
            g = hg_ref[:, cols].astype(F32)
            h_ref[0, :, cols] = (y * (g * _sigmoid(g))).astype(BF16)
            o = omf_ref[:, cols].astype(F32) + omb_ref[:, cols].astype(F32)
            y = o * lax.rsqrt(jnp.mean(o * o, axis=-1, keepdims=True) + EPS) * mn_ref[...]
            h_ref[2, :, cols] = (y * _sigmoid(mo_ref[:, cols].astype(F32))).astype(BF16)
        h_ref[1] = da_ref[...]

    y = _sigmoid(g0_ref[...].astype(F32)) * _dot(h_ref[0], wb_ref[0])
    y = y + _sigmoid(g1_ref[...].astype(F32)) * _dot(h_ref[1], wb_ref[1])
    y = y + _sigmoid(g2_ref[...].astype(F32)) * _dot(h_ref[2], wb_ref[2])
    o_ref[...] = y.astype(o_ref.dtype)


def _merge(p_main, p_merge, ohf, ohb, da, omf, omb, hg_gain, ml_gain, w_branch, d):
    n_all = da.shape[0]
    tm = 512
    tn = _pick(d, (512, 256, 128))
    row = lambda i, j: (i, 0)

    def gate_spec(jj):
        return pl.BlockSpec((tm, tn), lambda i, j: (i, jj * (d // tn) + j))

    return pl.pallas_call(
        _merge_kernel,
        grid=(n_all // tm, d // tn),
        in_specs=[pl.BlockSpec((tm, SEG), row), pl.BlockSpec((tm, SEG), row),
                  pl.BlockSpec((tm, SEG), lambda i, j: (i, S_HG_G)),
                  pl.BlockSpec((tm, SEG), row),
                  pl.BlockSpec((tm, SEG), row), pl.BlockSpec((tm, SEG), row),
                  pl.BlockSpec((tm, SEG), lambda i, j: (i, S_ML_O)),
                  gate_spec(0), gate_spec(1), gate_spec(2),
                  pl.BlockSpec((1, LANES), lambda i, j: (0, 0)),
                  pl.BlockSpec((1, LANES), lambda i, j: (0, 0)),
                  pl.BlockSpec((3, SEG, tn), lambda i, j: (0, 0, j))],
        out_specs=pl.BlockSpec((tm, tn), lambda i, j: (i, j)),
        out_shape=jax.ShapeDtypeStruct((n_all, d), BF16),
        scratch_shapes=[pltpu.VMEM((3, tm, SEG), BF16)],
        compiler_params=_cp(("arbitrary", "arbitrary")),
        name="branch_merge",
    )(ohf, ohb, p_main, da, omf, omb, p_main, p_merge, p_merge, p_merge, hg_gain, ml_gain,
      w_branch)


def _proj_resid_kernel(n_first, a_ref, w_ref, g_ref, *refs):
    x_parts, o_ref = refs[:-1], refs[-1]
    x = _row_tile(x_parts, pl.program_id(1), n_first)
    o_ref[...] = x + g_ref[0] * _dot(a_ref[...], w_ref[...])


def _proj_resid(rows, a, w, x_all, modsflat, layer, which_gate):
    n_all, k = a.shape
    d = w.shape[1]
    tm = 512
    tn = _pick(d, (1024, 512, 256, 128))
    x_specs, x_args, n_first = _split_rows(x_all, (tm, tn), 1)
    return pl.pallas_call(
        functools.partial(_proj_resid_kernel, n_first),
        grid=(d // tn, n_all // tm),
        in_specs=[pl.BlockSpec((tm, k), lambda j, i: (i, 0)),
                  pl.BlockSpec((k, tn), lambda j, i: (0, j)),
                  pl.BlockSpec((1, 1, tn),
                               lambda j, i: (rows.mod_row(layer, i, tm, which_gate), 0, j))]
        + x_specs,
        out_specs=pl.BlockSpec((tm, tn), lambda j, i: (i, j)),
        out_shape=jax.ShapeDtypeStruct((n_all, d), F32),
        compiler_params=_cp(("arbitrary", "arbitrary")),
        name="out_proj_residual",
    )(a, w, modsflat, *x_args)


def _swiglu_hidden(x, w1_ref, w3_ref):
    h1 = _dot(x, w1_ref[...])
    h3 = _dot(x, w3_ref[...])
    return (h1 * _sigmoid(h1)) * h3


def _shared_expert_kernel(tok_ref, w1_ref, w3_ref, w2_ref, o_ref):
    a = _swiglu_hidden(tok_ref[...], w1_ref, w3_ref)
    o_ref[...] = _dot(a.astype(BF16), w2_ref[...]).astype(o_ref.dtype)


def _shared_expert(tok, w1, w3, w2):
    n_all, d = tok.shape
    de = w1.shape[1]
    tm = 512
    full = lambda i: (0, 0)
    return pl.pallas_call(
        _shared_expert_kernel,
        grid=(n_all // tm,),
        in_specs=[pl.BlockSpec((tm, d), lambda i: (i, 0)),
                  pl.BlockSpec((d, de), full), pl.BlockSpec((d, de), full),
                  pl.BlockSpec((de, d), full)],
        out_specs=pl.BlockSpec((tm, d), lambda i: (i, 0)),
        out_shape=jax.ShapeDtypeStruct((n_all, d), BF16),
        compiler_params=_cp(("arbitrary",)),
        name="moe_shared_expert",
    )(tok, w1, w3, w2)


def _moe_plan_kernel(route_ref, pos_ref, cnt_ref, carry_ref):
    @pl.when(pl.program_id(0) == 0)
    def _():
        carry_ref[...] = jnp.zeros_like(carry_ref)

    r = route_ref[...]
    tm = r.shape[0]
    lane = lax.broadcasted_iota(jnp.int32, r.shape, 1).astype(F32)
    mask = jnp.zeros_like(r)
    for k in range(TOP_K):
        mask = mask + jnp.where(lane == r[:, k:k + 1], 1.0, 0.0)
    r_i = lax.broadcasted_iota(jnp.int32, (tm, tm), 0)
    c_i = lax.broadcasted_iota(jnp.int32, (tm, tm), 1)
    before = jnp.where(c_i < r_i, 1.0, 0.0).astype(BF16)
    rank = _dot(before, mask.astype(BF16)) + carry_ref[...]
    out = jnp.zeros_like(r)
    for k in range(TOP_K):
        pk = jnp.sum(jnp.where(lane == r[:, k:k + 1], rank, 0.0), axis=-1, keepdims=True)
        out = jnp.where(lane == k, pk, out)
    pos_ref[...] = out
    carry = carry_ref[...] + jnp.sum(mask, axis=0, keepdims=True)
    carry_ref[...] = carry
    cnt_ref[...] = jnp.broadcast_to(carry, cnt_ref.shape)


def _moe_plan(route):
    n = route.shape[0]
    tm = ROW_BLK
    return pl.pallas_call(
        _moe_plan_kernel,
        grid=(n // tm,),
        in_specs=[pl.BlockSpec((tm, LANES), lambda i: (i, 0))],
        out_specs=[pl.BlockSpec((tm, LANES), lambda i: (i, 0)),
                   pl.BlockSpec((8, LANES), lambda i: (0, 0))],
        out_shape=[jax.ShapeDtypeStruct((n, LANES), F32), jax.ShapeDtypeStruct((8, LANES), F32)],
        scratch_shapes=[pltpu.VMEM((1, LANES), F32)],
        compiler_params=_cp(("arbitrary",)),
        name="moe_plan",
    )(route)


def _moe_routed_kernel(ns, te_ref, na_ref, st_ref, tok_hbm, sw_ref, w1_ref, w3_ref, w2_ref, o_ref,
                       xbuf, x_ref, sem):
    tm = MOE_TILE
    t = pl.program_id(0)
    n_active = na_ref[0]
    par = t % 2

    def gather_copy(tile, buf, r):
        tok = st_ref[tile * tm + r]
        return pltpu.make_async_copy(
            tok_hbm.at[pl.ds(pl.multiple_of(tok * ns, ns), ns), :],
            xbuf.at[buf, pl.ds(pl.multiple_of(r * ns, ns), ns), :], sem.at[buf])

    def for_each_copy(tile, buf, act):
        def body(r4, carry):
            for u in range(4):
                act(gather_copy(tile, buf, r4 * 4 + u), u)
            return carry
        lax.fori_loop(0, tm // 4, body, 0)

    start = lambda cp, u: cp.start(priority=u % 2)

    @pl.when(t == 0)
    def _():
        for_each_copy(0, 0, start)

    @pl.when(t + 1 < n_active)
    def _():
        for_each_copy(t + 1, 1 - par, start)

    @pl.when(t < n_active)
    def _():
        for_each_copy(t, par, lambda cp, u: cp.wait())
        d = x_ref.shape[1]

        def emit(s, lo, hi):
            x_ref[:, s * LANES:(s + 1) * LANES] = lo.astype(BF16)
            x_ref[:, d // 2 + s * LANES:d // 2 + (s + 1) * LANES] = hi.astype(BF16)
        _load_slabs(xbuf.at[par], 0, tm, ns, emit)
        a = _swiglu_hidden(x_ref[...], w1_ref, w3_ref) * sw_ref[...]
        _store_slabs(_dot(a.astype(BF16), w2_ref[...]), o_ref)

    @pl.when(t >= n_active)
    def _():
        o_ref[...] = jnp.zeros_like(o_ref)


def _moe_routed(tok_slabs, slot_token, slot_w, tile_expert, n_active, w1, w3, w2, layer):
    _, n_exp, d, de = w1.shape
    ns = d // (2 * LANES)
    tm = MOE_TILE
    n_slots = slot_token.shape[0]
    nt = n_slots // tm
    grid_spec = pltpu.PrefetchScalarGridSpec(
        num_scalar_prefetch=3,
        grid=(nt,),
        in_specs=[pl.BlockSpec(memory_space=pl.ANY),
                  pl.BlockSpec((tm, 1), lambda t, te, na, st: (t, 0)),
                  pl.BlockSpec((None, None, d, de), lambda t, te, na, st: (layer, te[t], 0, 0)),
                  pl.BlockSpec((None, None, d, de), lambda t, te, na, st: (layer, te[t], 0, 0)),
                  pl.BlockSpec((None, None, de, d), lambda t, te, na, st: (layer, te[t], 0, 0))],
        out_specs=pl.BlockSpec((tm * ns, LANES), lambda t, te, na, st: (t, 0)),
        scratch_shapes=[pltpu.VMEM((2, tm * ns, LANES), jnp.int32), pltpu.VMEM((tm, d), BF16),
                        pltpu.SemaphoreType.DMA((2,))],
    )
    return pl.pallas_call(
        functools.partial(_moe_routed_kernel, ns),
        grid_spec=grid_spec,
        out_shape=jax.ShapeDtypeStruct((n_slots * ns, LANES), jnp.int32),
        compiler_params=_cp(("arbitrary",)),
        name="moe_routed_experts",
    )(tile_expert, n_active, slot_token, tok_slabs, slot_w, w1, w3, w2)


def _moe_combine_kernel(ns, s4_ref, ys_hbm, ysh_ref, x_ref, g_ref, o_ref, ybuf, sem):
    tm = x_ref.shape[0]
    d = x_ref.shape[1]
    i = pl.program_id(0)
    par = i % 2

    def gather_copy(tile, buf, row, k):
        slot = s4_ref[(tile * tm + row) * TOP_K + k]
        return pltpu.make_async_copy(
            ys_hbm.at[pl.ds(pl.multiple_of(slot * ns, ns), ns), :],
            ybuf.at[buf, pl.ds(pl.multiple_of((k * tm + row) * ns, ns), ns), :], sem.at[buf])

    def for_each_copy(tile, buf, act):
        def body(row, carry):
            for k in range(TOP_K):
                act(gather_copy(tile, buf, row, k), k)
            return carry
        lax.fori_loop(0, tm, body, 0)

    start = lambda cp, k: cp.start(priority=k % 2)

    @pl.when(i == 0)
    def _():
        for_each_copy(0, 0, start)

    @pl.when(i + 1 < pl.num_programs(0))
    def _():
        for_each_copy(i + 1, 1 - par, start)

    for_each_copy(i, par, lambda cp, k: cp.wait())

    gate = g_ref[0]
    for s in range(ns):
        lo_cols = slice(s * LANES, (s + 1) * LANES)
        hi_cols = slice(d // 2 + s * LANES, d // 2 + (s + 1) * LANES)
        y_lo = ysh_ref[:, lo_cols].astype(F32)
        y_hi = ysh_ref[:, hi_cols].astype(F32)
        for k in range(TOP_K):
            lo, hi = _unpack_pair(ybuf[par, pl.ds(k * tm * ns + s, tm, stride=ns), :])
            y_lo = y_lo + lo
            y_hi = y_hi + hi
        o_ref[:, lo_cols] = x_ref[:, lo_cols] + gate[:, lo_cols] * y_lo
        o_ref[:, hi_cols] = x_ref[:, hi_cols] + gate[:, hi_cols] * y_hi


def _moe_combine(rows, x_all, y_shared, ys_slabs, slot4, modsflat, layer, which_gate, n_rows):
    d = x_all.shape[1]
    ns = d // (2 * LANES)
    tm = 128
    grid_spec = pltpu.PrefetchScalarGridSpec(
        num_scalar_prefetch=1,
        grid=(n_rows // tm,),
        in_specs=[pl.BlockSpec(memory_space=pl.ANY),
                  pl.BlockSpec((tm, d), lambda i, s4: (i, 0)),
                  pl.BlockSpec((tm, d), lambda i, s4: (i, 0)),
                  pl.BlockSpec((1, 1, d),
                               lambda i, s4: (rows.mod_row(layer, i, tm, which_gate), 0, 0))],
        out_specs=pl.BlockSpec((tm, d), lambda i, s4: (i, 0)),
        scratch_shapes=[pltpu.VMEM((2, TOP_K * tm * ns, LANES), jnp.int32),
                        pltpu.SemaphoreType.DMA((2,))],
    )
    return pl.pallas_call(
        functools.partial(_moe_combine_kernel, ns),
        grid_spec=grid_spec,
        out_shape=jax.ShapeDtypeStruct((n_rows, d), F32),
        compiler_params=_cp(("arbitrary",)),
        name="moe_combine_residual",
    )(slot4, ys_slabs, y_shared, x_all, modsflat)


def _moe_slots(route, pos, counts, n_exp):
    n = route.shape[0]
    tm = MOE_TILE
    idx4 = route[:, :TOP_K].astype(jnp.int32)
    w4 = route[:, TOP_K:2 * TOP_K]
    pos4 = pos[:, :TOP_K].astype(jnp.int32)
    cnt = counts[0, :n_exp].astype(jnp.int32)
    padded = (cnt + tm - 1) // tm * tm
    ends = jnp.cumsum(padded)
    starts = ends - padded
    slot4 = (starts[idx4] + pos4).reshape(-1)
    n_slots = n * TOP_K + n_exp * tm
    nt = n_slots // tm
    tile_start = jnp.arange(nt, dtype=jnp.int32) * tm
    tile_expert = jnp.minimum(
        jnp.sum((ends[None, :] <= tile_start[:, None]).astype(jnp.int32), axis=1), n_exp - 1)
    n_active = (ends[-1:] // tm).astype(jnp.int32)
    token_of = jnp.repeat(jnp.arange(n, dtype=jnp.int32), TOP_K)
    pairs = jnp.stack([token_of, lax.bitcast_convert_type(w4.reshape(-1), jnp.int32)], axis=1)
    slot_meta = jnp.zeros((n_slots, 2), jnp.int32).at[slot4].set(pairs)
    slot_token = slot_meta[:, 0]
    slot_w = lax.bitcast_convert_type(slot_meta[:, 1:2], F32)
    return slot4, slot_token, slot_w, tile_expert, n_active


def _rope_tables(seq, ctx_len):
    n = DA_QK // 4
    t = jnp.arange(seq)
    inv = ROPE_THETA ** (-jnp.arange(n, dtype=F32) / n)
    row = (t // GRID_W).astype(F32)
    col = (t % GRID_W).astype(F32)
    ang = jnp.concatenate([row[:, None] * inv, col[:, None] * inv], axis=-1)
    ang = jnp.concatenate([ang, jnp.zeros((ctx_len, 2 * n), F32)], axis=0)
    return jnp.tile(jnp.cos(ang), (1, 4)), jnp.tile(jnp.sin(ang), (1, 4))


def kernel(x, c, ctx, c_ctx, norm1, norm2, w_ada, b_ada, w_in, hg_lb_logits, hg_norm, da_q_norm,
           da_k_norm, da_lambda, da_sub_norm, ml_igate_bias, ml_fgate_bias, ml_norm, w_branch,
           w_out, router_w, router_bias, exp_w1, exp_w3, exp_w2, sh_w1, sh_w3, sh_w2):
    batch, seq, d = x.shape
    ctx_len = ctx.shape[1]
    depth = w_ada.shape[0]
    n_exp = router_w.shape[-1]
    rows = _Rows(batch, seq, ctx_len)
    assert batch + 1 <= 16 and n_exp <= LANES

    x_all = (x.reshape(batch * seq, d), ctx.reshape(batch * ctx_len, d))
    cvec = jnp.zeros((16, d), F32).at[0].set(c_ctx).at[1:1 + batch].set(c)
    mods = _mods(cvec, w_ada, b_ada)
    modsflat = mods[:, :rows.n_groups].reshape(depth * rows.n_groups * 6, 1, d)
    cos_tab, sin_tab = _rope_tables(seq, ctx_len)
    gate_lo = N_SEG * SEG
    w_in_t = jnp.swapaxes(w_in, 1, 2)
    exp_w1_b, exp_w3_b, exp_w2_b = (w.astype(BF16) for w in (exp_w1, exp_w3, exp_w2))

    for l in range(depth):
        n_rows = rows.n_all if l < depth - 1 else rows.n_lat
        lam_init = 0.8 - 0.6 * math.exp(-0.3 * l)
        hx = _norm(rows, x_all, rows.n_all, norm1[l], modsflat, l, 1, 0)
        p_main = _matmul_wt(hx, w_in_t, l, 0, gate_lo, BF16)
        p_gate = _matmul_wt(hx, w_in_t, l, gate_lo, LANES, F32)
        p_merge = _matmul_wt(hx, w_in_t, l, gate_lo + N_GATE_COLS, N_BRANCH * d, BF16, m=n_rows)
        gates_t = p_gate[:, :N_GATE_COLS].T.reshape(N_GATE_COLS, 1, rows.n_all)

        ohf, ohb = _hgrn_scan(rows, p_main, hg_lb_logits, l)

        tile2 = lambda g: jnp.tile(g.reshape(1, DA_QK), (1, 2))
        qt, kh, vt = _da_prep(rows, p_main, cos_tab, sin_tab, tile2(da_q_norm[l]),
                              tile2(da_k_norm[l]))
        da = _attention(rows, qt, kh, vt, da_lambda[l], da_sub_norm[l].reshape(HEAD_W, 1),
                        lam_init, n_rows)

        bias = lambda bv: jnp.broadcast_to(bv.reshape(2 * HEADS, 1, 1), (2 * HEADS, 1, LANES))
        i_b, f_b = bias(ml_igate_bias[l]), bias(ml_fgate_bias[l])
        omf, omb = _mlstm_scan(rows, p_main, gates_t, i_b, f_b)

        ymid = _merge(p_main, p_merge, ohf, ohb, da, omf, omb, hg_norm[l].reshape(1, HEAD_W),
                      ml_norm[l].reshape(1, HEAD_W), w_branch[l].astype(BF16), d)
        x_all = _proj_resid(rows, ymid, w_out[l].astype(BF16), x_all, modsflat, l, 2)

        rw = jnp.pad(router_w[l], ((0, 0), (0, LANES - n_exp)))
        rw_hi = rw.astype(BF16)
        rw_lo = (rw - rw_hi.astype(F32)).astype(BF16)
        rb = jnp.pad(router_bias[l], (0, LANES - n_exp)).reshape(1, LANES)
        tok, tok_slabs, route = _norm(rows, x_all, n_rows, norm2[l], modsflat, l, 4, 3,
                                      router=(rw_hi, rw_lo, rb, n_exp))
        pos, counts = _moe_plan(route)
        slot4, slot_token, slot_w, tile_expert, n_active = _moe_slots(route, pos, counts, n_exp)
        ys = _moe_routed(tok_slabs, slot_token, slot_w, tile_expert, n_active,
                         exp_w1_b, exp_w3_b, exp_w2_b, l)
        y_sh = _shared_expert(tok, sh_w1[l].astype(BF16), sh_w3[l].astype(BF16),
                              sh_w2[l].astype(BF16))
        x_all = _moe_combine(rows, x_all, y_sh, ys, slot4, modsflat, l, 5, n_rows)

    return x_all.reshape(batch, seq, d)
```

```python
import functools
import math

import jax
import jax.numpy as jnp
from jax import lax
from jax.experimental import pallas as pl
from jax.experimental.pallas import tpu as pltpu

F32 = jnp.float32
BF16 = jnp.bfloat16

EPS = 1e-6
NEG_BIG = -1e30
TINY = 1e-30
GRID_W = 64
ROPE_THETA = 10000.0
ROUTED_SCALE = 2.5
TOP_K = 4
LOG2E = 1.4426950408889634

HEADS = 12
HEAD_W = 128
SEG = HEADS * HEAD_W
DA_QK = 64
N_GATE_COLS = 4 * HEADS

S_HG_Q, S_HG_FF, S_HG_FB, S_HG_I, S_HG_G = 0, 1, 2, 3, 4
S_DA_Q, S_DA_K, S_DA_V = 5, 6, 7
S_ML_Q, S_ML_K, S_ML_V, S_ML_O = 8, 9, 10, 11
N_SEG = 12
N_BRANCH = 3

LANES = 128
SUBLANES = 8
ROW_BLK = 256
SCAN_HEADS = 4
HG_CHUNK = 64
HG_SUB = 16
HG_FAST_SUB = 32
HG_SPREAD_MAX = 60.0
MOE_TILE = 256
ATTN_PAD_ROWS = 16
ATTN_RANGE_MAX = 60.0
VMEM_LIMIT = 56 * 1024 * 1024


def _cp(sem, vmem=VMEM_LIMIT):
    return pltpu.CompilerParams(dimension_semantics=sem, vmem_limit_bytes=vmem)


def _pick(n, cands):
    for c in cands:
        if n % c == 0:
            return c
    raise ValueError(f"no tile for {n} in {cands}")


def _dot(a, b):
    return jnp.dot(a, b, preferred_element_type=F32)


def _dot_nt(a, b):
    return lax.dot_general(a, b, (((1,), (1,)), ((), ())), preferred_element_type=F32)


def _dot_tn(a, b):
    return lax.dot_general(a, b, (((0,), (0,)), ((), ())), preferred_element_type=F32)


def _split(x):
    hi = x.astype(BF16)
    lo = (x - hi.astype(F32)).astype(BF16)
    return hi, lo


def _sigmoid(x):
    return 1.0 / (1.0 + jnp.exp(-x))


class _Rows:
    def __init__(self, batch, seq, ctx_len):
        self.batch, self.seq, self.ctx = batch, seq, ctx_len
        assert seq % ROW_BLK == 0 and ctx_len % ROW_BLK == 0
        self.nlt = seq // ROW_BLK
        self.nct = ctx_len // ROW_BLK
        self.n_lat = batch * seq
        self.n_all = batch * (seq + ctx_len)
        self.n_groups = batch + 1

    def group(self, i, tm):
        lat_tiles = self.n_lat // tm
        return jnp.where(i < lat_tiles, 1 + i // (self.seq // tm), 0)

    def mod_row(self, layer, i, tm, which):
        return (layer * self.n_groups + self.group(i, tm)) * 6 + which

    def scan_block(self, b, s, reverse):
        if reverse:
            cblk = self.batch * self.nlt + b * self.nct + (self.nct - 1 - s)
            lblk = b * self.nlt + (self.nlt - 1 - (s - self.nct))
        else:
            cblk = self.batch * self.nlt + b * self.nct + s
            lblk = b * self.nlt + (s - self.nct)
        return jnp.where(s < self.nct, cblk, lblk)


def _mods_kernel(c_ref, w_ref, b_ref, o_ref):
    c = c_ref[...]
    s_hi, s_lo = _split(c * _sigmoid(c))
    w_hi, w_lo = _split(w_ref[...])
    acc = _dot(s_hi, w_hi) + _dot(s_lo, w_hi) + _dot(s_hi, w_lo)
    o_ref[...] = acc + b_ref[...]


def _mods(cvec, w_ada, b_ada):
    depth, d, n = w_ada.shape
    tn = _pick(n, (512, 256, 128))
    rows = cvec.shape[0]
    return pl.pallas_call(
        _mods_kernel,
        grid=(depth, n // tn),
        in_specs=[pl.BlockSpec((rows, d), lambda l, j: (0, 0)),
                  pl.BlockSpec((None, d, tn), lambda l, j: (l, 0, j)),
                  pl.BlockSpec((None, 1, tn), lambda l, j: (l, 0, j))],
        out_specs=pl.BlockSpec((None, rows, tn), lambda l, j: (l, 0, j)),
        out_shape=jax.ShapeDtypeStruct((depth, rows, n), F32),
        compiler_params=_cp(("arbitrary", "arbitrary")),
        name="adaln_mods",
    )(cvec, w_ada, b_ada.reshape(depth, 1, n))


def _split_rows(x, block, tile_axis):
    if not isinstance(x, tuple):
        return [pl.BlockSpec(block, lambda *g: (g[tile_axis], _other(g, tile_axis)))], [x], None
    n_first = x[0].shape[0] // block[0]
    first = pl.BlockSpec(block, lambda *g: (jnp.minimum(g[tile_axis], n_first - 1),
                                            _other(g, tile_axis)))
    second = pl.BlockSpec(block, lambda *g: (jnp.maximum(g[tile_axis] - n_first, 0),
                                             _other(g, tile_axis)))
    return [first, second], list(x), n_first


def _other(g, tile_axis):
    return g[1 - tile_axis] if len(g) == 2 else 0


def _row_tile(parts, i, n_first):
    if len(parts) == 1:
        return parts[0][...]
    return jnp.where(i < n_first, parts[0][...], parts[1][...])


def _norm_mod(x, g_ref, sc_ref, sh_ref):
    ms = jnp.mean(x * x, axis=-1, keepdims=True)
    y = x * lax.rsqrt(ms + EPS) * g_ref[...]
    return y * (1.0 + sc_ref[0]) + sh_ref[0]


def _norm_kernel(n_first, *refs):
    x_parts, (g_ref, sc_ref, sh_ref, o_ref) = refs[:-4], refs[-4:]
    x = _row_tile(x_parts, pl.program_id(0), n_first)
    o_ref[...] = _norm_mod(x, g_ref, sc_ref, sh_ref).astype(BF16)


def _pack_pair(lo, hi):
    lo_b = pltpu.bitcast(lo.astype(BF16).astype(F32), jnp.int32)
    hi_b = pltpu.bitcast(hi.astype(BF16).astype(F32), jnp.int32)
    return (hi_b & jnp.int32(-65536)) | lax.shift_right_logical(lo_b, 16)


def _unpack_pair(w):
    lo = pltpu.bitcast(lax.shift_left(w, 16), F32)
    hi = pltpu.bitcast(w & jnp.int32(-65536), F32)
    return lo, hi


def _store_slabs(y, o_ref):
    tm, d = y.shape
    ns = d // (2 * LANES)
    for s in range(ns):
        lo = y[:, s * LANES:(s + 1) * LANES]
        hi = y[:, d // 2 + s * LANES:d // 2 + (s + 1) * LANES]
        o_ref[pl.ds(s, tm, stride=ns), :] = _pack_pair(lo, hi)


def _load_slabs(src_ref, row0, tm, ns, emit):
    for s in range(ns):
        lo, hi = _unpack_pair(src_ref[pl.ds(row0 + s, tm, stride=ns), :])
        emit(s, lo, hi)


def _norm_router_kernel(n_exp, x_ref, g_ref, sc_ref, sh_ref, rwh_ref, rwl_ref, rb_ref,
                        o_ref, slab_ref, route_ref):
    y = _norm_mod(x_ref[...], g_ref, sc_ref, sh_ref)
    o_ref[...] = y.astype(BF16)
    _store_slabs(y, slab_ref)
    y_hi, y_lo = _split(y)
    rwh = rwh_ref[...]
    logits = _dot(y_hi, rwh) + _dot(y_lo, rwh) + _dot(y_hi, rwl_ref[...])
    scores = _sigmoid(logits)
    lane = lax.broadcasted_iota(jnp.int32, scores.shape, 1).astype(F32)
    work = jnp.where(lane < n_exp, scores + rb_ref[...], -jnp.inf)
    route = jnp.zeros_like(scores)
    total = jnp.zeros_like(scores[:, :1])
    for k in range(TOP_K):
        mx = jnp.max(work, axis=-1, keepdims=True)
        first = jnp.min(jnp.where(work == mx, lane, float(LANES)), axis=-1, keepdims=True)
        hit = lane == first
        sc = jnp.sum(jnp.where(hit, scores, 0.0), axis=-1, keepdims=True)
        total = total + sc
        route = jnp.where(lane == k, first, route)
        route = jnp.where(lane == TOP_K + k, sc, route)
        work = jnp.where(hit, -jnp.inf, work)
    is_w = (lane >= TOP_K) & (lane < 2 * TOP_K)
    route_ref[...] = jnp.where(is_w, route / total * ROUTED_SCALE, route)


def _norm(rows, x_all, n_all, gain, modsflat, layer, which_scale, which_shift, router=None):
    d = gain.shape[0]
    tm = ROW_BLK

    def mod_idx(which):
        return lambda i: (rows.mod_row(layer, i, tm, which), 0, 0)

    x_specs, x_args, n_first = _split_rows(x_all, (tm, d), 0)
    in_specs = x_specs + [pl.BlockSpec((1, d), lambda i: (0, 0)),
                          pl.BlockSpec((1, 1, d), mod_idx(which_scale)),
                          pl.BlockSpec((1, 1, d), mod_idx(which_shift))]
    args = x_args + [gain.reshape(1, d), modsflat, modsflat]
    out_spec = pl.BlockSpec((tm, d), lambda i: (i, 0))
    out_shape = jax.ShapeDtypeStruct((n_all, d), BF16)
    if router is None:
        return pl.pallas_call(
            functools.partial(_norm_kernel, n_first), grid=(n_all // tm,), in_specs=in_specs,
            out_specs=out_spec, out_shape=out_shape, compiler_params=_cp(("arbitrary",)),
            name="prenorm",
        )(*args)
    rw_hi, rw_lo, rbias, n_exp = router
    ns = d // (2 * LANES)
    in_specs += [pl.BlockSpec((d, LANES), lambda i: (0, 0)),
                 pl.BlockSpec((d, LANES), lambda i: (0, 0)),
                 pl.BlockSpec((1, LANES), lambda i: (0, 0))]
    return pl.pallas_call(
        functools.partial(_norm_router_kernel, n_exp),
        grid=(n_all // tm,), in_specs=in_specs,
        out_specs=[out_spec, pl.BlockSpec((tm * ns, LANES), lambda i: (i, 0)),
                   pl.BlockSpec((tm, LANES), lambda i: (i, 0))],
        out_shape=[out_shape, jax.ShapeDtypeStruct((n_all * ns, LANES), jnp.int32),
                   jax.ShapeDtypeStruct((n_all, LANES), F32)],
        compiler_params=_cp(("arbitrary",)), name="prenorm_router",
    )(*args, rw_hi, rw_lo, rbias)


def _mm_kernel(a_ref, w_ref, o_ref):
    o_ref[...] = _dot(a_ref[...], w_ref[...]).astype(o_ref.dtype)


def _matmul(a, w, out_dtype, m=None, tm=512):
    k = a.shape[1]
    m = a.shape[0] if m is None else m
    n = w.shape[1]
    tn = _pick(n, (1024, 512, 256, 128))
    return pl.pallas_call(
        _mm_kernel,
        grid=(n // tn, m // tm),
        in_specs=[pl.BlockSpec((tm, k), lambda j, i: (i, 0)),
                  pl.BlockSpec((k, tn), lambda j, i: (0, j))],
        out_specs=pl.BlockSpec((tm, tn), lambda j, i: (i, j)),
        out_shape=jax.ShapeDtypeStruct((m, n), out_dtype),
        compiler_params=_cp(("arbitrary", "arbitrary")),
        name="in_proj",
    )(a, w)


def _mm_wt_kernel(a_ref, wt_ref, o_ref, wb_ref):
    @pl.when(pl.program_id(1) == 0)
    def _():
        wb_ref[...] = wt_ref[0].T.astype(BF16)

    o_ref[...] = _dot(a_ref[...], wb_ref[...]).astype(o_ref.dtype)


def _matmul_wt(a, wt_stack, layer, row0, n, out_dtype, m=None, tm=512):
    k = a.shape[1]
    m = a.shape[0] if m is None else m
    tn = _pick(n, (1024, 512, 256, 128))
    return pl.pallas_call(
        _mm_wt_kernel,
        grid=(n // tn, m // tm),
        in_specs=[pl.BlockSpec((tm, k), lambda j, i: (i, 0)),
                  pl.BlockSpec((pl.Element(1), pl.Element(tn), pl.Element(k)),
                               lambda j, i: (layer, pl.multiple_of(row0 + j * tn, SUBLANES), 0))],
        out_specs=pl.BlockSpec((tm, tn), lambda j, i: (i, j)),
        out_shape=jax.ShapeDtypeStruct((m, n), out_dtype),
        scratch_shapes=[pltpu.VMEM((k, tn), BF16)],
        compiler_params=_cp(("arbitrary", "arbitrary")),
        name="in_proj",
    )(a, wt_stack)


def _hgrn_exact_chunk(q, k, bcum, v, st, reverse):
    c = HG_CHUNK
    last = 0 if reverse else c - 1
    b_last = bcum[last:last + 1]
    o_inter = _dot_nt((q * jnp.exp(bcum)).astype(BF16), st.astype(BF16))
    kd = k * jnp.exp(b_last - bcum)
    st_new = st * jnp.exp(b_last) + _dot_tn(v.astype(BF16), kd.astype(BF16))

    ones = jnp.ones((LANES, LANES), BF16)
    sub_row = lax.broadcasted_iota(jnp.int32, (HG_SUB, LANES), 0)
    v16 = v.astype(BF16)
    outs = []
    for i in range(c // HG_SUB):
        r0 = i * HG_SUB
        bsub = bcum[r0:r0 + HG_SUB]
        qsub = q[r0:r0 + HG_SUB]
        zs = []
        for s in range(HG_SUB):
            keep = (sub_row <= s) if reverse else (sub_row >= s)
            dl = jnp.where(keep, bsub - bcum[r0 + s:r0 + s + 1], NEG_BIG)
            zs.append(qsub * (k[r0 + s:r0 + s + 1] * jnp.exp(dl)))
        red = _dot(jnp.concatenate(zs, axis=0).astype(BF16), ones)
        o_sub = red[0:HG_SUB] * v[r0:r0 + 1]
        for s in range(1, HG_SUB):
            o_sub = o_sub + red[s * HG_SUB:(s + 1) * HG_SUB] * v[r0 + s:r0 + s + 1]
        lo, hi = (r0 + HG_SUB, c) if reverse else (0, r0)
        if hi > lo:
            ref_row = bcum[lo:lo + 1] if reverse else bcum[hi - 1:hi]
            qi = (qsub * jnp.exp(bsub - ref_row)).astype(BF16)
            kt = (k[lo:hi] * jnp.exp(ref_row - bcum[lo:hi])).astype(BF16)
            o_sub = o_sub + _dot(_dot_nt(qi, kt).astype(BF16), v16[lo:hi])
        outs.append(o_sub)
    return o_inter + jnp.concatenate(outs, axis=0), st_new


def _hgrn_fast_blocks(dirs):
    c, sub = HG_CHUNK, HG_FAST_SUB
    n_chunk = ROW_BLK // c
    units = [(d, n_chunk - 1 - step if d['reverse'] else step)
             for step in range(n_chunk) for d in dirs]
    qs, xs, eb, scores, vals, outs = {}, {}, {}, {}, {}, {}
    for d, ci in units:
        reverse, u = d['reverse'], (id(d), ci)
        rows = pl.ds(ci * c, c)
        q, k, bcum, v = d['q_s'][rows, :], d['k_s'][rows, :], d['b_s'][rows, :], d['v_s'][rows, :]
        v16 = v.astype(BF16)
        last = 0 if reverse else c - 1
        b_last = bcum[last:last + 1]
        qs[u] = (q * jnp.exp(bcum)).astype(BF16)
        eb[u] = jnp.exp(b_last)
        xs[u] = _dot_tn(v16, (k * jnp.exp(b_last - bcum)).astype(BF16))
        for i in range(c // sub):
            r0 = i * sub
            lo, hi = (r0, c) if reverse else (0, r0 + sub)
            ref_row = bcum[r0 + sub - 1:r0 + sub] if reverse else bcum[r0:r0 + 1]
            qi = (q[r0:r0 + sub] * jnp.exp(bcum[r0:r0 + sub] - ref_row)).astype(BF16)
            kt = (k[lo:hi] * jnp.exp(ref_row - bcum[lo:hi])).astype(BF16)
            scores[u, i] = _dot_nt(qi, kt)
            vals[u, i] = v16[lo:hi]
    for d, ci in units:
        reverse, u = d['reverse'], (id(d), ci)
        parts = []
        for i in range(c // sub):
            r0 = i * sub
            lo, hi = (r0, c) if reverse else (0, r0 + sub)
            key = lax.broadcasted_iota(jnp.int32, (sub, hi - lo), 1) + lo
            qry = lax.broadcasted_iota(jnp.int32, (sub, hi - lo), 0) + r0
            a = jnp.where((key >= qry) if reverse else (key <= qry), scores[u, i], 0.0)
            parts.append(_dot(a.astype(BF16), vals[u, i]))
        outs[u] = jnp.concatenate(parts, axis=0)
    st = {id(d): d['st'][...] for d in dirs}
    for d, ci in units:
        u = (id(d), ci)
        o = outs[u] + _dot_nt(qs[u], st[id(d)].astype(BF16))
        d['o'][pl.ds(ci * c, c), d['cols']] = o.astype(d['o'].dtype)
        st[id(d)] = st[id(d)] * eb[u] + xs[u]
    for d in dirs:
        d['st'][...] = st[id(d)]


def _hgrn_kernel(layer, *refs):
    n_in, n_scr = 4, 5
    outs = refs[2 * n_in:2 * n_in + 2]
    scr = refs[2 * n_in + 2:]
    dirs = []
    for hh in range(SCAN_HEADS):
        cols = slice(hh * HEAD_W, (hh + 1) * HEAD_W)
        for di, reverse in enumerate((False, True)):
            q_ref, f_ref, v_ref, lb_ref = refs[di * n_in:(di + 1) * n_in]
            st, q_s, k_s, b_s, v_s = (r.at[hh] for r in scr[di * n_scr:(di + 1) * n_scr])
            dirs.append(dict(reverse=reverse, cols=cols, q=q_ref, f=f_ref, v=v_ref, lb=lb_ref,
                             o=outs[di], st=st, q_s=q_s, k_s=k_s, b_s=b_s, v_s=v_s))

    @pl.when(pl.program_id(2) == 0)
    def _():
        for d in dirs:
            d['st'][...] = jnp.zeros_like(d['st'])

    r_i = lax.broadcasted_iota(jnp.int32, (ROW_BLK, ROW_BLK), 0)
    c_i = lax.broadcasted_iota(jnp.int32, (ROW_BLK, ROW_BLK), 1)
    same_chunk = r_i // HG_CHUNK == c_i // HG_CHUNK
    tris = {rev: jnp.where(same_chunk & ((c_i >= r_i) if rev else (c_i <= r_i)), 1.0, 0.0
                           ).astype(BF16) for rev in (False, True)}
    spread = jnp.zeros((1, LANES), F32)
    for d in dirs:
        cols = d['cols']
        lg = d['lb'][0][:, cols]
        e = jnp.exp(lg - jnp.max(lg, axis=0, keepdims=True))
        p = e / jnp.sum(e, axis=0, keepdims=True)
        lb = jnp.zeros((1, LANES), F32)
        for i in range(1, layer + 1):
            lb = lb + p[i:i + 1]

        qp = d['q'][:, cols].astype(F32)
        fp = d['f'][:, cols].astype(F32)
        d['q_s'][...] = qp * _sigmoid(qp) * HEAD_W ** -0.5
        d['k_s'][...] = (1.0 - lb) * _sigmoid(-fp)
        d['v_s'][...] = d['v'][:, cols].astype(F32)
        lf = jnp.log(jnp.maximum(lb + (1.0 - lb) * _sigmoid(fp), TINY))
        tri = tris[d['reverse']]
        lf_hi, lf_lo = _split(lf)
        bcum = _dot(tri, lf_hi) + _dot(tri, lf_lo)
        d['b_s'][...] = bcum
        for r0 in range(0, ROW_BLK, HG_FAST_SUB):
            spread = jnp.maximum(
                spread, jnp.abs(bcum[r0:r0 + 1] - bcum[r0 + HG_FAST_SUB - 1:r0 + HG_FAST_SUB]))
    small = jnp.max(spread) < HG_SPREAD_MAX

    @pl.when(small)
    def _():
        _hgrn_fast_blocks(dirs)

    @pl.when(jnp.logical_not(small))
    def _():
        n_chunk = ROW_BLK // HG_CHUNK
        for d in dirs:
            st = d['st'][...]
            for ci in (range(n_chunk - 1, -1, -1) if d['reverse'] else range(n_chunk)):
                rows = pl.ds(ci * HG_CHUNK, HG_CHUNK)
                o, st = _hgrn_exact_chunk(d['q_s'][rows, :], d['k_s'][rows, :], d['b_s'][rows, :],
                                          d['v_s'][rows, :], st, d['reverse'])
                d['o'][rows, d['cols']] = o.astype(d['o'].dtype)
            d['st'][...] = st


def _hgrn_scan(rows, p_main, lb_logits, layer):
    n_all = p_main.shape[0]
    steps = rows.nct + rows.nlt
    depth = lb_logits.shape[1]

    def dir_specs(direction):
        reverse = direction == 1
        blk = lambda b, s: rows.scan_block(b, s, reverse)
        spec = lambda seg: pl.BlockSpec((ROW_BLK, width),
                                        lambda b, g, s: (blk(b, s), seg * groups + g))
        ins = [spec(S_HG_Q), spec(S_HG_FB if reverse else S_HG_FF), spec(S_HG_I),
               pl.BlockSpec((1, depth, width), lambda b, g, s: (direction, 0, g))]
        return ins, pl.BlockSpec((ROW_BLK, width), lambda b, g, s: (blk(b, s), g))

    groups, width = HEADS // SCAN_HEADS, SCAN_HEADS * HEAD_W
    ins_f, out_f = dir_specs(0)
    ins_b, out_b = dir_specs(1)
    out = jax.ShapeDtypeStruct((n_all, SEG), BF16)
    args = (p_main, p_main, p_main, lb_logits)
    return pl.pallas_call(
        functools.partial(_hgrn_kernel, layer),
        grid=(rows.batch, groups, steps),
        in_specs=ins_f + ins_b,
        out_specs=[out_f, out_b],
        out_shape=[out, out],
        scratch_shapes=([pltpu.VMEM((SCAN_HEADS, HEAD_W, HEAD_W), F32)]
                        + [pltpu.VMEM((SCAN_HEADS, ROW_BLK, HEAD_W), F32)] * 4) * 2,
        compiler_params=_cp(("arbitrary", "arbitrary", "arbitrary")),
        name="hgrn2_scan",
    )(*args, *args)


def _mlstm_kernel(*refs):
    c = ROW_BLK
    n_in = 7
    chains = []
    for hh in range(SCAN_HEADS):
        cols = slice(hh * HEAD_W, (hh + 1) * HEAD_W)
        for di, reverse in enumerate((False, True)):
            chains.append(dict(reverse=reverse, hh=hh, cols=cols, ins=refs[di * n_in:(di + 1) * n_in],
                               o=refs[2 * n_in + di], s=refs[2 * n_in + 2 + 2 * di].at[hh],
                               m=refs[2 * n_in + 3 + 2 * di].at[hh]))
    dirs = chains

    @pl.when(pl.program_id(2) == 0)
    def _():
        for d in dirs:
            d['s'][...] = jnp.zeros_like(d['s'])
            d['m'][...] = jnp.zeros_like(d['m'])

    r_i = lax.broadcasted_iota(jnp.int32, (c, c), 0)
    c_i = lax.broadcasted_iota(jnp.int32, (c, c), 1)
    for d in dirs:
        q_ref, k_ref, v_ref, ig_ref, fg_ref, ib_ref, fb_ref = d['ins']
        hh, cols = d['hh'], d['cols']
        d['q'] = q_ref[:, cols]
        d['kt'] = (k_ref[:, cols].astype(F32) * HEAD_W ** -0.5).T
        d['v_cat'] = jnp.concatenate([v_ref[:, cols], jnp.ones((c, HEAD_W), BF16)], axis=1)
        d['ig'] = ig_ref[hh] + ib_ref[hh][:, :1]
        fx = fg_ref[hh] + fb_ref[hh][:, :1]
        lf = jnp.minimum(fx, 0.0) - jnp.log1p(jnp.exp(-jnp.abs(fx)))
        cum = jnp.where((r_i >= c_i) if d['reverse'] else (r_i <= c_i), 1.0, 0.0).astype(BF16)
        lf_hi, lf_lo = _split(jnp.broadcast_to(lf, (8, c)))
        d['brow'] = (_dot(lf_hi, cum) + _dot(lf_lo, cum))[0:1]
    for d in dirs:
        d['qk'] = _dot(d['q'], d['kt'].astype(BF16))
        d['state'] = d['s'][...]
        d['q_state'] = _dot(d['q'], d['state'].astype(BF16))
    for d in dirs:
        b_rows = jnp.broadcast_to(d['brow'], (c, c))
        b_cols = b_rows.T
        keep = (c_i >= r_i) if d['reverse'] else (c_i <= r_i)
        dlog = jnp.where(keep, b_cols - b_rows + d['ig'], NEG_BIG)
        d['m_prev'] = d['m'][:, :1]
        inter = b_cols[:, :1] + d['m_prev']
        d['m_t'] = jnp.maximum(jnp.max(dlog, axis=-1, keepdims=True), inter)
        d['w'] = (jnp.exp(dlog - d['m_t']) * d['qk']).astype(BF16)
        d['w_inter'] = jnp.exp(inter - d['m_t'])
    for d in dirs:
        nd = _dot(d['w'], d['v_cat']) + d['w_inter'] * d['q_state']
        num, den = nd[:, :HEAD_W], nd[:, HEAD_W:]
        d['o'][:, d['cols']] = (num / jnp.maximum(jnp.abs(den), jnp.exp(-d['m_t']))
                                ).astype(d['o'].dtype)
    for d in dirs:
        last = 0 if d['reverse'] else c - 1
        m_new = d['m_t'][last:last + 1]
        b_last = d['brow'][:, last:last + 1]
        wk = jnp.exp(b_last - d['brow'] + d['ig'] - m_new)
        dec = jnp.exp(b_last + d['m_prev'] - m_new)
        d['s'][...] = dec * d['state'] + _dot((d['kt'] * wk).astype(BF16), d['v_cat'])
        d['m'][...] = jnp.broadcast_to(m_new, d['m'].shape)


def _mlstm_scan(rows, p_main, gates_t, i_bias, f_bias):
    n_all = p_main.shape[0]
    steps = rows.nct + rows.nlt

    def dir_specs(direction):
        reverse = direction == 1
        blk = lambda b, s: rows.scan_block(b, s, reverse)
        spec = lambda seg: pl.BlockSpec((ROW_BLK, width),
                                        lambda b, g, s: (blk(b, s), seg * groups + g))
        gate = lambda which: pl.BlockSpec((SCAN_HEADS, 1, ROW_BLK),
                                          lambda b, g, s: (which * groups + g, 0, blk(b, s)))
        bias = pl.BlockSpec((SCAN_HEADS, 1, LANES), lambda b, g, s: (direction * groups + g, 0, 0))
        ins = [spec(S_ML_Q), spec(S_ML_K), spec(S_ML_V), gate(direction), gate(2 + direction),
               bias, bias]
        return ins, pl.BlockSpec((ROW_BLK, width), lambda b, g, s: (blk(b, s), g))

    groups, width = HEADS // SCAN_HEADS, SCAN_HEADS * HEAD_W
    ins_f, out_f = dir_specs(0)
    ins_b, out_b = dir_specs(1)
    out = jax.ShapeDtypeStruct((n_all, SEG), BF16)
    args = (p_main, p_main, p_main, gates_t, gates_t, i_bias, f_bias)
    return pl.pallas_call(
        _mlstm_kernel,
        grid=(rows.batch, groups, steps),
        in_specs=ins_f + ins_b,
        out_specs=[out_f, out_b],
        out_shape=[out, out],
        scratch_shapes=[pltpu.VMEM((SCAN_HEADS, HEAD_W, 2 * HEAD_W), F32),
                        pltpu.VMEM((SCAN_HEADS, 1, LANES), F32)] * 2,
        compiler_params=_cp(("arbitrary", "arbitrary", "arbitrary")),
        name="mlstm_scan",
    )(*args, *args)


def _da_prep_kernel(q_ref, k_ref, v_ref, cos_ref, sin_ref, qg_ref, kg_ref, qt_ref, ko_ref, vt_ref):
    r_i = lax.broadcasted_iota(jnp.int32, (LANES, LANES), 0)
    c_i = lax.broadcasted_iota(jnp.int32, (LANES, LANES), 1)
    blockdiag = jnp.where((r_i // DA_QK) == (c_i // DA_QK), 1.0 / DA_QK, 0.0).astype(BF16)
    cos, sin = cos_ref[...], sin_ref[...]
    lane = lax.broadcasted_iota(jnp.int32, cos.shape, 1)
    first_half = (lane % DA_QK) < (DA_QK // 2)

    def qk_norm_rope(x, gain):
        x_hi, x_lo = _split(x * x)
        ms = _dot(x_hi, blockdiag) + _dot(x_lo, blockdiag)
        y = x * lax.rsqrt(ms + EPS) * gain
        rot = jnp.where(first_half, -pltpu.roll(y, LANES - DA_QK // 2, 1),
                        pltpu.roll(y, DA_QK // 2, 1))
        return y * cos + rot * sin

    for h in range(HEADS):
        cols = slice(h * HEAD_W, (h + 1) * HEAD_W)
        q = qk_norm_rope(q_ref[:, cols].astype(F32), qg_ref[...]) * (DA_QK ** -0.5 * LOG2E)
        qt_ref[cols, :] = q.T.astype(BF16)
        ko_ref[:, cols] = qk_norm_rope(k_ref[:, cols].astype(F32), kg_ref[...]).astype(BF16)
        vt_ref[cols, :] = v_ref[:, cols].astype(F32).T.astype(BF16)


def _da_prep(rows, p_main, cos_tab, sin_tab, q_gain, k_gain):
    n_all = p_main.shape[0]
    tm = ROW_BLK
    out_t = jax.ShapeDtypeStruct((SEG, n_all), BF16)

    def tab_idx(i):
        lat = i % rows.nlt
        ctx = rows.nlt + (i - rows.batch * rows.nlt) % rows.nct
        return (jnp.where(i < rows.batch * rows.nlt, lat, ctx), 0)

    out = jax.ShapeDtypeStruct((n_all, SEG), BF16)
    return pl.pallas_call(
        _da_prep_kernel,
        grid=(n_all // tm,),
        in_specs=[pl.BlockSpec((tm, SEG), lambda i: (i, S_DA_Q)),
                  pl.BlockSpec((tm, SEG), lambda i: (i, S_DA_K)),
                  pl.BlockSpec((tm, SEG), lambda i: (i, S_DA_V)),
                  pl.BlockSpec((tm, LANES), tab_idx),
                  pl.BlockSpec((tm, LANES), tab_idx),
                  pl.BlockSpec((1, LANES), lambda i: (0, 0)),
                  pl.BlockSpec((1, LANES), lambda i: (0, 0))],
        out_specs=[pl.BlockSpec((SEG, tm), lambda i: (0, i)),
                   pl.BlockSpec((tm, SEG), lambda i: (i, 0)),
                   pl.BlockSpec((SEG, tm), lambda i: (0, i))],
        out_shape=[out_t, out, out_t],
        compiler_params=_cp(("arbitrary",)),
        name="da_qk_prep",
    )(p_main, p_main, p_main, cos_tab, sin_tab, q_gain, k_gain)


def _attn_kernel(lam_init, n_lat_q_blocks, n_lat_keys, tk, qt_ref, kc_ref, vct_ref, kl_ref, vlt_ref,
                 lam_ref, sn_ref, o_ref, acc_ref, m_ref, sa_ref, sb_ref, kn_ref):
    tq = qt_ref.shape[1]
    qt = qt_ref[...]
    chan = lax.broadcasted_iota(jnp.int32, qt.shape, 0)
    zero = jnp.zeros_like(qt)
    q2t = jnp.concatenate([jnp.where(chan < DA_QK, qt, zero), jnp.where(chan >= DA_QK, qt, zero)],
                          axis=1)

    def weighted_values(vt, st, m):
        p = jnp.exp2((st - m).astype(BF16))
        ones = jnp.ones((ATTN_PAD_ROWS, vt.shape[1]), BF16)
        return _dot(jnp.concatenate([vt, ones], axis=0), p)

    st = _dot(kc_ref[...], q2t)
    m0 = jnp.max(st, axis=0, keepdims=True)
    m_ref[...] = m0
    acc_ref[...] = weighted_values(vct_ref[...], st, m0)

    n_chunks = n_lat_keys // tk
    unroll = _pick(n_chunks, (4, 2, 1))
    bufs = (sa_ref, sb_ref) if unroll > 1 else (sa_ref, sa_ref)

    def chunk(c):
        return pl.ds(pl.multiple_of(c * tk, tk), tk)

    def scores(c, buf):
        st = _dot(kl_ref[chunk(c), :], q2t)
        buf[...] = st
        return jnp.max(st, axis=0, keepdims=True)

    def absorb(c, buf, m_cur):
        m_old = m_ref[...]
        m_new = jnp.maximum(m_old, m_cur)
        pv = weighted_values(vlt_ref[:, chunk(c)], buf[...], m_new)
        acc_ref[...] = jnp.exp2(m_old - m_new) * acc_ref[...] + pv
        m_ref[...] = m_new

    def body(j, m_pend):
        for u in range(unroll):
            c = j * unroll + u
            if unroll > 1:
                m_next = scores(jnp.minimum(c + 1, n_chunks - 1), bufs[(u + 1) % 2])
                absorb(c, bufs[u % 2], m_pend)
            else:
                absorb(c, bufs[0], m_pend)
                m_next = scores(jnp.minimum(c + 1, n_chunks - 1), bufs[0])
            m_pend = m_next
        return m_pend

    n_trips = jnp.where(pl.program_id(2) < n_lat_q_blocks, n_chunks // unroll, 0)

    @pl.when(pl.program_id(2) == 0)
    def _():
        r_i = lax.broadcasted_iota(jnp.int32, (LANES, LANES), 0)
        c_i = lax.broadcasted_iota(jnp.int32, (LANES, LANES), 1)
        same_map = jnp.where((r_i // DA_QK) == (c_i // DA_QK), 1.0, 0.0).astype(BF16)

        def max_sq_norm(k):
            kf = k.astype(F32)
            hi, lo = _split(kf * kf)
            return jnp.max(_dot(hi, same_map) + _dot(lo, same_map), axis=0, keepdims=True)

        def body_norm(c, best):
            return jnp.maximum(best, max_sq_norm(kl_ref[chunk(c), :]))
        kn_ref[...] = lax.fori_loop(0, n_chunks, body_norm, max_sq_norm(kc_ref[...]))

    qf = q2t.astype(F32)
    q_sq = jnp.sum(qf * qf, axis=0, keepdims=True)
    col = lax.broadcasted_iota(jnp.int32, q_sq.shape, 1)
    k_sq = jnp.where(col < tq, kn_ref[:, 0:1], kn_ref[:, DA_QK:DA_QK + 1])
    bound = jnp.sqrt(q_sq * k_sq) * 1.01 + 0.01
    bounded = jnp.max(bound - m0) <= ATTN_RANGE_MAX

    def body_bounded(j, carry):
        for u in range(unroll):
            c = j * unroll + u
            st = _dot(kl_ref[chunk(c), :], q2t)
            acc_ref[...] += weighted_values(vlt_ref[:, chunk(c)], st, m0)
        return carry

    @pl.when(bounded)
    def _():
        lax.fori_loop(0, n_trips, body_bounded, 0)

    @pl.when(jnp.logical_not(bounded))
    def _():
        lax.fori_loop(0, n_trips, body, scores(0, bufs[0]))

    lv = lam_ref[...]
    lam = (jnp.exp(jnp.sum(lv[0:1] * lv[1:2], axis=-1, keepdims=True))
           - jnp.exp(jnp.sum(lv[2:3] * lv[3:4], axis=-1, keepdims=True)) + lam_init)
    o = acc_ref[0:HEAD_W, :] / acc_ref[HEAD_W:HEAD_W + 1, :]
    a = o[:, :tq] - lam * o[:, tq:]
    ms = jnp.mean(a * a, axis=0, keepdims=True)
    y = a * lax.rsqrt(ms + EPS) * sn_ref[...] * (1.0 - lam_init)
    o_ref[...] = y.T.astype(o_ref.dtype)


def _attention_call(rows, qt, kh, vt, lam_vec, sub_gain, lam_init, n_rows, prev):
    assert rows.nct == 1
    ctx_blk = lambda b: rows.batch * rows.nlt + b
    tk = _pick(rows.seq, (512, 256))
    if prev is None:
        tq = _pick(rows.seq, (512, 256))
        n_q = rows.seq // tq
        q_blk = lambda b, i: b * n_q + i
        kern = functools.partial(_attn_kernel, lam_init, n_q, rows.seq, tk)
        extra_specs, extra_args, aliases = [], [], {}
    else:
        tq, n_q = ROW_BLK, 0
        q_blk = lambda b, i: ctx_blk(b)
        attn = functools.partial(_attn_kernel, lam_init, n_q, rows.seq, tk)
        kern = lambda *refs: attn(*refs[:7], *refs[8:])
        extra_specs, extra_args, aliases = [pl.BlockSpec(memory_space=pl.ANY)], [prev], {7: 0}
    return pl.pallas_call(
        kern,
        grid=(rows.batch, HEADS, max(n_q, 1)),
        in_specs=[pl.BlockSpec((HEAD_W, tq), lambda b, h, i: (h, q_blk(b, i))),
                  pl.BlockSpec((ROW_BLK, HEAD_W), lambda b, h, i: (ctx_blk(b), h)),
                  pl.BlockSpec((HEAD_W, ROW_BLK), lambda b, h, i: (h, ctx_blk(b))),
                  pl.BlockSpec((rows.seq, HEAD_W), lambda b, h, i: (b, h)),
                  pl.BlockSpec((HEAD_W, rows.seq), lambda b, h, i: (h, b)),
                  pl.BlockSpec(lam_vec.shape, lambda b, h, i: (0, 0)),
                  pl.BlockSpec((HEAD_W, 1), lambda b, h, i: (0, 0))] + extra_specs,
        out_specs=pl.BlockSpec((tq, HEAD_W), lambda b, h, i: (q_blk(b, i), h)),
        out_shape=jax.ShapeDtypeStruct((n_rows, SEG), BF16),
        scratch_shapes=[pltpu.VMEM((HEAD_W + ATTN_PAD_ROWS, 2 * tq), F32),
                        pltpu.VMEM((1, 2 * tq), F32), pltpu.VMEM((tk, 2 * tq), F32),
                        pltpu.VMEM((tk, 2 * tq), F32), pltpu.VMEM((1, LANES), F32)],
        input_output_aliases=aliases,
        compiler_params=_cp(("arbitrary", "arbitrary", "arbitrary")),
        name="diff_attention" if prev is None else "diff_attention_ctx",
    )(qt, kh, vt, kh, vt, lam_vec, sub_gain, *extra_args)


def _attention(rows, qt, kh, vt, lam_vec, sub_gain, lam_init, n_rows):
    da = _attention_call(rows, qt, kh, vt, lam_vec, sub_gain, lam_init, n_rows, None)
    if n_rows > rows.n_lat:
        da = _attention_call(rows, qt, kh, vt, lam_vec, sub_gain, lam_init, n_rows, da)
    return da


def _merge_kernel(ohf_ref, ohb_ref, hg_ref, da_ref, omf_ref, omb_ref, mo_ref, g0_ref, g1_ref,
                  g2_ref, hn_ref, mn_ref, wb_ref, o_ref, h_ref):
    @pl.when(pl.program_id(1) == 0)
    def _():
        for h in range(HEADS):
            cols = slice(h * HEAD_W, (h + 1) * HEAD_W)
            o = ohf_ref[:, cols].astype(F32) + ohb_ref[:, cols].astype(F32)
            y = o * lax.rsqrt(jnp.mean(o * o, axis=-1, keepdims=True) + EPS) * hn_ref[...]
            g = hg_ref[:, cols].astype(F32)
            h_ref[0, :, cols] = (y * (g * _sigmoid(g))).astype(BF16)
            o = omf_ref[:, cols].astype(F32) + omb_ref[:, cols].astype(F32)
            y = o * lax.rsqrt(jnp.mean(o * o, axis=-1, keepdims=True) + EPS) * mn_ref[...]
            h_ref[2, :, cols] = (y * _sigmoid(mo_ref[:, cols].astype(F32))).astype(BF16)
        h_ref[1] = da_ref[...]

    y = _sigmoid(g0_ref[...].astype(F32)) * _dot(h_ref[0], wb_ref[0])
    y = y + _sigmoid(g1_ref[...].astype(F32)) * _dot(h_ref[1], wb_ref[1])
    y = y + _sigmoid(g2_ref[...].astype(F32)) * _dot(h_ref[2], wb_ref[2])
    o_ref[...] = y.astype(o_ref.dtype)


def _merge(p_main, p_merge, ohf, ohb, da, omf, omb, hg_gain, ml_gain, w_branch, d):
    n_all = da.shape[0]
    tm = 512
    tn = _pick(d, (512, 256, 128))
    row = lambda i, j: (i, 0)

    def gate_spec(jj):
        return pl.BlockSpec((tm, tn), lambda i, j: (i, jj * (d // tn) + j))

    return pl.pallas_call(
        _merge_kernel,
        grid=(n_all // tm, d // tn),
        in_specs=[pl.BlockSpec((tm, SEG), row), pl.BlockSpec((tm, SEG), row),
                  pl.BlockSpec((tm, SEG), lambda i, j: (i, S_HG_G)),
                  pl.BlockSpec((tm, SEG), row),
                  pl.BlockSpec((tm, SEG), row), pl.BlockSpec((tm, SEG), row),
                  pl.BlockSpec((tm, SEG), lambda i, j: (i, S_ML_O)),
                  gate_spec(0), gate_spec(1), gate_spec(2),
                  pl.BlockSpec((1, LANES), lambda i, j: (0, 0)),
                  pl.BlockSpec((1, LANES), lambda i, j: (0, 0)),
                  pl.BlockSpec((3, SEG, tn), lambda i, j: (0, 0, j))],
        out_specs=pl.BlockSpec((tm, tn), lambda i, j: (i, j)),
        out_shape=jax.ShapeDtypeStruct((n_all, d), BF16),
        scratch_shapes=[pltpu.VMEM((3, tm, SEG), BF16)],
        compiler_params=_cp(("arbitrary", "arbitrary")),
        name="branch_merge",
    )(ohf, ohb, p_main, da, omf, omb, p_main, p_merge, p_merge, p_merge, hg_gain, ml_gain,
      w_branch)


def _proj_resid_kernel(n_first, a_ref, w_ref, g_ref, *refs):
    x_parts, o_ref = refs[:-1], refs[-1]
    x = _row_tile(x_parts, pl.program_id(1), n_first)
    o_ref[...] = x + g_ref[0] * _dot(a_ref[...], w_ref[...])


def _proj_resid(rows, a, w, x_all, modsflat, layer, which_gate):
    n_all, k = a.shape
    d = w.shape[1]
    tm = 512
    tn = _pick(d, (1024, 512, 256, 128))
    x_specs, x_args, n_first = _split_rows(x_all, (tm, tn), 1)
    return pl.pallas_call(
        functools.partial(_proj_resid_kernel, n_first),
        grid=(d // tn, n_all // tm),
        in_specs=[pl.BlockSpec((tm, k), lambda j, i: (i, 0)),
                  pl.BlockSpec((k, tn), lambda j, i: (0, j)),
                  pl.BlockSpec((1, 1, tn),
                               lambda j, i: (rows.mod_row(layer, i, tm, which_gate), 0, j))]
        + x_specs,
        out_specs=pl.BlockSpec((tm, tn), lambda j, i: (i, j)),
        out_shape=jax.ShapeDtypeStruct((n_all, d), F32),
        compiler_params=_cp(("arbitrary", "arbitrary")),
        name="out_proj_residual",
    )(a, w, modsflat, *x_args)


def _swiglu_hidden(x, w1_ref, w3_ref):
    h1 = _dot(x, w1_ref[...])
    h3 = _dot(x, w3_ref[...])
    return (h1 * _sigmoid(h1)) * h3


def _shared_expert_kernel(tok_ref, w1_ref, w3_ref, w2_ref, o_ref):
    a = _swiglu_hidden(tok_ref[...], w1_ref, w3_ref)
    o_ref[...] = _dot(a.astype(BF16), w2_ref[...]).astype(o_ref.dtype)


def _shared_expert(tok, w1, w3, w2):
    n_all, d = tok.shape
    de = w1.shape[1]
    tm = 512
    full = lambda i: (0, 0)
    return pl.pallas_call(
        _shared_expert_kernel,
        grid=(n_all // tm,),
        in_specs=[pl.BlockSpec((tm, d), lambda i: (i, 0)),
                  pl.BlockSpec((d, de), full), pl.BlockSpec((d, de), full),
                  pl.BlockSpec((de, d), full)],
        out_specs=pl.BlockSpec((tm, d), lambda i: (i, 0)),
        out_shape=jax.ShapeDtypeStruct((n_all, d), BF16),
        compiler_params=_cp(("arbitrary",)),
        name="moe_shared_expert",
    )(tok, w1, w3, w2)


def _moe_plan_kernel(route_ref, pos_ref, cnt_ref, carry_ref):
    @pl.when(pl.program_id(0) == 0)
    def _():
        carry_ref[...] = jnp.zeros_like(carry_ref)

    r = route_ref[...]
    tm = r.shape[0]
    lane = lax.broadcasted_iota(jnp.int32, r.shape, 1).astype(F32)
    mask = jnp.zeros_like(r)
    for k in range(TOP_K):
        mask = mask + jnp.where(lane == r[:, k:k + 1], 1.0, 0.0)
    r_i = lax.broadcasted_iota(jnp.int32, (tm, tm), 0)
    c_i = lax.broadcasted_iota(jnp.int32, (tm, tm), 1)
    before = jnp.where(c_i < r_i, 1.0, 0.0).astype(BF16)
    rank = _dot(before, mask.astype(BF16)) + carry_ref[...]
    out = jnp.zeros_like(r)
    for k in range(TOP_K):
        pk = jnp.sum(jnp.where(lane == r[:, k:k + 1], rank, 0.0), axis=-1, keepdims=True)
        out = jnp.where(lane == k, pk, out)
    pos_ref[...] = out
    carry = carry_ref[...] + jnp.sum(mask, axis=0, keepdims=True)
    carry_ref[...] = carry
    cnt_ref[...] = jnp.broadcast_to(carry, cnt_ref.shape)


def _moe_plan(route):
    n = route.shape[0]
    tm = ROW_BLK
    return pl.pallas_call(
        _moe_plan_kernel,
        grid=(n // tm,),
        in_specs=[pl.BlockSpec((tm, LANES), lambda i: (i, 0))],
        out_specs=[pl.BlockSpec((tm, LANES), lambda i: (i, 0)),
                   pl.BlockSpec((8, LANES), lambda i: (0, 0))],
        out_shape=[jax.ShapeDtypeStruct((n, LANES), F32), jax.ShapeDtypeStruct((8, LANES), F32)],
        scratch_shapes=[pltpu.VMEM((1, LANES), F32)],
        compiler_params=_cp(("arbitrary",)),
        name="moe_plan",
    )(route)


def _moe_routed_kernel(ns, te_ref, na_ref, st_ref, tok_hbm, sw_ref, w1_ref, w3_ref, w2_ref, o_ref,
                       xbuf, x_ref, sem):
    tm = MOE_TILE
    t = pl.program_id(0)
    n_active = na_ref[0]
    par = t % 2

    def gather_copy(tile, buf, r):
        tok = st_ref[tile * tm + r]
        return pltpu.make_async_copy(
            tok_hbm.at[pl.ds(pl.multiple_of(tok * ns, ns), ns), :],
            xbuf.at[buf, pl.ds(pl.multiple_of(r * ns, ns), ns), :], sem.at[buf])

    def for_each_copy(tile, buf, act):
        def body(r4, carry):
            for u in range(4):
                act(gather_copy(tile, buf, r4 * 4 + u), u)
            return carry
        lax.fori_loop(0, tm // 4, body, 0)

    start = lambda cp, u: cp.start(priority=u % 2)

    @pl.when(t == 0)
    def _():
        for_each_copy(0, 0, start)

    @pl.when(t + 1 < n_active)
    def _():
        for_each_copy(t + 1, 1 - par, start)

    @pl.when(t < n_active)
    def _():
        for_each_copy(t, par, lambda cp, u: cp.wait())
        d = x_ref.shape[1]

        def emit(s, lo, hi):
            x_ref[:, s * LANES:(s + 1) * LANES] = lo.astype(BF16)
            x_ref[:, d // 2 + s * LANES:d // 2 + (s + 1) * LANES] = hi.astype(BF16)
        _load_slabs(xbuf.at[par], 0, tm, ns, emit)
        a = _swiglu_hidden(x_ref[...], w1_ref, w3_ref) * sw_ref[...]
        _store_slabs(_dot(a.astype(BF16), w2_ref[...]), o_ref)

    @pl.when(t >= n_active)
    def _():
        o_ref[...] = jnp.zeros_like(o_ref)


def _moe_routed(tok_slabs, slot_token, slot_w, tile_expert, n_active, w1, w3, w2, layer):
    _, n_exp, d, de = w1.shape
    ns = d // (2 * LANES)
    tm = MOE_TILE
    n_slots = slot_token.shape[0]
    nt = n_slots // tm
    grid_spec = pltpu.PrefetchScalarGridSpec(
        num_scalar_prefetch=3,
        grid=(nt,),
        in_specs=[pl.BlockSpec(memory_space=pl.ANY),
                  pl.BlockSpec((tm, 1), lambda t, te, na, st: (t, 0)),
                  pl.BlockSpec((None, None, d, de), lambda t, te, na, st: (layer, te[t], 0, 0)),
                  pl.BlockSpec((None, None, d, de), lambda t, te, na, st: (layer, te[t], 0, 0)),
                  pl.BlockSpec((None, None, de, d), lambda t, te, na, st: (layer, te[t], 0, 0))],
        out_specs=pl.BlockSpec((tm * ns, LANES), lambda t, te, na, st: (t, 0)),
        scratch_shapes=[pltpu.VMEM((2, tm * ns, LANES), jnp.int32), pltpu.VMEM((tm, d), BF16),
                        pltpu.SemaphoreType.DMA((2,))],
    )
    return pl.pallas_call(
        functools.partial(_moe_routed_kernel, ns),
        grid_spec=grid_spec,
        out_shape=jax.ShapeDtypeStruct((n_slots * ns, LANES), jnp.int32),
        compiler_params=_cp(("arbitrary",)),
        name="moe_routed_experts",
    )(tile_expert, n_active, slot_token, tok_slabs, slot_w, w1, w3, w2)


def _moe_combine_kernel(ns, s4_ref, ys_hbm, ysh_ref, x_ref, g_ref, o_ref, ybuf, sem):
    tm = x_ref.shape[0]
    d = x_ref.shape[1]
    i = pl.program_id(0)
    par = i % 2

    def gather_copy(tile, buf, row, k):
        slot = s4_ref[(tile * tm + row) * TOP_K + k]
        return pltpu.make_async_copy(
            ys_hbm.at[pl.ds(pl.multiple_of(slot * ns, ns), ns), :],
            ybuf.at[buf, pl.ds(pl.multiple_of((k * tm + row) * ns, ns), ns), :], sem.at[buf])

    def for_each_copy(tile, buf, act):
        def body(row, carry):
            for k in range(TOP_K):
                act(gather_copy(tile, buf, row, k), k)
            return carry
        lax.fori_loop(0, tm, body, 0)

    start = lambda cp, k: cp.start(priority=k % 2)

    @pl.when(i == 0)
    def _():
        for_each_copy(0, 0, start)

    @pl.when(i + 1 < pl.num_programs(0))
    def _():
        for_each_copy(i + 1, 1 - par, start)

    for_each_copy(i, par, lambda cp, k: cp.wait())

    gate = g_ref[0]
    for s in range(ns):
        lo_cols = slice(s * LANES, (s + 1) * LANES)
        hi_cols = slice(d // 2 + s * LANES, d // 2 + (s + 1) * LANES)
        y_lo = ysh_ref[:, lo_cols].astype(F32)
        y_hi = ysh_ref[:, hi_cols].astype(F32)
        for k in range(TOP_K):
            lo, hi = _unpack_pair(ybuf[par, pl.ds(k * tm * ns + s, tm, stride=ns), :])
            y_lo = y_lo + lo
            y_hi = y_hi + hi
        o_ref[:, lo_cols] = x_ref[:, lo_cols] + gate[:, lo_cols] * y_lo
        o_ref[:, hi_cols] = x_ref[:, hi_cols] + gate[:, hi_cols] * y_hi


def _moe_combine(rows, x_all, y_shared, ys_slabs, slot4, modsflat, layer, which_gate, n_rows):
    d = x_all.shape[1]
    ns = d // (2 * LANES)
    tm = 128
    grid_spec = pltpu.PrefetchScalarGridSpec(
        num_scalar_prefetch=1,
        grid=(n_rows // tm,),
        in_specs=[pl.BlockSpec(memory_space=pl.ANY),
                  pl.BlockSpec((tm, d), lambda i, s4: (i, 0)),
                  pl.BlockSpec((tm, d), lambda i, s4: (i, 0)),
                  pl.BlockSpec((1, 1, d),
                               lambda i, s4: (rows.mod_row(layer, i, tm, which_gate), 0, 0))],
        out_specs=pl.BlockSpec((tm, d), lambda i, s4: (i, 0)),
        scratch_shapes=[pltpu.VMEM((2, TOP_K * tm * ns, LANES), jnp.int32),
                        pltpu.SemaphoreType.DMA((2,))],
    )
    return pl.pallas_call(
        functools.partial(_moe_combine_kernel, ns),
        grid_spec=grid_spec,
        out_shape=jax.ShapeDtypeStruct((n_rows, d), F32),
        compiler_params=_cp(("arbitrary",)),
        name="moe_combine_residual",
    )(slot4, ys_slabs, y_shared, x_all, modsflat)


def _moe_slots(route, pos, counts, n_exp):
    n = route.shape[0]
    tm = MOE_TILE
    idx4 = route[:, :TOP_K].astype(jnp.int32)
    w4 = route[:, TOP_K:2 * TOP_K]
    pos4 = pos[:, :TOP_K].astype(jnp.int32)
    cnt = counts[0, :n_exp].astype(jnp.int32)
    padded = (cnt + tm - 1) // tm * tm
    ends = jnp.cumsum(padded)
    starts = ends - padded
    slot4 = (starts[idx4] + pos4).reshape(-1)
    n_slots = n * TOP_K + n_exp * tm
    nt = n_slots // tm
    tile_start = jnp.arange(nt, dtype=jnp.int32) * tm
    tile_expert = jnp.minimum(
        jnp.sum((ends[None, :] <= tile_start[:, None]).astype(jnp.int32), axis=1), n_exp - 1)
    n_active = (ends[-1:] // tm).astype(jnp.int32)
    token_of = jnp.repeat(jnp.arange(n, dtype=jnp.int32), TOP_K)
    pairs = jnp.stack([token_of, lax.bitcast_convert_type(w4.reshape(-1), jnp.int32)], axis=1)
    slot_meta = jnp.zeros((n_slots, 2), jnp.int32).at[slot4].set(pairs)
    slot_token = slot_meta[:, 0]
    slot_w = lax.bitcast_convert_type(slot_meta[:, 1:2], F32)
    return slot4, slot_token, slot_w, tile_expert, n_active


def _rope_tables(seq, ctx_len):
    n = DA_QK // 4
    t = jnp.arange(seq)
    inv = ROPE_THETA ** (-jnp.arange(n, dtype=F32) / n)
    row = (t // GRID_W).astype(F32)
    col = (t % GRID_W).astype(F32)
    ang = jnp.concatenate([row[:, None] * inv, col[:, None] * inv], axis=-1)
    ang = jnp.concatenate([ang, jnp.zeros((ctx_len, 2 * n), F32)], axis=0)
    return jnp.tile(jnp.cos(ang), (1, 4)), jnp.tile(jnp.sin(ang), (1, 4))


def kernel(x, c, ctx, c_ctx, norm1, norm2, w_ada, b_ada, w_in, hg_lb_logits, hg_norm, da_q_norm,
           da_k_norm, da_lambda, da_sub_norm, ml_igate_bias, ml_fgate_bias, ml_norm, w_branch,
           w_out, router_w, router_bias, exp_w1, exp_w3, exp_w2, sh_w1, sh_w3, sh_w2):
    batch, seq, d = x.shape
    ctx_len = ctx.shape[1]
    depth = w_ada.shape[0]
    n_exp = router_w.shape[-1]
    rows = _Rows(batch, seq, ctx_len)
    assert batch + 1 <= 16 and n_exp <= LANES

    x_all = (x.reshape(batch * seq, d), ctx.reshape(batch * ctx_len, d))
    cvec = jnp.zeros((16, d), F32).at[0].set(c_ctx).at[1:1 + batch].set(c)
    mods = _mods(cvec, w_ada, b_ada)
    modsflat = mods[:, :rows.n_groups].reshape(depth * rows.n_groups * 6, 1, d)
    cos_tab, sin_tab = _rope_tables(seq, ctx_len)
    gate_lo = N_SEG * SEG
    w_in_t = jnp.swapaxes(w_in, 1, 2)
    exp_w1_b, exp_w3_b, exp_w2_b = (w.astype(BF16) for w in (exp_w1, exp_w3, exp_w2))

    for l in range(depth):
        n_rows = rows.n_all if l < depth - 1 else rows.n_lat
        lam_init = 0.8 - 0.6 * math.exp(-0.3 * l)
        hx = _norm(rows, x_all, rows.n_all, norm1[l], modsflat, l, 1, 0)
        p_main = _matmul_wt(hx, w_in_t, l, 0, gate_lo, BF16)
        p_gate = _matmul_wt(hx, w_in_t, l, gate_lo, LANES, F32)
        p_merge = _matmul_wt(hx, w_in_t, l, gate_lo + N_GATE_COLS, N_BRANCH * d, BF16, m=n_rows)
        gates_t = p_gate[:, :N_GATE_COLS].T.reshape(N_GATE_COLS, 1, rows.n_all)

        ohf, ohb = _hgrn_scan(rows, p_main, hg_lb_logits, l)

        tile2 = lambda g: jnp.tile(g.reshape(1, DA_QK), (1, 2))
        qt, kh, vt = _da_prep(rows, p_main, cos_tab, sin_tab, tile2(da_q_norm[l]),
                              tile2(da_k_norm[l]))
        da = _attention(rows, qt, kh, vt, da_lambda[l], da_sub_norm[l].reshape(HEAD_W, 1),
                        lam_init, n_rows)

        bias = lambda bv: jnp.broadcast_to(bv.reshape(2 * HEADS, 1, 1), (2 * HEADS, 1, LANES))
        i_b, f_b = bias(ml_igate_bias[l]), bias(ml_fgate_bias[l])
        omf, omb = _mlstm_scan(rows, p_main, gates_t, i_b, f_b)

        ymid = _merge(p_main, p_merge, ohf, ohb, da, omf, omb, hg_norm[l].reshape(1, HEAD_W),
                      ml_norm[l].reshape(1, HEAD_W), w_branch[l].astype(BF16), d)
        x_all = _proj_resid(rows, ymid, w_out[l].astype(BF16), x_all, modsflat, l, 2)

        rw = jnp.pad(router_w[l], ((0, 0), (0, LANES - n_exp)))
        rw_hi = rw.astype(BF16)
        rw_lo = (rw - rw_hi.astype(F32)).astype(BF16)
        rb = jnp.pad(router_bias[l], (0, LANES - n_exp)).reshape(1, LANES)
        tok, tok_slabs, route = _norm(rows, x_all, n_rows, norm2[l], modsflat, l, 4, 3,
                                      router=(rw_hi, rw_lo, rb, n_exp))
        pos, counts = _moe_plan(route)
        slot4, slot_token, slot_w, tile_expert, n_active = _moe_slots(route, pos, counts, n_exp)
        ys = _moe_routed(tok_slabs, slot_token, slot_w, tile_expert, n_active,
                         exp_w1_b, exp_w3_b, exp_w2_b, l)
        y_sh = _shared_expert(tok, sh_w1[l].astype(BF16), sh_w3[l].astype(BF16),
                              sh_w2[l].astype(BF16))
        x_all = _moe_combine(rows, x_all, y_sh, ys, slot4, modsflat, l, 5, n_rows)

    return x_all.reshape(batch, seq, d)
```

```python
import functools
import math

import jax
import jax.numpy as jnp
from jax import lax
from jax.experimental import pallas as pl
from jax.experimental.pallas import tpu as pltpu

F32 = jnp.float32
BF16 = jnp.bfloat16

EPS = 1e-6
NEG_BIG = -1e30
TINY = 1e-30
GRID_W = 64
ROPE_THETA = 10000.0
ROUTED_SCALE = 2.5
TOP_K = 4
LOG2E = 1.4426950408889634

HEADS = 12
HEAD_W = 128
SEG = HEADS * HEAD_W
DA_QK = 64
N_GATE_COLS = 4 * HEADS

S_HG_Q, S_HG_FF, S_HG_FB, S_HG_I, S_HG_G = 0, 1, 2, 3, 4
S_DA_Q, S_DA_K, S_DA_V = 5, 6, 7
S_ML_Q, S_ML_K, S_ML_V, S_ML_O = 8, 9, 10, 11
N_SEG = 12
N_BRANCH = 3

LANES = 128
SUBLANES = 8
ROW_BLK = 256
SCAN_HEADS = 4
HG_CHUNK = 64
HG_SUB = 16
HG_FAST_SUB = 32
HG_SPREAD_MAX = 60.0
MOE_TILE = 256
ATTN_PAD_ROWS = 16
ATTN_RANGE_MAX = 60.0
VMEM_LIMIT = 56 * 1024 * 1024


def _cp(sem, vmem=VMEM_LIMIT):
    return pltpu.CompilerParams(dimension_semantics=sem, vmem_limit_bytes=vmem)


def _pick(n, cands):
    for c in cands:
        if n % c == 0:
            return c
    raise ValueError(f"no tile for {n} in {cands}")


def _dot(a, b):
    return jnp.dot(a, b, preferred_element_type=F32)


def _dot_nt(a, b):
    return lax.dot_general(a, b, (((1,), (1,)), ((), ())), preferred_element_type=F32)


def _dot_tn(a, b):
    return lax.dot_general(a, b, (((0,), (0,)), ((), ())), preferred_element_type=F32)


def _split(x):
    hi = x.astype(BF16)
    lo = (x - hi.astype(F32)).astype(BF16)
    return hi, lo


def _sigmoid(x):
    return 1.0 / (1.0 + jnp.exp(-x))


class _Rows:
    def __init__(self, batch, seq, ctx_len):
        self.batch, self.seq, self.ctx = batch, seq, ctx_len
        assert seq % ROW_BLK == 0 and ctx_len % ROW_BLK == 0
        self.nlt = seq // ROW_BLK
        self.nct = ctx_len // ROW_BLK
        self.n_lat = batch * seq
        self.n_all = batch * (seq + ctx_len)
        self.n_groups = batch + 1

    def group(self, i, tm):
        lat_tiles = self.n_lat // tm
        return jnp.where(i < lat_tiles, 1 + i // (self.seq // tm), 0)

    def mod_row(self, layer, i, tm, which):
        return (layer * self.n_groups + self.group(i, tm)) * 6 + which

    def scan_block(self, b, s, reverse):
        if reverse:
            cblk = self.batch * self.nlt + b * self.nct + (self.nct - 1 - s)
            lblk = b * self.nlt + (self.nlt - 1 - (s - self.nct))
        else:
            cblk = self.batch * self.nlt + b * self.nct + s
            lblk = b * self.nlt + (s - self.nct)
        return jnp.where(s < self.nct, cblk, lblk)


def _mods_kernel(c_ref, w_ref, b_ref, o_ref):
    c = c_ref[...]
    s_hi, s_lo = _split(c * _sigmoid(c))
    w_hi, w_lo = _split(w_ref[...])
    acc = _dot(s_hi, w_hi) + _dot(s_lo, w_hi) + _dot(s_hi, w_lo)
    o_ref[...] = acc + b_ref[...]


def _mods(cvec, w_ada, b_ada):
    depth, d, n = w_ada.shape
    tn = _pick(n, (512, 256, 128))
    rows = cvec.shape[0]
    return pl.pallas_call(
        _mods_kernel,
        grid=(depth, n // tn),
        in_specs=[pl.BlockSpec((rows, d), lambda l, j: (0, 0)),
                  pl.BlockSpec((None, d, tn), lambda l, j: (l, 0, j)),
                  pl.BlockSpec((None, 1, tn), lambda l, j: (l, 0, j))],
        out_specs=pl.BlockSpec((None, rows, tn), lambda l, j: (l, 0, j)),
        out_shape=jax.ShapeDtypeStruct((depth, rows, n), F32),
        compiler_params=_cp(("arbitrary", "arbitrary")),
        name="adaln_mods",
    )(cvec, w_ada, b_ada.reshape(depth, 1, n))


def _split_rows(x, block, tile_axis):
    if not isinstance(x, tuple):
        return [pl.BlockSpec(block, lambda *g: (g[tile_axis], _other(g, tile_axis)))], [x], None
    n_first = x[0].shape[0] // block[0]
    first = pl.BlockSpec(block, lambda *g: (jnp.minimum(g[tile_axis], n_first - 1),
                                            _other(g, tile_axis)))
    second = pl.BlockSpec(block, lambda *g: (jnp.maximum(g[tile_axis] - n_first, 0),
                                             _other(g, tile_axis)))
    return [first, second], list(x), n_first


def _other(g, tile_axis):
    return g[1 - tile_axis] if len(g) == 2 else 0


def _row_tile(parts, i, n_first):
    if len(parts) == 1:
        return parts[0][...]
    return jnp.where(i < n_first, parts[0][...], parts[1][...])


def _norm_mod(x, g_ref, sc_ref, sh_ref):
    ms = jnp.mean(x * x, axis=-1, keepdims=True)
    y = x * lax.rsqrt(ms + EPS) * g_ref[...]
    return y * (1.0 + sc_ref[0]) + sh_ref[0]


def _norm_kernel(n_first, *refs):
    x_parts, (g_ref, sc_ref, sh_ref, o_ref) = refs[:-4], refs[-4:]
    x = _row_tile(x_parts, pl.program_id(0), n_first)
    o_ref[...] = _norm_mod(x, g_ref, sc_ref, sh_ref).astype(BF16)


def _pack_pair(lo, hi):
    lo_b = pltpu.bitcast(lo.astype(BF16).astype(F32), jnp.int32)
    hi_b = pltpu.bitcast(hi.astype(BF16).astype(F32), jnp.int32)
    return (hi_b & jnp.int32(-65536)) | lax.shift_right_logical(lo_b, 16)


def _unpack_pair(w):
    lo = pltpu.bitcast(lax.shift_left(w, 16), F32)
    hi = pltpu.bitcast(w & jnp.int32(-65536), F32)
    return lo, hi


def _store_slabs(y, o_ref):
    tm, d = y.shape
    ns = d // (2 * LANES)
    for s in range(ns):
        lo = y[:, s * LANES:(s + 1) * LANES]
        hi = y[:, d // 2 + s * LANES:d // 2 + (s + 1) * LANES]
        o_ref[pl.ds(s, tm, stride=ns), :] = _pack_pair(lo, hi)


def _load_slabs(src_ref, row0, tm, ns, emit):
    for s in range(ns):
        lo, hi = _unpack_pair(src_ref[pl.ds(row0 + s, tm, stride=ns), :])
        emit(s, lo, hi)


def _norm_router_kernel(n_exp, x_ref, g_ref, sc_ref, sh_ref, rwh_ref, rwl_ref, rb_ref,
                        o_ref, slab_ref, route_ref):
    y = _norm_mod(x_ref[...], g_ref, sc_ref, sh_ref)
    o_ref[...] = y.astype(BF16)
    _store_slabs(y, slab_ref)
    y_hi, y_lo = _split(y)
    rwh = rwh_ref[...]
    logits = _dot(y_hi, rwh) + _dot(y_lo, rwh) + _dot(y_hi, rwl_ref[...])
    scores = _sigmoid(logits)
    lane = lax.broadcasted_iota(jnp.int32, scores.shape, 1).astype(F32)
    work = jnp.where(lane < n_exp, scores + rb_ref[...], -jnp.inf)
    route = jnp.zeros_like(scores)
    total = jnp.zeros_like(scores[:, :1])
    for k in range(TOP_K):
        mx = jnp.max(work, axis=-1, keepdims=True)
        first = jnp.min(jnp.where(work == mx, lane, float(LANES)), axis=-1, keepdims=True)
        hit = lane == first
        sc = jnp.sum(jnp.where(hit, scores, 0.0), axis=-1, keepdims=True)
        total = total + sc
        route = jnp.where(lane == k, first, route)
        route = jnp.where(lane == TOP_K + k, sc, route)
        work = jnp.where(hit, -jnp.inf, work)
    is_w = (lane >= TOP_K) & (lane < 2 * TOP_K)
    route_ref[...] = jnp.where(is_w, route / total * ROUTED_SCALE, route)


def _norm(rows, x_all, n_all, gain, modsflat, layer, which_scale, which_shift, router=None):
    d = gain.shape[0]
    tm = ROW_BLK

    def mod_idx(which):
        return lambda i: (rows.mod_row(layer, i, tm, which), 0, 0)

    x_specs, x_args, n_first = _split_rows(x_all, (tm, d), 0)
    in_specs = x_specs + [pl.BlockSpec((1, d), lambda i: (0, 0)),
                          pl.BlockSpec((1, 1, d), mod_idx(which_scale)),
                          pl.BlockSpec((1, 1, d), mod_idx(which_shift))]
    args = x_args + [gain.reshape(1, d), modsflat, modsflat]
    out_spec = pl.BlockSpec((tm, d), lambda i: (i, 0))
    out_shape = jax.ShapeDtypeStruct((n_all, d), BF16)
    if router is None:
        return pl.pallas_call(
            functools.partial(_norm_kernel, n_first), grid=(n_all // tm,), in_specs=in_specs,
            out_specs=out_spec, out_shape=out_shape, compiler_params=_cp(("arbitrary",)),
            name="prenorm",
        )(*args)
    rw_hi, rw_lo, rbias, n_exp = router
    ns = d // (2 * LANES)
    in_specs += [pl.BlockSpec((d, LANES), lambda i: (0, 0)),
                 pl.BlockSpec((d, LANES), lambda i: (0, 0)),
                 pl.BlockSpec((1, LANES), lambda i: (0, 0))]
    return pl.pallas_call(
        functools.partial(_norm_router_kernel, n_exp),
        grid=(n_all // tm,), in_specs=in_specs,
        out_specs=[out_spec, pl.BlockSpec((tm * ns, LANES), lambda i: (i, 0)),
                   pl.BlockSpec((tm, LANES), lambda i: (i, 0))],
        out_shape=[out_shape, jax.ShapeDtypeStruct((n_all * ns, LANES), jnp.int32),
                   jax.ShapeDtypeStruct((n_all, LANES), F32)],
        compiler_params=_cp(("arbitrary",)), name="prenorm_router",
    )(*args, rw_hi, rw_lo, rbias)


def _mm_kernel(a_ref, w_ref, o_ref):
    o_ref[...] = _dot(a_ref[...], w_ref[...]).astype(o_ref.dtype)


def _matmul(a, w, out_dtype, m=None, tm=512):
    k = a.shape[1]
    m = a.shape[0] if m is None else m
    n = w.shape[1]
    tn = _pick(n, (1024, 512, 256, 128))
    return pl.pallas_call(
        _mm_kernel,
        grid=(n // tn, m // tm),
        in_specs=[pl.BlockSpec((tm, k), lambda j, i: (i, 0)),
                  pl.BlockSpec((k, tn), lambda j, i: (0, j))],
        out_specs=pl.BlockSpec((tm, tn), lambda j, i: (i, j)),
        out_shape=jax.ShapeDtypeStruct((m, n), out_dtype),
        compiler_params=_cp(("arbitrary", "arbitrary")),
        name="in_proj",
    )(a, w)


def _mm_wt_kernel(a_ref, wt_ref, o_ref, wb_ref):
    @pl.when(pl.program_id(1) == 0)
    def _():
        wb_ref[...] = wt_ref[0].T.astype(BF16)

    o_ref[...] = _dot(a_ref[...], wb_ref[...]).astype(o_ref.dtype)


def _matmul_wt(a, wt_stack, layer, row0, n, out_dtype, m=None, tm=512):
    k = a.shape[1]
    m = a.shape[0] if m is None else m
    tn = _pick(n, (1024, 512, 256, 128))
    return pl.pallas_call(
        _mm_wt_kernel,
        grid=(n // tn, m // tm),
        in_specs=[pl.BlockSpec((tm, k), lambda j, i: (i, 0)),
                  pl.BlockSpec((pl.Element(1), pl.Element(tn), pl.Element(k)),
                               lambda j, i: (layer, pl.multiple_of(row0 + j * tn, SUBLANES), 0))],
        out_specs=pl.BlockSpec((tm, tn), lambda j, i: (i, j)),
        out_shape=jax.ShapeDtypeStruct((m, n), out_dtype),
        scratch_shapes=[pltpu.VMEM((k, tn), BF16)],
        compiler_params=_cp(("arbitrary", "arbitrary")),
        name="in_proj",
    )(a, wt_stack)


def _hgrn_exact_chunk(q, k, bcum, v, st, reverse):
    c = HG_CHUNK
    last = 0 if reverse else c - 1
    b_last = bcum[last:last + 1]
    o_inter = _dot_nt((q * jnp.exp(bcum)).astype(BF16), st.astype(BF16))
    kd = k * jnp.exp(b_last - bcum)
    st_new = st * jnp.exp(b_last) + _dot_tn(v.astype(BF16), kd.astype(BF16))

    ones = jnp.ones((LANES, LANES), BF16)
    sub_row = lax.broadcasted_iota(jnp.int32, (HG_SUB, LANES), 0)
    v16 = v.astype(BF16)
    outs = []
    for i in range(c // HG_SUB):
        r0 = i * HG_SUB
        bsub = bcum[r0:r0 + HG_SUB]
        qsub = q[r0:r0 + HG_SUB]
        zs = []
        for s in range(HG_SUB):
            keep = (sub_row <= s) if reverse else (sub_row >= s)
            dl = jnp.where(keep, bsub - bcum[r0 + s:r0 + s + 1], NEG_BIG)
            zs.append(qsub * (k[r0 + s:r0 + s + 1] * jnp.exp(dl)))
        red = _dot(jnp.concatenate(zs, axis=0).astype(BF16), ones)
        o_sub = red[0:HG_SUB] * v[r0:r0 + 1]
        for s in range(1, HG_SUB):
            o_sub = o_sub + red[s * HG_SUB:(s + 1) * HG_SUB] * v[r0 + s:r0 + s + 1]
        lo, hi = (r0 + HG_SUB, c) if reverse else (0, r0)
        if hi > lo:
            ref_row = bcum[lo:lo + 1] if reverse else bcum[hi - 1:hi]
            qi = (qsub * jnp.exp(bsub - ref_row)).astype(BF16)
            kt = (k[lo:hi] * jnp.exp(ref_row - bcum[lo:hi])).astype(BF16)
            o_sub = o_sub + _dot(_dot_nt(qi, kt).astype(BF16), v16[lo:hi])
        outs.append(o_sub)
    return o_inter + jnp.concatenate(outs, axis=0), st_new


def _hgrn_fast_blocks(dirs):
    c, sub = HG_CHUNK, HG_FAST_SUB
    n_chunk = ROW_BLK // c
    units = [(d, n_chunk - 1 - step if d['reverse'] else step)
             for step in range(n_chunk) for d in dirs]
    qs, xs, eb, scores, vals, outs = {}, {}, {}, {}, {}, {}
    for d, ci in units:
        reverse, u = d['reverse'], (id(d), ci)
        rows = pl.ds(ci * c, c)
        q, k, bcum, v = d['q_s'][rows, :], d['k_s'][rows, :], d['b_s'][rows, :], d['v_s'][rows, :]
        v16 = v.astype(BF16)
        last = 0 if reverse else c - 1
        b_last = bcum[last:last + 1]
        qs[u] = (q * jnp.exp(bcum)).astype(BF16)
        eb[u] = jnp.exp(b_last)
        xs[u] = _dot_tn(v16, (k * jnp.exp(b_last - bcum)).astype(BF16))
        for i in range(c // sub):
            r0 = i * sub
            lo, hi = (r0, c) if reverse else (0, r0 + sub)
            ref_row = bcum[r0 + sub - 1:r0 + sub] if reverse else bcum[r0:r0 + 1]
            qi = (q[r0:r0 + sub] * jnp.exp(bcum[r0:r0 + sub] - ref_row)).astype(BF16)
            kt = (k[lo:hi] * jnp.exp(ref_row - bcum[lo:hi])).astype(BF16)
            scores[u, i] = _dot_nt(qi, kt)
            vals[u, i] = v16[lo:hi]
    for d, ci in units:
        reverse, u = d['reverse'], (id(d), ci)
        parts = []
        for i in range(c // sub):
            r0 = i * sub
            lo, hi = (r0, c) if reverse else (0, r0 + sub)
            key = lax.broadcasted_iota(jnp.int32, (sub, hi - lo), 1) + lo
            qry = lax.broadcasted_iota(jnp.int32, (sub, hi - lo), 0) + r0
            a = jnp.where((key >= qry) if reverse else (key <= qry), scores[u, i], 0.0)
            parts.append(_dot(a.astype(BF16), vals[u, i]))
        outs[u] = jnp.concatenate(parts, axis=0)
    st = {id(d): d['st'][...] for d in dirs}
    for d, ci in units:
        u = (id(d), ci)
        o = outs[u] + _dot_nt(qs[u], st[id(d)].astype(BF16))
        d['o'][pl.ds(ci * c, c), d['cols']] = o.astype(d['o'].dtype)
        st[id(d)] = st[id(d)] * eb[u] + xs[u]
    for d in dirs:
        d['st'][...] = st[id(d)]


def _hgrn_kernel(layer, *refs):
    n_in, n_scr = 4, 5
    outs = refs[2 * n_in:2 * n_in + 2]
    scr = refs[2 * n_in + 2:]
    dirs = []
    for hh in range(SCAN_HEADS):
        cols = slice(hh * HEAD_W, (hh + 1) * HEAD_W)
        for di, reverse in enumerate((False, True)):
            q_ref, f_ref, v_ref, lb_ref = refs[di * n_in:(di + 1) * n_in]
            st, q_s, k_s, b_s, v_s = (r.at[hh] for r in scr[di * n_scr:(di + 1) * n_scr])
            dirs.append(dict(reverse=reverse, cols=cols, q=q_ref, f=f_ref, v=v_ref, lb=lb_ref,
                             o=outs[di], st=st, q_s=q_s, k_s=k_s, b_s=b_s, v_s=v_s))

    @pl.when(pl.program_id(2) == 0)
    def _():
        for d in dirs:
            d['st'][...] = jnp.zeros_like(d['st'])

    r_i = lax.broadcasted_iota(jnp.int32, (ROW_BLK, ROW_BLK), 0)
    c_i = lax.broadcasted_iota(jnp.int32, (ROW_BLK, ROW_BLK), 1)
    same_chunk = r_i // HG_CHUNK == c_i // HG_CHUNK
    tris = {rev: jnp.where(same_chunk & ((c_i >= r_i) if rev else (c_i <= r_i)), 1.0, 0.0
                           ).astype(BF16) for rev in (False, True)}
    spread = jnp.zeros((1, LANES), F32)
    for d in dirs:
        cols = d['cols']
        lg = d['lb'][0][:, cols]
        e = jnp.exp(lg - jnp.max(lg, axis=0, keepdims=True))
        p = e / jnp.sum(e, axis=0, keepdims=True)
        lb = jnp.zeros((1, LANES), F32)
        for i in range(1, layer + 1):
            lb = lb + p[i:i + 1]

        qp = d['q'][:, cols].astype(F32)
        fp = d['f'][:, cols].astype(F32)
        d['q_s'][...] = qp * _sigmoid(qp) * HEAD_W ** -0.5
        d['k_s'][...] = (1.0 - lb) * _sigmoid(-fp)
        d['v_s'][...] = d['v'][:, cols].astype(F32)
        lf = jnp.log(jnp.maximum(lb + (1.0 - lb) * _sigmoid(fp), TINY))
        tri = tris[d['reverse']]
        lf_hi, lf_lo = _split(lf)
        bcum = _dot(tri, lf_hi) + _dot(tri, lf_lo)
        d['b_s'][...] = bcum
        for r0 in range(0, ROW_BLK, HG_FAST_SUB):
            spread = jnp.maximum(
                spread, jnp.abs(bcum[r0:r0 + 1] - bcum[r0 + HG_FAST_SUB - 1:r0 + HG_FAST_SUB]))
    small = jnp.max(spread) < HG_SPREAD_MAX

    @pl.when(small)
    def _():
        _hgrn_fast_blocks(dirs)

    @pl.when(jnp.logical_not(small))
    def _():
        n_chunk = ROW_BLK // HG_CHUNK
        for d in dirs:
            st = d['st'][...]
            for ci in (range(n_chunk - 1, -1, -1) if d['reverse'] else range(n_chunk)):
                rows = pl.ds(ci * HG_CHUNK, HG_CHUNK)
                o, st = _hgrn_exact_chunk(d['q_s'][rows, :], d['k_s'][rows, :], d['b_s'][rows, :],
                                          d['v_s'][rows, :], st, d['reverse'])
                d['o'][rows, d['cols']] = o.astype(d['o'].dtype)
            d['st'][...] = st


def _hgrn_scan(rows, p_main, lb_logits, layer):
    n_all = p_main.shape[0]
    steps = rows.nct + rows.nlt
    depth = lb_logits.shape[1]

    def dir_specs(direction):
        reverse = direction == 1
        blk = lambda b, s: rows.scan_block(b, s, reverse)
        spec = lambda seg: pl.BlockSpec((ROW_BLK, width),
                                        lambda b, g, s: (blk(b, s), seg * groups + g))
        ins = [spec(S_HG_Q), spec(S_HG_FB if reverse else S_HG_FF), spec(S_HG_I),
               pl.BlockSpec((1, depth, width), lambda b, g, s: (direction, 0, g))]
        return ins, pl.BlockSpec((ROW_BLK, width), lambda b, g, s: (blk(b, s), g))

    groups, width = HEADS // SCAN_HEADS, SCAN_HEADS * HEAD_W
    ins_f, out_f = dir_specs(0)
    ins_b, out_b = dir_specs(1)
    out = jax.ShapeDtypeStruct((n_all, SEG), BF16)
    args = (p_main, p_main, p_main, lb_logits)
    return pl.pallas_call(
        functools.partial(_hgrn_kernel, layer),
        grid=(rows.batch, groups, steps),
        in_specs=ins_f + ins_b,
        out_specs=[out_f, out_b],
        out_shape=[out, out],
        scratch_shapes=([pltpu.VMEM((SCAN_HEADS, HEAD_W, HEAD_W), F32)]
                        + [pltpu.VMEM((SCAN_HEADS, ROW_BLK, HEAD_W), F32)] * 4) * 2,
        compiler_params=_cp(("arbitrary", "arbitrary", "arbitrary")),
        name="hgrn2_scan",
    )(*args, *args)


def _mlstm_kernel(*refs):
    c = ROW_BLK
    n_in = 7
    chains = []
    for hh in range(SCAN_HEADS):
        cols = slice(hh * HEAD_W, (hh + 1) * HEAD_W)
        for di, reverse in enumerate((False, True)):
            chains.append(dict(reverse=reverse, hh=hh, cols=cols, ins=refs[di * n_in:(di + 1) * n_in],
                               o=refs[2 * n_in + di], s=refs[2 * n_in + 2 + 2 * di].at[hh],
                               m=refs[2 * n_in + 3 + 2 * di].at[hh]))
    dirs = chains

    @pl.when(pl.program_id(2) == 0)
    def _():
        for d in dirs:
            d['s'][...] = jnp.zeros_like(d['s'])
            d['m'][...] = jnp.zeros_like(d['m'])

    r_i = lax.broadcasted_iota(jnp.int32, (c, c), 0)
    c_i = lax.broadcasted_iota(jnp.int32, (c, c), 1)
    for d in dirs:
        q_ref, k_ref, v_ref, ig_ref, fg_ref, ib_ref, fb_ref = d['ins']
        hh, cols = d['hh'], d['cols']
        d['q'] = q_ref[:, cols]
        d['kt'] = (k_ref[:, cols].astype(F32) * HEAD_W ** -0.5).T
        d['v_cat'] = jnp.concatenate([v_ref[:, cols], jnp.ones((c, HEAD_W), BF16)], axis=1)
        d['ig'] = ig_ref[hh] + ib_ref[hh][:, :1]
        fx = fg_ref[hh] + fb_ref[hh][:, :1]
        lf = jnp.minimum(fx, 0.0) - jnp.log1p(jnp.exp(-jnp.abs(fx)))
        cum = jnp.where((r_i >= c_i) if d['reverse'] else (r_i <= c_i), 1.0, 0.0).astype(BF16)
        lf_hi, lf_lo = _split(jnp.broadcast_to(lf, (8, c)))
        d['brow'] = (_dot(lf_hi, cum) + _dot(lf_lo, cum))[0:1]
    for d in dirs:
        d['qk'] = _dot(d['q'], d['kt'].astype(BF16))
        d['state'] = d['s'][...]
        d['q_state'] = _dot(d['q'], d['state'].astype(BF16))
    for d in dirs:
        b_rows = jnp.broadcast_to(d['brow'], (c, c))
        b_cols = b_rows.T
        keep = (c_i >= r_i) if d['reverse'] else (c_i <= r_i)
        dlog = jnp.where(keep, b_cols - b_rows + d['ig'], NEG_BIG)
        d['m_prev'] = d['m'][:, :1]
        inter = b_cols[:, :1] + d['m_prev']
        d['m_t'] = jnp.maximum(jnp.max(dlog, axis=-1, keepdims=True), inter)
        d['w'] = (jnp.exp(dlog - d['m_t']) * d['qk']).astype(BF16)
        d['w_inter'] = jnp.exp(inter - d['m_t'])
    for d in dirs:
        nd = _dot(d['w'], d['v_cat']) + d['w_inter'] * d['q_state']
        num, den = nd[:, :HEAD_W], nd[:, HEAD_W:]
        d['o'][:, d['cols']] = (num / jnp.maximum(jnp.abs(den), jnp.exp(-d['m_t']))
                                ).astype(d['o'].dtype)
    for d in dirs:
        last = 0 if d['reverse'] else c - 1
        m_new = d['m_t'][last:last + 1]
        b_last = d['brow'][:, last:last + 1]
        wk = jnp.exp(b_last - d['brow'] + d['ig'] - m_new)
        dec = jnp.exp(b_last + d['m_prev'] - m_new)
        d['s'][...] = dec * d['state'] + _dot((d['kt'] * wk).astype(BF16), d['v_cat'])
        d['m'][...] = jnp.broadcast_to(m_new, d['m'].shape)


def _mlstm_scan(rows, p_main, gates_t, i_bias, f_bias):
    n_all = p_main.shape[0]
    steps = rows.nct + rows.nlt

    def dir_specs(direction):
        reverse = direction == 1
        blk = lambda b, s: rows.scan_block(b, s, reverse)
        spec = lambda seg: pl.BlockSpec((ROW_BLK, width),
                                        lambda b, g, s: (blk(b, s), seg * groups + g))
        gate = lambda which: pl.BlockSpec((SCAN_HEADS, 1, ROW_BLK),
                                          lambda b, g, s: (which * groups + g, 0, blk(b, s)))
        bias = pl.BlockSpec((SCAN_HEADS, 1, LANES), lambda b, g, s: (direction * groups + g, 0, 0))
        ins = [spec(S_ML_Q), spec(S_ML_K), spec(S_ML_V), gate(direction), gate(2 + direction),
               bias, bias]
        return ins, pl.BlockSpec((ROW_BLK, width), lambda b, g, s: (blk(b, s), g))

    groups, width = HEADS // SCAN_HEADS, SCAN_HEADS * HEAD_W
    ins_f, out_f = dir_specs(0)
    ins_b, out_b = dir_specs(1)
    out = jax.ShapeDtypeStruct((n_all, SEG), BF16)
    args = (p_main, p_main, p_main, gates_t, gates_t, i_bias, f_bias)
    return pl.pallas_call(
        _mlstm_kernel,
        grid=(rows.batch, groups, steps),
        in_specs=ins_f + ins_b,
        out_specs=[out_f, out_b],
        out_shape=[out, out],
        scratch_shapes=[pltpu.VMEM((SCAN_HEADS, HEAD_W, 2 * HEAD_W), F32),
                        pltpu.VMEM((SCAN_HEADS, 1, LANES), F32)] * 2,
        compiler_params=_cp(("arbitrary", "arbitrary", "arbitrary")),
        name="mlstm_scan",
    )(*args, *args)


def _da_prep_kernel(q_ref, k_ref, v_ref, cos_ref, sin_ref, qg_ref, kg_ref, qt_ref, ko_ref, vt_ref):
    r_i = lax.broadcasted_iota(jnp.int32, (LANES, LANES), 0)
    c_i = lax.broadcasted_iota(jnp.int32, (LANES, LANES), 1)
    blockdiag = jnp.where((r_i // DA_QK) == (c_i // DA_QK), 1.0 / DA_QK, 0.0).astype(BF16)
    cos, sin = cos_ref[...], sin_ref[...]
    lane = lax.broadcasted_iota(jnp.int32, cos.shape, 1)
    first_half = (lane % DA_QK) < (DA_QK // 2)

    def qk_norm_rope(x, gain):
        x_hi, x_lo = _split(x * x)
        ms = _dot(x_hi, blockdiag) + _dot(x_lo, blockdiag)
        y = x * lax.rsqrt(ms + EPS) * gain
        rot = jnp.where(first_half, -pltpu.roll(y, LANES - DA_QK // 2, 1),
                        pltpu.roll(y, DA_QK // 2, 1))
        return y * cos + rot * sin

    for h in range(HEADS):
        cols = slice(h * HEAD_W, (h + 1) * HEAD_W)
        q = qk_norm_rope(q_ref[:, cols].astype(F32), qg_ref[...]) * (DA_QK ** -0.5 * LOG2E)
        qt_ref[cols, :] = q.T.astype(BF16)
        ko_ref[:, cols] = qk_norm_rope(k_ref[:, cols].astype(F32), kg_ref[...]).astype(BF16)
        vt_ref[cols, :] = v_ref[:, cols].astype(F32).T.astype(BF16)


def _da_prep(rows, p_main, cos_tab, sin_tab, q_gain, k_gain):
    n_all = p_main.shape[0]
    tm = ROW_BLK
    out_t = jax.ShapeDtypeStruct((SEG, n_all), BF16)

    def tab_idx(i):
        lat = i % rows.nlt
        ctx = rows.nlt + (i - rows.batch * rows.nlt) % rows.nct
        return (jnp.where(i < rows.batch * rows.nlt, lat, ctx), 0)

    out = jax.ShapeDtypeStruct((n_all, SEG), BF16)
    return pl.pallas_call(
        _da_prep_kernel,
        grid=(n_all // tm,),
        in_specs=[pl.BlockSpec((tm, SEG), lambda i: (i, S_DA_Q)),
                  pl.BlockSpec((tm, SEG), lambda i: (i, S_DA_K)),
                  pl.BlockSpec((tm, SEG), lambda i: (i, S_DA_V)),
                  pl.BlockSpec((tm, LANES), tab_idx),
                  pl.BlockSpec((tm, LANES), tab_idx),
                  pl.BlockSpec((1, LANES), lambda i: (0, 0)),
                  pl.BlockSpec((1, LANES), lambda i: (0, 0))],
        out_specs=[pl.BlockSpec((SEG, tm), lambda i: (0, i)),
                   pl.BlockSpec((tm, SEG), lambda i: (i, 0)),
                   pl.BlockSpec((SEG, tm), lambda i: (0, i))],
        out_shape=[out_t, out, out_t],
        compiler_params=_cp(("arbitrary",)),
        name="da_qk_prep",
    )(p_main, p_main, p_main, cos_tab, sin_tab, q_gain, k_gain)


def _attn_kernel(lam_init, n_lat_q_blocks, n_lat_keys, tk, qt_ref, kc_ref, vct_ref, kl_ref, vlt_ref,
                 lam_ref, sn_ref, o_ref, acc_ref, m_ref, sa_ref, sb_ref, kn_ref):
    tq = qt_ref.shape[1]
    qt = qt_ref[...]
    chan = lax.broadcasted_iota(jnp.int32, qt.shape, 0)
    zero = jnp.zeros_like(qt)
    q2t = jnp.concatenate([jnp.where(chan < DA_QK, qt, zero), jnp.where(chan >= DA_QK, qt, zero)],
                          axis=1)

    def weighted_values(vt, st, m):
        p = jnp.exp2((st - m).astype(BF16))
        ones = jnp.ones((ATTN_PAD_ROWS, vt.shape[1]), BF16)
        return _dot(jnp.concatenate([vt, ones], axis=0), p)

    st = _dot(kc_ref[...], q2t)
    m0 = jnp.max(st, axis=0, keepdims=True)
    m_ref[...] = m0
    acc_ref[...] = weighted_values(vct_ref[...], st, m0)

    n_chunks = n_lat_keys // tk
    unroll = _pick(n_chunks, (16, 8, 4, 2, 1))
    bufs = (sa_ref, sb_ref) if unroll > 1 else (sa_ref, sa_ref)

    def chunk(c):
        return pl.ds(pl.multiple_of(c * tk, tk), tk)

    def scores(c, buf):
        st = _dot(kl_ref[chunk(c), :], q2t)
        buf[...] = st
        return jnp.max(st, axis=0, keepdims=True)

    def absorb(c, buf, m_cur):
        m_old = m_ref[...]
        m_new = jnp.maximum(m_old, m_cur)
        pv = weighted_values(vlt_ref[:, chunk(c)], buf[...], m_new)
        acc_ref[...] = jnp.exp2(m_old - m_new) * acc_ref[...] + pv
        m_ref[...] = m_new

    def body(j, m_pend):
        for u in range(unroll):
            c = j * unroll + u
            if unroll > 1:
                m_next = scores(jnp.minimum(c + 1, n_chunks - 1), bufs[(u + 1) % 2])
                absorb(c, bufs[u % 2], m_pend)
            else:
                absorb(c, bufs[0], m_pend)
                m_next = scores(jnp.minimum(c + 1, n_chunks - 1), bufs[0])
            m_pend = m_next
        return m_pend

    n_trips = jnp.where(pl.program_id(2) < n_lat_q_blocks, n_chunks // unroll, 0)

    @pl.when(pl.program_id(2) == 0)
    def _():
        r_i = lax.broadcasted_iota(jnp.int32, (LANES, LANES), 0)
        c_i = lax.broadcasted_iota(jnp.int32, (LANES, LANES), 1)
        same_map = jnp.where((r_i // DA_QK) == (c_i // DA_QK), 1.0, 0.0).astype(BF16)

        def max_sq_norm(k):
            kf = k.astype(F32)
            hi, lo = _split(kf * kf)
            return jnp.max(_dot(hi, same_map) + _dot(lo, same_map), axis=0, keepdims=True)

        def body_norm(c, best):
            return jnp.maximum(best, max_sq_norm(kl_ref[chunk(c), :]))
        kn_ref[...] = lax.fori_loop(0, n_chunks, body_norm, max_sq_norm(kc_ref[...]))

    qf = q2t.astype(F32)
    q_sq = jnp.sum(qf * qf, axis=0, keepdims=True)
    col = lax.broadcasted_iota(jnp.int32, q_sq.shape, 1)
    k_sq = jnp.where(col < tq, kn_ref[:, 0:1], kn_ref[:, DA_QK:DA_QK + 1])
    bound = jnp.sqrt(q_sq * k_sq) * 1.01 + 0.01
    bounded = jnp.max(bound - m0) <= ATTN_RANGE_MAX

    def body_bounded(j, carry):
        for u in range(unroll):
            c = j * unroll + u
            st = _dot(kl_ref[chunk(c), :], q2t)
            acc_ref[...] += weighted_values(vlt_ref[:, chunk(c)], st, m0)
        return carry

    @pl.when(bounded)
    def _():
        lax.fori_loop(0, n_trips, body_bounded, 0)

    @pl.when(jnp.logical_not(bounded))
    def _():
        lax.fori_loop(0, n_trips, body, scores(0, bufs[0]))

    lv = lam_ref[...]
    lam = (jnp.exp(jnp.sum(lv[0:1] * lv[1:2], axis=-1, keepdims=True))
           - jnp.exp(jnp.sum(lv[2:3] * lv[3:4], axis=-1, keepdims=True)) + lam_init)
    o = acc_ref[0:HEAD_W, :] / acc_ref[HEAD_W:HEAD_W + 1, :]
    a = o[:, :tq] - lam * o[:, tq:]
    ms = jnp.mean(a * a, axis=0, keepdims=True)
    y = a * lax.rsqrt(ms + EPS) * sn_ref[...] * (1.0 - lam_init)
    o_ref[...] = y.T.astype(o_ref.dtype)


def _attention_call(rows, qt, kh, vt, lam_vec, sub_gain, lam_init, n_rows, prev):
    assert rows.nct == 1
    ctx_blk = lambda b: rows.batch * rows.nlt + b
    tk = _pick(rows.seq, (512, 256))
    if prev is None:
        tq = _pick(rows.seq, (512, 256))
        n_q = rows.seq // tq
        q_blk = lambda b, i: b * n_q + i
        kern = functools.partial(_attn_kernel, lam_init, n_q, rows.seq, tk)
        extra_specs, extra_args, aliases = [], [], {}
    else:
        tq, n_q = ROW_BLK, 0
        q_blk = lambda b, i: ctx_blk(b)
        attn = functools.partial(_attn_kernel, lam_init, n_q, rows.seq, tk)
        kern = lambda *refs: attn(*refs[:7], *refs[8:])
        extra_specs, extra_args, aliases = [pl.BlockSpec(memory_space=pl.ANY)], [prev], {7: 0}
    return pl.pallas_call(
        kern,
        grid=(rows.batch, HEADS, max(n_q, 1)),
        in_specs=[pl.BlockSpec((HEAD_W, tq), lambda b, h, i: (h, q_blk(b, i))),
                  pl.BlockSpec((ROW_BLK, HEAD_W), lambda b, h, i: (ctx_blk(b), h)),
                  pl.BlockSpec((HEAD_W, ROW_BLK), lambda b, h, i: (h, ctx_blk(b))),
                  pl.BlockSpec((rows.seq, HEAD_W), lambda b, h, i: (b, h)),
                  pl.BlockSpec((HEAD_W, rows.seq), lambda b, h, i: (h, b)),
                  pl.BlockSpec(lam_vec.shape, lambda b, h, i: (0, 0)),
                  pl.BlockSpec((HEAD_W, 1), lambda b, h, i: (0, 0))] + extra_specs,
        out_specs=pl.BlockSpec((tq, HEAD_W), lambda b, h, i: (q_blk(b, i), h)),
        out_shape=jax.ShapeDtypeStruct((n_rows, SEG), BF16),
        scratch_shapes=[pltpu.VMEM((HEAD_W + ATTN_PAD_ROWS, 2 * tq), F32),
                        pltpu.VMEM((1, 2 * tq), F32), pltpu.VMEM((tk, 2 * tq), F32),
                        pltpu.VMEM((tk, 2 * tq), F32), pltpu.VMEM((1, LANES), F32)],
        input_output_aliases=aliases,
        compiler_params=_cp(("arbitrary", "arbitrary", "arbitrary")),
        name="diff_attention" if prev is None else "diff_attention_ctx",
    )(qt, kh, vt, kh, vt, lam_vec, sub_gain, *extra_args)


def _attention(rows, qt, kh, vt, lam_vec, sub_gain, lam_init, n_rows):
    da = _attention_call(rows, qt, kh, vt, lam_vec, sub_gain, lam_init, n_rows, None)
    if n_rows > rows.n_lat:
        da = _attention_call(rows, qt, kh, vt, lam_vec, sub_gain, lam_init, n_rows, da)
    return da


def _merge_kernel(ohf_ref, ohb_ref, hg_ref, da_ref, omf_ref, omb_ref, mo_ref, g0_ref, g1_ref,
                  g2_ref, hn_ref, mn_ref, wb_ref, o_ref, h_ref):
    @pl.when(pl.program_id(1) == 0)
    def _():
        for h in range(HEADS):
            cols = slice(h * HEAD_W, (h + 1) * HEAD_W)
            o = ohf_ref[:, cols].astype(F32) + ohb_ref[:, cols].astype(F32)
            y = o * lax.rsqrt(jnp.mean(o * o, axis=-1, keepdims=True) + EPS) * hn_ref[...]
            g = hg_ref[:, cols].astype(F32)
            h_ref[0, :, cols] = (y * (g * _sigmoid(g))).astype(BF16)
            o = omf_ref[:, cols].astype(F32) + omb_ref[:, cols].astype(F32)
            y = o * lax.rsqrt(jnp.mean(o * o, axis=-1, keepdims=True) + EPS) * mn_ref[...]
            h_ref[2, :, cols] = (y * _sigmoid(mo_ref[:, cols].astype(F32))).astype(BF16)
        h_ref[1] = da_ref[...]

    y = _sigmoid(g0_ref[...].astype(F32)) * _dot(h_ref[0], wb_ref[0])
    y = y + _sigmoid(g1_ref[...].astype(F32)) * _dot(h_ref[1], wb_ref[1])
    y = y + _sigmoid(g2_ref[...].astype(F32)) * _dot(h_ref[2], wb_ref[2])
    o_ref[...] = y.astype(o_ref.dtype)


def _merge(p_main, p_merge, ohf, ohb, da, omf, omb, hg_gain, ml_gain, w_branch, d):
    n_all = da.shape[0]
    tm = 512
    tn = _pick(d, (512, 256, 128))
    row = lambda i, j: (i, 0)

    def gate_spec(jj):
        return pl.BlockSpec((tm, tn), lambda i, j: (i, jj * (d // tn) + j))

    return pl.pallas_call(
        _merge_kernel,
        grid=(n_all // tm, d // tn),
        in_specs=[pl.BlockSpec((tm, SEG), row), pl.BlockSpec((tm, SEG), row),
                  pl.BlockSpec((tm, SEG), lambda i, j: (i, S_HG_G)),
                  pl.BlockSpec((tm, SEG), row),
                  pl.BlockSpec((tm, SEG), row), pl.BlockSpec((tm, SEG), row),
                  pl.BlockSpec((tm, SEG), lambda i, j: (i, S_ML_O)),
                  gate_spec(0), gate_spec(1), gate_spec(2),
                  pl.BlockSpec((1, LANES), lambda i, j: (0, 0)),
                  pl.BlockSpec((1, LANES), lambda i, j: (0, 0)),
                  pl.BlockSpec((3, SEG, tn), lambda i, j: (0, 0, j))],
        out_specs=pl.BlockSpec((tm, tn), lambda i, j: (i, j)),
        out_shape=jax.ShapeDtypeStruct((n_all, d), BF16),
        scratch_shapes=[pltpu.VMEM((3, tm, SEG), BF16)],
        compiler_params=_cp(("arbitrary", "arbitrary")),
        name="branch_merge",
    )(ohf, ohb, p_main, da, omf, omb, p_main, p_merge, p_merge, p_merge, hg_gain, ml_gain,
      w_branch)


def _proj_resid_kernel(n_first, a_ref, w_ref, g_ref, *refs):
    x_parts, o_ref = refs[:-1], refs[-1]
    x = _row_tile(x_parts, pl.program_id(1), n_first)
    o_ref[...] = x + g_ref[0] * _dot(a_ref[...], w_ref[...])


def _proj_resid(rows, a, w, x_all, modsflat, layer, which_gate):
    n_all, k = a.shape
    d = w.shape[1]
    tm = 512
    tn = _pick(d, (1024, 512, 256, 128))
    x_specs, x_args, n_first = _split_rows(x_all, (tm, tn), 1)
    return pl.pallas_call(
        functools.partial(_proj_resid_kernel, n_first),
        grid=(d // tn, n_all // tm),
        in_specs=[pl.BlockSpec((tm, k), lambda j, i: (i, 0)),
                  pl.BlockSpec((k, tn), lambda j, i: (0, j)),
                  pl.BlockSpec((1, 1, tn),
                               lambda j, i: (rows.mod_row(layer, i, tm, which_gate), 0, j))]
        + x_specs,
        out_specs=pl.BlockSpec((tm, tn), lambda j, i: (i, j)),
        out_shape=jax.ShapeDtypeStruct((n_all, d), F32),
        compiler_params=_cp(("arbitrary", "arbitrary")),
        name="out_proj_residual",
    )(a, w, modsflat, *x_args)


def _swiglu_hidden(x, w1_ref, w3_ref):
    h1 = _dot(x, w1_ref[...])
    h3 = _dot(x, w3_ref[...])
    return (h1 * _sigmoid(h1)) * h3


def _shared_expert_kernel(tok_ref, w1_ref, w3_ref, w2_ref, o_ref):
    a = _swiglu_hidden(tok_ref[...], w1_ref, w3_ref)
    o_ref[...] = _dot(a.astype(BF16), w2_ref[...]).astype(o_ref.dtype)


def _shared_expert(tok, w1, w3, w2):
    n_all, d = tok.shape
    de = w1.shape[1]
    tm = 512
    full = lambda i: (0, 0)
    return pl.pallas_call(
        _shared_expert_kernel,
        grid=(n_all // tm,),
        in_specs=[pl.BlockSpec((tm, d), lambda i: (i, 0)),
                  pl.BlockSpec((d, de), full), pl.BlockSpec((d, de), full),
                  pl.BlockSpec((de, d), full)],
        out_specs=pl.BlockSpec((tm, d), lambda i: (i, 0)),
        out_shape=jax.ShapeDtypeStruct((n_all, d), BF16),
        compiler_params=_cp(("arbitrary",)),
        name="moe_shared_expert",
    )(tok, w1, w3, w2)


def _moe_plan_kernel(route_ref, pos_ref, cnt_ref, carry_ref):
    @pl.when(pl.program_id(0) == 0)
    def _():
        carry_ref[...] = jnp.zeros_like(carry_ref)

    r = route_ref[...]
    tm = r.shape[0]
    lane = lax.broadcasted_iota(jnp.int32, r.shape, 1).astype(F32)
    mask = jnp.zeros_like(r)
    for k in range(TOP_K):
        mask = mask + jnp.where(lane == r[:, k:k + 1], 1.0, 0.0)
    r_i = lax.broadcasted_iota(jnp.int32, (tm, tm), 0)
    c_i = lax.broadcasted_iota(jnp.int32, (tm, tm), 1)
    before = jnp.where(c_i < r_i, 1.0, 0.0).astype(BF16)
    rank = _dot(before, mask.astype(BF16)) + carry_ref[...]
    out = jnp.zeros_like(r)
    for k in range(TOP_K):
        pk = jnp.sum(jnp.where(lane == r[:, k:k + 1], rank, 0.0), axis=-1, keepdims=True)
        out = jnp.where(lane == k, pk, out)
    pos_ref[...] = out
    carry = carry_ref[...] + jnp.sum(mask, axis=0, keepdims=True)
    carry_ref[...] = carry
    cnt_ref[...] = jnp.broadcast_to(carry, cnt_ref.shape)


def _moe_plan(route):
    n = route.shape[0]
    tm = ROW_BLK
    return pl.pallas_call(
        _moe_plan_kernel,
        grid=(n // tm,),
        in_specs=[pl.BlockSpec((tm, LANES), lambda i: (i, 0))],
        out_specs=[pl.BlockSpec((tm, LANES), lambda i: (i, 0)),
                   pl.BlockSpec((8, LANES), lambda i: (0, 0))],
        out_shape=[jax.ShapeDtypeStruct((n, LANES), F32), jax.ShapeDtypeStruct((8, LANES), F32)],
        scratch_shapes=[pltpu.VMEM((1, LANES), F32)],
        compiler_params=_cp(("arbitrary",)),
        name="moe_plan",
    )(route)


def _moe_routed_kernel(ns, te_ref, na_ref, st_ref, tok_hbm, sw_ref, w1_ref, w3_ref, w2_ref, o_ref,
                       xbuf, x_ref, sem):
    tm = MOE_TILE
    t = pl.program_id(0)
    n_active = na_ref[0]
    par = t % 2

    def gather_copy(tile, buf, r):
        tok = st_ref[tile * tm + r]
        return pltpu.make_async_copy(
            tok_hbm.at[pl.ds(pl.multiple_of(tok * ns, ns), ns), :],
            xbuf.at[buf, pl.ds(pl.multiple_of(r * ns, ns), ns), :], sem.at[buf])

    def for_each_copy(tile, buf, act):
        def body(r4, carry):
            for u in range(4):
                act(gather_copy(tile, buf, r4 * 4 + u), u)
            return carry
        lax.fori_loop(0, tm // 4, body, 0)

    start = lambda cp, u: cp.start(priority=u % 2)

    @pl.when(t == 0)
    def _():
        for_each_copy(0, 0, start)

    @pl.when(t + 1 < n_active)
    def _():
        for_each_copy(t + 1, 1 - par, start)

    @pl.when(t < n_active)
    def _():
        for_each_copy(t, par, lambda cp, u: cp.wait())
        d = x_ref.shape[1]

        def emit(s, lo, hi):
            x_ref[:, s * LANES:(s + 1) * LANES] = lo.astype(BF16)
            x_ref[:, d // 2 + s * LANES:d // 2 + (s + 1) * LANES] = hi.astype(BF16)
        _load_slabs(xbuf.at[par], 0, tm, ns, emit)
        a = _swiglu_hidden(x_ref[...], w1_ref, w3_ref) * sw_ref[...]
        _store_slabs(_dot(a.astype(BF16), w2_ref[...]), o_ref)

    @pl.when(t >= n_active)
    def _():
        o_ref[...] = jnp.zeros_like(o_ref)


def _moe_routed(tok_slabs, slot_token, slot_w, tile_expert, n_active, w1, w3, w2, layer):
    _, n_exp, d, de = w1.shape
    ns = d // (2 * LANES)
    tm = MOE_TILE
    n_slots = slot_token.shape[0]
    nt = n_slots // tm
    grid_spec = pltpu.PrefetchScalarGridSpec(
        num_scalar_prefetch=3,
        grid=(nt,),
        in_specs=[pl.BlockSpec(memory_space=pl.ANY),
                  pl.BlockSpec((tm, 1), lambda t, te, na, st: (t, 0)),
                  pl.BlockSpec((None, None, d, de), lambda t, te, na, st: (layer, te[t], 0, 0)),
                  pl.BlockSpec((None, None, d, de), lambda t, te, na, st: (layer, te[t], 0, 0)),
                  pl.BlockSpec((None, None, de, d), lambda t, te, na, st: (layer, te[t], 0, 0))],
        out_specs=pl.BlockSpec((tm * ns, LANES), lambda t, te, na, st: (t, 0)),
        scratch_shapes=[pltpu.VMEM((2, tm * ns, LANES), jnp.int32), pltpu.VMEM((tm, d), BF16),
                        pltpu.SemaphoreType.DMA((2,))],
    )
    return pl.pallas_call(
        functools.partial(_moe_routed_kernel, ns),
        grid_spec=grid_spec,
        out_shape=jax.ShapeDtypeStruct((n_slots * ns, LANES), jnp.int32),
        compiler_params=_cp(("arbitrary",)),
        name="moe_routed_experts",
    )(tile_expert, n_active, slot_token, tok_slabs, slot_w, w1, w3, w2)


def _moe_combine_kernel(ns, s4_ref, ys_hbm, ysh_ref, x_ref, g_ref, o_ref, ybuf, sem):
    tm = x_ref.shape[0]
    d = x_ref.shape[1]
    i = pl.program_id(0)
    par = i % 2

    def gather_copy(tile, buf, row, k):
        slot = s4_ref[(tile * tm + row) * TOP_K + k]
        return pltpu.make_async_copy(
            ys_hbm.at[pl.ds(pl.multiple_of(slot * ns, ns), ns), :],
            ybuf.at[buf, pl.ds(pl.multiple_of((k * tm + row) * ns, ns), ns), :], sem.at[buf])

    def for_each_copy(tile, buf, act):
        def body(row, carry):
            for k in range(TOP_K):
                act(gather_copy(tile, buf, row, k), k)
            return carry
        lax.fori_loop(0, tm, body, 0)

    start = lambda cp, k: cp.start(priority=k % 2)

    @pl.when(i == 0)
    def _():
        for_each_copy(0, 0, start)

    @pl.when(i + 1 < pl.num_programs(0))
    def _():
        for_each_copy(i + 1, 1 - par, start)

    for_each_copy(i, par, lambda cp, k: cp.wait())

    gate = g_ref[0]
    for s in range(ns):
        lo_cols = slice(s * LANES, (s + 1) * LANES)
        hi_cols = slice(d // 2 + s * LANES, d // 2 + (s + 1) * LANES)
        y_lo = ysh_ref[:, lo_cols].astype(F32)
        y_hi = ysh_ref[:, hi_cols].astype(F32)
        for k in range(TOP_K):
            lo, hi = _unpack_pair(ybuf[par, pl.ds(k * tm * ns + s, tm, stride=ns), :])
            y_lo = y_lo + lo
            y_hi = y_hi + hi
        o_ref[:, lo_cols] = x_ref[:, lo_cols] + gate[:, lo_cols] * y_lo
        o_ref[:, hi_cols] = x_ref[:, hi_cols] + gate[:, hi_cols] * y_hi


def _moe_combine(rows, x_all, y_shared, ys_slabs, slot4, modsflat, layer, which_gate, n_rows):
    d = x_all.shape[1]
    ns = d // (2 * LANES)
    tm = 128
    grid_spec = pltpu.PrefetchScalarGridSpec(
        num_scalar_prefetch=1,
        grid=(n_rows // tm,),
        in_specs=[pl.BlockSpec(memory_space=pl.ANY),
                  pl.BlockSpec((tm, d), lambda i, s4: (i, 0)),
                  pl.BlockSpec((tm, d), lambda i, s4: (i, 0)),
                  pl.BlockSpec((1, 1, d),
                               lambda i, s4: (rows.mod_row(layer, i, tm, which_gate), 0, 0))],
        out_specs=pl.BlockSpec((tm, d), lambda i, s4: (i, 0)),
        scratch_shapes=[pltpu.VMEM((2, TOP_K * tm * ns, LANES), jnp.int32),
                        pltpu.SemaphoreType.DMA((2,))],
    )
    return pl.pallas_call(
        functools.partial(_moe_combine_kernel, ns),
        grid_spec=grid_spec,
        out_shape=jax.ShapeDtypeStruct((n_rows, d), F32),
        compiler_params=_cp(("arbitrary",)),
        name="moe_combine_residual",
    )(slot4, ys_slabs, y_shared, x_all, modsflat)


def _moe_slots(route, pos, counts, n_exp):
    n = route.shape[0]
    tm = MOE_TILE
    idx4 = route[:, :TOP_K].astype(jnp.int32)
    w4 = route[:, TOP_K:2 * TOP_K]
    pos4 = pos[:, :TOP_K].astype(jnp.int32)
    cnt = counts[0, :n_exp].astype(jnp.int32)
    padded = (cnt + tm - 1) // tm * tm
    ends = jnp.cumsum(padded)
    starts = ends - padded
    slot4 = (starts[idx4] + pos4).reshape(-1)
    n_slots = n * TOP_K + n_exp * tm
    nt = n_slots // tm
    tile_start = jnp.arange(nt, dtype=jnp.int32) * tm
    tile_expert = jnp.minimum(
        jnp.sum((ends[None, :] <= tile_start[:, None]).astype(jnp.int32), axis=1), n_exp - 1)
    n_active = (ends[-1:] // tm).astype(jnp.int32)
    token_of = jnp.repeat(jnp.arange(n, dtype=jnp.int32), TOP_K)
    pairs = jnp.stack([token_of, lax.bitcast_convert_type(w4.reshape(-1), jnp.int32)], axis=1)
    slot_meta = jnp.zeros((n_slots, 2), jnp.int32).at[slot4].set(pairs)
    slot_token = slot_meta[:, 0]
    slot_w = lax.bitcast_convert_type(slot_meta[:, 1:2], F32)
    return slot4, slot_token, slot_w, tile_expert, n_active


def _rope_tables(seq, ctx_len):
    n = DA_QK // 4
    t = jnp.arange(seq)
    inv = ROPE_THETA ** (-jnp.arange(n, dtype=F32) / n)
    row = (t // GRID_W).astype(F32)
    col = (t % GRID_W).astype(F32)
    ang = jnp.concatenate([row[:, None] * inv, col[:, None] * inv], axis=-1)
    ang = jnp.concatenate([ang, jnp.zeros((ctx_len, 2 * n), F32)], axis=0)
    return jnp.tile(jnp.cos(ang), (1, 4)), jnp.tile(jnp.sin(ang), (1, 4))


def kernel(x, c, ctx, c_ctx, norm1, norm2, w_ada, b_ada, w_in, hg_lb_logits, hg_norm, da_q_norm,
           da_k_norm, da_lambda, da_sub_norm, ml_igate_bias, ml_fgate_bias, ml_norm, w_branch,
           w_out, router_w, router_bias, exp_w1, exp_w3, exp_w2, sh_w1, sh_w3, sh_w2):
    batch, seq, d = x.shape
    ctx_len = ctx.shape[1]
    depth = w_ada.shape[0]
    n_exp = router_w.shape[-1]
    rows = _Rows(batch, seq, ctx_len)
    assert batch + 1 <= 16 and n_exp <= LANES

    x_all = (x.reshape(batch * seq, d), ctx.reshape(batch * ctx_len, d))
    cvec = jnp.zeros((16, d), F32).at[0].set(c_ctx).at[1:1 + batch].set(c)
    mods = _mods(cvec, w_ada, b_ada)
    modsflat = mods[:, :rows.n_groups].reshape(depth * rows.n_groups * 6, 1, d)
    cos_tab, sin_tab = _rope_tables(seq, ctx_len)
    gate_lo = N_SEG * SEG
    w_in_t = jnp.swapaxes(w_in, 1, 2)
    exp_w1_b, exp_w3_b, exp_w2_b = (w.astype(BF16) for w in (exp_w1, exp_w3, exp_w2))

    for l in range(depth):
        n_rows = rows.n_all if l < depth - 1 else rows.n_lat
        lam_init = 0.8 - 0.6 * math.exp(-0.3 * l)
        hx = _norm(rows, x_all, rows.n_all, norm1[l], modsflat, l, 1, 0)
        p_main = _matmul_wt(hx, w_in_t, l, 0, gate_lo, BF16)
        p_gate = _matmul_wt(hx, w_in_t, l, gate_lo, LANES, F32)
        p_merge = _matmul_wt(hx, w_in_t, l, gate_lo + N_GATE_COLS, N_BRANCH * d, BF16, m=n_rows)
        gates_t = p_gate[:, :N_GATE_COLS].T.reshape(N_GATE_COLS, 1, rows.n_all)

        ohf, ohb = _hgrn_scan(rows, p_main, hg_lb_logits, l)

        tile2 = lambda g: jnp.tile(g.reshape(1, DA_QK), (1, 2))
        qt, kh, vt = _da_prep(rows, p_main, cos_tab, sin_tab, tile2(da_q_norm[l]),
                              tile2(da_k_norm[l]))
        da = _attention(rows, qt, kh, vt, da_lambda[l], da_sub_norm[l].reshape(HEAD_W, 1),
                        lam_init, n_rows)

        bias = lambda bv: jnp.broadcast_to(bv.reshape(2 * HEADS, 1, 1), (2 * HEADS, 1, LANES))
        i_b, f_b = bias(ml_igate_bias[l]), bias(ml_fgate_bias[l])
        omf, omb = _mlstm_scan(rows, p_main, gates_t, i_b, f_b)

        ymid = _merge(p_main, p_merge, ohf, ohb, da, omf, omb, hg_norm[l].reshape(1, HEAD_W),
                      ml_norm[l].reshape(1, HEAD_W), w_branch[l].astype(BF16), d)
        x_all = _proj_resid(rows, ymid, w_out[l].astype(BF16), x_all, modsflat, l, 2)

        rw = jnp.pad(router_w[l], ((0, 0), (0, LANES - n_exp)))
        rw_hi = rw.astype(BF16)
        rw_lo = (rw - rw_hi.astype(F32)).astype(BF16)
        rb = jnp.pad(router_bias[l], (0, LANES - n_exp)).reshape(1, LANES)
        tok, tok_slabs, route = _norm(rows, x_all, n_rows, norm2[l], modsflat, l, 4, 3,
                                      router=(rw_hi, rw_lo, rb, n_exp))
        pos, counts = _moe_plan(route)
        slot4, slot_token, slot_w, tile_expert, n_active = _moe_slots(route, pos, counts, n_exp)
        ys = _moe_routed(tok_slabs, slot_token, slot_w, tile_expert, n_active,
                         exp_w1_b, exp_w3_b, exp_w2_b, l)
        y_sh = _shared_expert(tok, sh_w1[l].astype(BF16), sh_w3[l].astype(BF16),
                              sh_w2[l].astype(BF16))
        x_all = _moe_combine(rows, x_all, y_sh, ys, slot4, modsflat, l, 5, n_rows)

    return x_all.reshape(batch, seq, d)
```
